```python
import jax, jax.numpy as jnp
from jax import lax
import numpy as np

D_MODEL = 1024
BATCH = 8
SEQ = 4096
DEPTH = 1

ATTN_WIDTH = D_MODEL // 2
HGRN_WIDTH = D_MODEL - ATTN_WIDTH
MIX_WIDTH = ATTN_WIDTH + HGRN_WIDTH
ATTN_HEAD_DIM = 64
ATTN_HEADS = ATTN_WIDTH // ATTN_HEAD_DIM
HGRN_EXPAND = 128
HGRN_HEADS = HGRN_WIDTH // HGRN_EXPAND
DILATED_PATTERNS = ((128, 1), (512, 4), (2048, 16))
ATTN_BLOCK = 128
ROPE_THETA = 500000.0
ROPE_DIMS = ATTN_HEAD_DIM // 4
HGRN_CHUNK = 64
NORM_EPS = 1e-6
IN_COLS = 4 * ATTN_WIDTH + 4 * HGRN_WIDTH

kernel_name = "hymba_dilated_attn_hgrn2_hybrid"


def _rmsnorm(x, w):
    xf = x.astype(jnp.float32)
    y = xf * lax.rsqrt(jnp.mean(xf * xf, axis=-1, keepdims=True) + NORM_EPS)
    return (y * w.astype(jnp.float32)).astype(x.dtype)


def _head_rmsnorm(o, w, n_heads):
    b, s, width = o.shape
    oh = o.reshape(b, s, n_heads, width // n_heads)
    oh = oh * lax.rsqrt(jnp.mean(oh * oh, axis=-1, keepdims=True) + NORM_EPS)
    return oh.reshape(b, s, width) * w.astype(jnp.float32)


def _partial_rotary(t, positions):
    half = ROPE_DIMS // 2
    inv_freq = ROPE_THETA ** (-jnp.arange(half, dtype=jnp.float32) * (2.0 / ROPE_DIMS))
    ang = positions.astype(jnp.float32)[..., None] * inv_freq
    cos = jnp.cos(ang)[:, :, None, :]
    sin = jnp.sin(ang)[:, :, None, :]
    t = t.astype(jnp.float32)
    t1, t2, rest = t[..., :half], t[..., half:ROPE_DIMS], t[..., ROPE_DIMS:]
    return jnp.concatenate([t1 * cos - t2 * sin, t2 * cos + t1 * sin, rest], axis=-1)


def _dilated_window_attn(q, k, v, window, dilation):
    b, h, s, e = q.shape
    span = window // dilation
    sub_len = s // dilation
    n_blk = -(-sub_len // ATTN_BLOCK)
    pad = n_blk * ATTN_BLOCK - sub_len

    def to_blocks(t):
        t = t.reshape(b, h, sub_len, dilation, e).transpose(0, 1, 3, 2, 4)
        t = jnp.pad(t, ((0, 0), (0, 0), (0, 0), (0, pad), (0, 0)))
        return t.reshape(b, h, dilation, n_blk, ATTN_BLOCK, e)

    def with_prev(t):
        prev = jnp.concatenate([jnp.zeros_like(t[:, :, :, :1]), t[:, :, :, :-1]], axis=3)
        return jnp.concatenate([prev, t], axis=4)

    qb, kb, vb = to_blocks(q), to_blocks(k), to_blocks(v)
    kw, vw = with_prev(kb), with_prev(vb)
    scores = jnp.einsum('bhrnqe,bhrnke->bhrnqk', qb, kw) * (e ** -0.5)
    qi = jnp.arange(ATTN_BLOCK)[:, None]
    kj = jnp.arange(2 * ATTN_BLOCK)[None, :]
    dist = ATTN_BLOCK + qi - kj
    band = (dist >= 0) & (dist <= span)
    first = (jnp.arange(n_blk) == 0)[:, None, None]
    mask = band[None] & ~(first & (kj < ATTN_BLOCK)[None])
    scores = jnp.where(mask, scores, -jnp.inf)
    m = jnp.max(scores, axis=-1)
    p = jnp.exp(scores - m[..., None])
    l = jnp.sum(p, axis=-1)
    o = jnp.einsum('bhrnqk,bhrnke->bhrnqe', p, vw)

    def from_blocks(t):
        tail = t.shape[5:]
        t = t.reshape(b, h, dilation, n_blk * ATTN_BLOCK, *tail)[:, :, :, :sub_len]
        t = jnp.moveaxis(t, 2, 3)
        return t.reshape(b, h, s, *tail)

    return from_blocks(o), from_blocks(m), from_blocks(l)


def _longnet_mixture(q, k, v):
    outs = [_dilated_window_attn(q, k, v, w, d) for (w, d) in DILATED_PATTERNS]
    m_all = jnp.stack([o[1] for o in outs], axis=0)
    m_top = jnp.max(m_all, axis=0)
    wts = jnp.exp(m_all - m_top)
    num = sum(wts[i][..., None] * outs[i][0] for i in range(len(outs)))
    den = sum(wts[i] * outs[i][2] for i in range(len(outs)))
    return num / den[..., None]


def _hgrn2_chunked(q, k, v, log_f):
    b, h, s, e = q.shape
    ev = v.shape[-1]
    nc = s // HGRN_CHUNK
    rs = lambda t: t.reshape(b, h, nc, HGRN_CHUNK, t.shape[-1])
    q, k, v, log_f = rs(q), rs(k), rs(v), rs(log_f)
    cum = jnp.cumsum(log_f, axis=3)
    last = cum[:, :, :, -1:]
    q_dec = q * jnp.exp(cum)
    k_inv = k * jnp.exp(-cum)
    k_end = k * jnp.exp(last - cum)
    causal = jnp.tril(jnp.ones((HGRN_CHUNK, HGRN_CHUNK), dtype=bool))
    att = jnp.where(causal, jnp.einsum('bhnte,bhnse->bhnts', q_dec, k_inv), 0.0)
    o_intra = jnp.einsum('bhnts,bhnsv->bhntv', att, v)
    chunk_decay = jnp.exp(last[:, :, :, 0])

    def step(state, xs):
        qd, ke, vc, dec = xs
        o = jnp.einsum('bhte,bhev->bhtv', qd, state)
        state = dec[..., None] * state + jnp.einsum('bhte,bhtv->bhev', ke, vc)
        return state, o

    xs = (jnp.moveaxis(q_dec, 2, 0), jnp.moveaxis(k_end, 2, 0),
          jnp.moveaxis(v, 2, 0), jnp.moveaxis(chunk_decay, 2, 0))
    state0 = jnp.zeros((b, h, e, ev), jnp.float32)
    _, o_inter = lax.scan(step, state0, xs)
    o = o_intra + jnp.moveaxis(o_inter, 0, 2)
    return o.reshape(b, h, s, ev)


def setup_inputs(seed: int = 0) -> dict:
    key = jax.random.key(seed)
    ks = jax.random.split(key, 10)
    x = jax.random.normal(ks[0], (BATCH, SEQ, D_MODEL), jnp.float32)
    offset = jax.random.randint(ks[1], (BATCH, 1), 0, 4096, dtype=jnp.int32)
    positions = (offset + jnp.arange(SEQ, dtype=jnp.int32)[None, :]).astype(jnp.int32)
    w_in = jax.random.normal(ks[2], (DEPTH, D_MODEL, IN_COLS), jnp.float32) * D_MODEL ** -0.5
    w_out = jax.random.normal(ks[3], (DEPTH, MIX_WIDTH, D_MODEL), jnp.float32) * MIX_WIDTH ** -0.5
    mix_norm_w = 1.0 + 0.02 * jax.random.normal(ks[4], (DEPTH, D_MODEL), jnp.float32)
    attn_out_norm_w = 1.0 + 0.02 * jax.random.normal(ks[5], (DEPTH, ATTN_WIDTH), jnp.float32)
    hgrn_out_norm_w = 1.0 + 0.02 * jax.random.normal(ks[6], (DEPTH, HGRN_WIDTH), jnp.float32)
    hgrn_lb_raw = 0.1 * jax.random.normal(ks[7], (DEPTH + 1, HGRN_WIDTH), jnp.float32)
    final_norm_w = 1.0 + 0.02 * jax.random.normal(ks[8], (D_MODEL,), jnp.float32)
    return {"x": x, "positions": positions, "w_in": w_in, "w_out": w_out,
            "mix_norm_w": mix_norm_w, "attn_out_norm_w": attn_out_norm_w,
            "hgrn_out_norm_w": hgrn_out_norm_w, "hgrn_lb_raw": hgrn_lb_raw,
            "final_norm_w": final_norm_w}


def reference(x, positions, w_in, w_out, mix_norm_w, attn_out_norm_w,
              hgrn_out_norm_w, hgrn_lb_raw, final_norm_w):
    b, s, _ = x.shape
    f32 = jnp.float32
    lower_bounds = jnp.cumsum(jax.nn.softmax(hgrn_lb_raw.astype(f32), axis=0), axis=0)
    split_at = [ATTN_WIDTH * i for i in range(1, 5)] + \
               [4 * ATTN_WIDTH + HGRN_WIDTH * i for i in range(1, 4)]
    for layer in range(DEPTH):
        hn = _rmsnorm(x, mix_norm_w[layer])
        proj = hn @ w_in[layer]
        aq, ak, av, ag, hq, hf, hi, hg = jnp.split(proj, split_at, axis=-1)

        aq = _partial_rotary(aq.reshape(b, s, ATTN_HEADS, ATTN_HEAD_DIM), positions)
        ak = _partial_rotary(ak.reshape(b, s, ATTN_HEADS, ATTN_HEAD_DIM), positions)
        av = av.reshape(b, s, ATTN_HEADS, ATTN_HEAD_DIM).astype(f32)
        bhse = lambda t: t.transpose(0, 2, 1, 3)
        attn = _longnet_mixture(bhse(aq), bhse(ak), bhse(av))
        attn = attn.transpose(0, 2, 1, 3).reshape(b, s, ATTN_WIDTH)

        lb = lower_bounds[layer]
        f = lb + (1.0 - lb) * jax.nn.sigmoid(hf.astype(f32))
        hkey = 1.0 - f
        hquery = jax.nn.silu(hq.astype(f32))
        hh = lambda t: t.reshape(b, s, HGRN_HEADS, HGRN_EXPAND).transpose(0, 2, 1, 3)
        rec = _hgrn2_chunked(hh(hquery), hh(hkey), hh(hi.astype(f32)), hh(jnp.log(f)))
        rec = rec.transpose(0, 2, 1, 3).reshape(b, s, HGRN_WIDTH)

        y_attn = _head_rmsnorm(attn, attn_out_norm_w[layer], ATTN_HEADS) * jax.nn.silu(ag.astype(f32))
        y_hgrn = _head_rmsnorm(rec, hgrn_out_norm_w[layer], HGRN_HEADS) * jax.nn.silu(hg.astype(f32))
        mixed = jnp.concatenate([y_attn, y_hgrn], axis=-1).astype(x.dtype)
        x = x + mixed @ w_out[layer]
    return _rmsnorm(x, final_norm_w)
```

```python
import functools

import numpy as np
import jax
import jax.numpy as jnp
from jax import lax
from jax.experimental import pallas as pl
from jax.experimental.pallas import tpu as pltpu

F32 = jnp.float32
BF16 = jnp.bfloat16

LANES = 128
ATTN_HEAD_DIM = 64
HGRN_HEAD_DIM = 128
DILATED_PATTERNS = ((128, 1), (512, 4), (2048, 16))
ATTN_BLOCK = 128
ROPE_THETA = 500000.0
ROPE_DIMS = ATTN_HEAD_DIM // 4
HGRN_CHUNK = 64
NORM_EPS = 1e-6
VMEM_LIMIT = 56 * 1024 * 1024

PROJ_ROWS = 512
HGRN_ROWS = 256


def _silu(t):
    return t * jax.nn.sigmoid(t)


def _in_proj_kernel(x_ref, pos_ref, freq_ref, sgn_up_ref, sgn_dn_ref, nw_ref, w_ref,
                    q_ref, k_ref, v_ref, ag_ref, hq_ref, hf_ref, hi_ref, hg_ref, *, width):
    x = x_ref[...]
    ms = jnp.mean(x * x, axis=-1, keepdims=True)
    hn = ((x * lax.rsqrt(ms + NORM_EPS)) * nw_ref[...]).astype(BF16)

    ang = pos_ref[...].astype(F32) * freq_ref[...]
    cos = jnp.cos(ang)
    sin = jnp.sin(ang)
    sgn_up = sgn_up_ref[...]
    sgn_dn = sgn_dn_ref[...]
    half = ROPE_DIMS // 2

    def rope(t):
        cols = []
        for j in range(width // LANES):
            tj = t[:, j * LANES:(j + 1) * LANES]
            up = pltpu.roll(tj, LANES - half, 1)
            dn = pltpu.roll(tj, half, 1)
            cols.append(tj * cos + sin * (up * sgn_up + dn * sgn_dn))
        return jnp.concatenate(cols, axis=1)

    def proj(g):
        return jnp.dot(hn, w_ref[:, g * width:(g + 1) * width], preferred_element_type=F32)

    scale = ATTN_HEAD_DIM ** -0.5
    q_ref[...] = (rope(proj(0)) * scale).astype(BF16)
    k_ref[...] = rope(proj(1)).astype(BF16)
    v_ref[...] = proj(2).astype(BF16)
    ag_ref[...] = _silu(proj(3)).astype(BF16)
    hq_ref[...] = _silu(proj(4)).astype(BF16)
    hf_ref[...] = proj(5).astype(BF16)
    hi_ref[...] = proj(6).astype(BF16)
    hg_ref[...] = _silu(proj(7)).astype(BF16)


def _rope_lane_tables():
    half = ROPE_DIMS // 2
    inv_freq = ROPE_THETA ** (-np.arange(half, dtype=np.float32) * (2.0 / ROPE_DIMS))
    d = np.arange(LANES) % ATTN_HEAD_DIM
    freq = np.where(d < ROPE_DIMS, inv_freq[d % half], 0.0).astype(np.float32)
    sgn_up = np.where(d < half, -1.0, 0.0).astype(np.float32)
    sgn_dn = np.where((d >= half) & (d < ROPE_DIMS), 1.0, 0.0).astype(np.float32)
    return freq[None], sgn_up[None], sgn_dn[None]


def _in_proj(x2, pos2, norm_w, w_bf16, width):
    n, d_model = x2.shape
    rows = PROJ_ROWS
    freq, sgn_up, sgn_dn = _rope_lane_tables()
    row_spec = lambda c: pl.BlockSpec((rows, c), lambda i: (i, 0))
    full_spec = lambda a: pl.BlockSpec(a.shape, lambda i: (0,) * a.ndim)
    out_sds = jax.ShapeDtypeStruct((n, width), BF16)
    return pl.pallas_call(
        functools.partial(_in_proj_kernel, width=width),
        out_shape=[out_sds] * 8,
        grid=(n // rows,),
        in_specs=[row_spec(d_model), row_spec(1), full_spec(freq), full_spec(sgn_up),
                  full_spec(sgn_dn), full_spec(norm_w), full_spec(w_bf16)],
        out_specs=[row_spec(width)] * 8,
        compiler_params=pltpu.CompilerParams(
            dimension_semantics=("parallel",), vmem_limit_bytes=VMEM_LIMIT),
        name="in_proj",
    )(x2, pos2, freq, sgn_up, sgn_dn, norm_w, w_bf16)


def _attn_kernel(q_ref, k_ref, v_ref, bias_ref, bias0_ref, o_ref,
                 qf, kf, vf, oacc, macc, lacc, *, seq, pad):
    blk = ATTN_BLOCK
    qf[...] = q_ref[...].astype(F32)
    kf[0:pad, :] = jnp.zeros((pad, LANES), F32)
    vf[0:pad, :] = jnp.zeros((pad, LANES), F32)
    kf[pad:pad + seq, :] = k_ref[...].astype(F32)
    vf[pad:pad + seq, :] = v_ref[...].astype(F32)

    lane = lax.broadcasted_iota(jnp.int32, (1, LANES), 1)
    head_a = lane < ATTN_HEAD_DIM

    for window, dil in DILATED_PATTERNS:
        assert window // dil == blk
        n_blk = seq // dil // blk

        def body(i, carry, dil=dil, n_blk=n_blk):
            r = i // n_blk
            n = i % n_blk
            q_start = r + dil * blk * n
            k_start = pad + r + dil * blk * (n - 1)
            if dil == 1:
                q_rows = pl.ds(pl.multiple_of(q_start, blk), blk)
                k_rows = pl.ds(pl.multiple_of(k_start, blk), 2 * blk)
            else:
                q_rows = pl.ds(q_start, blk, stride=dil)
                k_rows = pl.ds(k_start, 2 * blk, stride=dil)
            qb = qf[q_rows, :].astype(BF16)
            kb = kf[k_rows, :]
            vb = vf[k_rows, :].astype(BF16)
            bias = jnp.where(n == 0, bias0_ref[...], bias_ref[...])

            def one_head(mask):
                kh = jnp.where(mask, kb, 0.0).astype(BF16)
                s = lax.dot_general(qb, kh, (((1,), (1,)), ((), ())),
                                    preferred_element_type=F32) + bias
                m = jnp.max(s, axis=-1, keepdims=True)
                p = jnp.exp(s - m)
                l = jnp.sum(p, axis=-1, keepdims=True)
                o = jnp.dot(p.astype(BF16), vb, preferred_element_type=F32)
                return o, m, l

            o_a, m_a, l_a = one_head(head_a)
            o_b, m_b, l_b = one_head(jnp.logical_not(head_a))
            o = jnp.where(head_a, o_a, o_b)
            m = jnp.where(head_a, m_a, m_b)
            l = jnp.where(head_a, l_a, l_b)
            if dil == DILATED_PATTERNS[0][1]:
                oacc[q_rows, :] = o
                macc[q_rows, :] = m
                lacc[q_rows, :] = l
            else:
                m_old = macc[q_rows, :]
                m_new = jnp.maximum(m_old, m)
                w_old = jnp.exp(m_old - m_new)
                w_blk = jnp.exp(m - m_new)
                oacc[q_rows, :] = w_old * oacc[q_rows, :] + w_blk * o
                lacc[q_rows, :] = w_old * lacc[q_rows, :] + w_blk * l
                macc[q_rows, :] = m_new
            return carry

        lax.fori_loop(0, dil * n_blk, body, 0)

    o_ref[...] = (oacc[...] / lacc[...]).astype(BF16)


def _band_bias():
    qi = np.arange(ATTN_BLOCK)[:, None]
    kj = np.arange(2 * ATTN_BLOCK)[None, :]
    dist = ATTN_BLOCK + qi - kj
    band = (dist >= 0) & (dist <= ATTN_BLOCK)
    first = band & (kj >= ATTN_BLOCK)
    to_bias = lambda mk: np.where(mk, 0.0, -np.inf).astype(np.float32)
    return to_bias(band), to_bias(first)


def _attention(q, k, v):
    b, seq, width = q.shape
    pad = max(d for _, d in DILATED_PATTERNS) * ATTN_BLOCK
    bias, bias0 = _band_bias()
    col_spec = pl.BlockSpec((None, seq, LANES), lambda i, j: (i, 0, j))
    bias_spec = pl.BlockSpec(bias.shape, lambda i, j: (0, 0))
    return pl.pallas_call(
        functools.partial(_attn_kernel, seq=seq, pad=pad),
        out_shape=jax.ShapeDtypeStruct((b, seq, width), BF16),
        grid=(b, width // LANES),
        in_specs=[col_spec, col_spec, col_spec, bias_spec, bias_spec],
        out_specs=col_spec,
        scratch_shapes=[pltpu.VMEM((seq, LANES), F32),
                        pltpu.VMEM((pad + seq, LANES), F32),
                        pltpu.VMEM((pad + seq, LANES), F32),
                        pltpu.VMEM((seq, LANES), F32),
                        pltpu.VMEM((seq, LANES), F32),
                        pltpu.VMEM((seq, LANES), F32)],
        compiler_params=pltpu.CompilerParams(
            dimension_semantics=("parallel", "parallel"), vmem_limit_bytes=VMEM_LIMIT),
        name="dilated_attn",
    )(q, k, v, bias, bias0)


def _split3(t):
    hi = t.astype(BF16)
    r1 = t - hi.astype(F32)
    mid = r1.astype(BF16)
    lo = (r1 - mid.astype(F32)).astype(BF16)
    return hi, mid, lo


def _hgrn_kernel(hq_ref, hf_ref, hi_ref, lbraw_ref, tri_ref, o_ref, state_ref, *, layer, rows):
    chunk = HGRN_CHUNK
    n_heads = hq_ref.shape[-1] // HGRN_HEAD_DIM

    @pl.when(pl.program_id(1) == 0)
    def _():
        state_ref[...] = jnp.zeros(state_ref.shape, F32)

    raw = lbraw_ref[...]
    e = jnp.exp(raw - jnp.max(raw, axis=0, keepdims=True))
    sm = e / jnp.sum(e, axis=0, keepdims=True)
    lb = jnp.sum(sm[0:layer + 1, :], axis=0, keepdims=True)

    tri = tri_ref[...]
    ti = lax.broadcasted_iota(jnp.int32, (chunk, chunk), 0)
    si = lax.broadcasted_iota(jnp.int32, (chunk, chunk), 1)
    causal = ti >= si

    def chunk_body(c, carry):
        rws = pl.ds(pl.multiple_of(c * chunk, chunk), chunk)
        f = lb + (1.0 - lb) * jax.nn.sigmoid(hf_ref[rws, :].astype(F32))
        hkey = 1.0 - f
        logf = jnp.log(f)
        cum = sum(jnp.dot(tri, part, preferred_element_type=F32) for part in _split3(logf))
        last = cum[chunk - 1:chunk, :]
        q_dec = (hq_ref[rws, :].astype(F32) * jnp.exp(cum)).astype(BF16)
        k_inv = (hkey * jnp.exp(-cum)).astype(BF16)
        k_end = (hkey * jnp.exp(last - cum)).astype(BF16)
        decay = jnp.exp(last)
        v = hi_ref[rws, :]
        outs = []
        for h in range(n_heads):
            cs = slice(h * HGRN_HEAD_DIM, (h + 1) * HGRN_HEAD_DIM)
            att = lax.dot_general(q_dec[:, cs], k_inv[:, cs], (((1,), (1,)), ((), ())),
                                  preferred_element_type=F32)
            att = jnp.where(causal, att, 0.0).astype(BF16)
            st = state_ref[h]
            o = jnp.dot(att, v[:, cs], preferred_element_type=F32)
            o = o + lax.dot_general(q_dec[:, cs], st.astype(BF16), (((1,), (1,)), ((), ())),
                                    preferred_element_type=F32)
            upd = lax.dot_general(v[:, cs], k_end[:, cs], (((0,), (0,)), ((), ())),
                                  preferred_element_type=F32)
            state_ref[h] = decay[:, cs] * st + upd
            outs.append(o)
        o_ref[rws, :] = jnp.concatenate(outs, axis=1).astype(o_ref.dtype)
        return carry

    lax.fori_loop(0, rows // chunk, chunk_body, 0)


def _hgrn(hq, hf, hi, lb_raw, layer):
    b, seq, width = hq.shape
    rows = HGRN_ROWS
    tri = np.tril(np.ones((HGRN_CHUNK, HGRN_CHUNK), np.float32))
    tri = jnp.asarray(tri, BF16)
    row_spec = pl.BlockSpec((None, rows, width), lambda i, j: (i, j, 0))
    full_spec = lambda a: pl.BlockSpec(a.shape, lambda i, j: (0,) * a.ndim)
    n_heads = width // HGRN_HEAD_DIM
    return pl.pallas_call(
        functools.partial(_hgrn_kernel, layer=layer, rows=rows),
        out_shape=jax.ShapeDtypeStruct((b, seq, width), BF16),
        grid=(b, seq // rows),
        in_specs=[row_spec, row_spec, row_spec, full_spec(lb_raw), full_spec(tri)],
        out_specs=row_spec,
        scratch_shapes=[pltpu.VMEM((n_heads, HGRN_HEAD_DIM, HGRN_HEAD_DIM), F32)],
        compiler_params=pltpu.CompilerParams(
            dimension_semantics=("parallel", "arbitrary"), vmem_limit_bytes=VMEM_LIMIT),
        name="hgrn",
    )(hq, hf, hi, lb_raw, tri)


def _out_proj_kernel(x_ref, attn_ref, ag_ref, rec_ref, hg_ref, anw_ref, hnw_ref, fnw_ref,
                     seg_a_ref, seg_h_ref, w_ref, out_ref, *, attn_width):
    def head_norm(t, seg_ref, head_dim):
        sq = (t * t).astype(BF16)
        ms = jnp.dot(sq, seg_ref[...], preferred_element_type=F32) * (1.0 / head_dim)
        return t * lax.rsqrt(ms + NORM_EPS)

    a = attn_ref[...].astype(F32)
    y_a = head_norm(a, seg_a_ref, ATTN_HEAD_DIM) * anw_ref[...] * ag_ref[...].astype(F32)
    r = rec_ref[...].astype(F32)
    y_h = head_norm(r, seg_h_ref, HGRN_HEAD_DIM) * hnw_ref[...] * hg_ref[...].astype(F32)
    mixed = jnp.dot(y_a.astype(BF16), w_ref[0:attn_width, :], preferred_element_type=F32)
    mixed = mixed + jnp.dot(y_h.astype(BF16), w_ref[attn_width:, :], preferred_element_type=F32)
    x = x_ref[...] + mixed
    ms = jnp.mean(x * x, axis=-1, keepdims=True)
    out_ref[...] = (x * lax.rsqrt(ms + NORM_EPS)) * fnw_ref[...]


def _segment_ones(width, head_dim):
    idx = np.arange(width) // head_dim
    return jnp.asarray((idx[:, None] == idx[None, :]).astype(np.float32), BF16)


def _out_proj(x2, attn, ag, rec, hg, anw, hnw, fnw, w_bf16):
    n, d_model = x2.shape
    attn_width = attn.shape[-1]
    hgrn_width = rec.shape[-1]
    rows = PROJ_ROWS
    seg_a = _segment_ones(attn_width, ATTN_HEAD_DIM)
    seg_h = _segment_ones(hgrn_width, HGRN_HEAD_DIM)
    row_spec = lambda c: pl.BlockSpec((rows, c), lambda i: (i, 0))
    full_spec = lambda a: pl.BlockSpec(a.shape, lambda i: (0,) * a.ndim)
    return pl.pallas_call(
        functools.partial(_out_proj_kernel, attn_width=attn_width),
        out_shape=jax.ShapeDtypeStruct((n, d_model), F32),
        grid=(n // rows,),
        in_specs=[row_spec(d_model), row_spec(attn_width), row_spec(attn_width),
                  row_spec(hgrn_width), row_spec(hgrn_width), full_spec(anw), full_spec(hnw),
                  full_spec(fnw), full_spec(seg_a), full_spec(seg_h), full_spec(w_bf16)],
        out_specs=row_spec(d_model),
        compiler_params=pltpu.CompilerParams(
            dimension_semantics=("parallel",), vmem_limit_bytes=VMEM_LIMIT),
        name="out_proj",
    )(x2, attn, ag, rec, hg, anw, hnw, fnw, seg_a, seg_h, w_bf16)


def kernel(x, positions, w_in, w_out, mix_norm_w, attn_out_norm_w, hgrn_out_norm_w,
           hgrn_lb_raw, final_norm_w):
    b, seq, d_model = x.shape
    depth = w_in.shape[0]
    attn_width = attn_out_norm_w.shape[-1]
    hgrn_width = hgrn_out_norm_w.shape[-1]
    assert depth == 1 and attn_width == hgrn_width and w_in.shape[-1] == 8 * attn_width
    assert seq % (max(d for _, d in DILATED_PATTERNS) * ATTN_BLOCK) == 0
    n = b * seq
    layer = 0
    x2 = x.reshape(n, d_model)
    pos2 = positions.reshape(n, 1)
    q, k, v, ag, hq, hf, hi, hg = _in_proj(
        x2, pos2, mix_norm_w[layer][None], w_in[layer].astype(BF16), attn_width)
    to3 = lambda t: t.reshape(b, seq, t.shape[-1])
    attn = _attention(to3(q), to3(k), to3(v))
    rec = _hgrn(to3(hq), to3(hf), to3(hi), hgrn_lb_raw, layer)
    out = _out_proj(x2, attn.reshape(n, attn_width), ag, rec.reshape(n, hgrn_width), hg,
                    attn_out_norm_w[layer][None], hgrn_out_norm_w[layer][None],
                    final_norm_w[None], w_out[layer].astype(BF16))
    return out.reshape(b, seq, d_model)
```

```python
import functools

import numpy as np
import jax
import jax.numpy as jnp
from jax import lax
from jax.experimental import pallas as pl
from jax.experimental.pallas import tpu as pltpu

F32 = jnp.float32
BF16 = jnp.bfloat16

LANES = 128
ATTN_HEAD_DIM = 64
HGRN_HEAD_DIM = 128
DILATED_PATTERNS = ((128, 1), (512, 4), (2048, 16))
ATTN_BLOCK = 128
ROPE_THETA = 500000.0
ROPE_DIMS = ATTN_HEAD_DIM // 4
HGRN_CHUNK = 64
NORM_EPS = 1e-6
VMEM_LIMIT = 56 * 1024 * 1024

PROJ_ROWS = 512
HGRN_ROWS = 256
ATTN_RUN = 8
ATTN_CHAINS = 8


def _silu(t):
    return t * jax.nn.sigmoid(t)


def _in_proj_kernel(x_ref, pos_ref, freq_ref, sgn_up_ref, sgn_dn_ref, nw_ref, w_ref,
                    q_ref, k_ref, v_ref, ag_ref, hq_ref, hf_ref, hi_ref, hg_ref, *, width):
    x = x_ref[...]
    ms = jnp.mean(x * x, axis=-1, keepdims=True)
    hn = ((x * lax.rsqrt(ms + NORM_EPS)) * nw_ref[...]).astype(BF16)

    ang = pos_ref[...].astype(F32) * freq_ref[...]
    cos = jnp.cos(ang)
    sin = jnp.sin(ang)
    sgn_up = sgn_up_ref[...]
    sgn_dn = sgn_dn_ref[...]
    half = ROPE_DIMS // 2

    def rope(t):
        cols = []
        for j in range(width // LANES):
            tj = t[:, j * LANES:(j + 1) * LANES]
            up = pltpu.roll(tj, LANES - half, 1)
            dn = pltpu.roll(tj, half, 1)
            cols.append(tj * cos + sin * (up * sgn_up + dn * sgn_dn))
        return jnp.concatenate(cols, axis=1)

    def proj(g):
        return jnp.dot(hn, w_ref[:, g * width:(g + 1) * width], preferred_element_type=F32)

    scale = ATTN_HEAD_DIM ** -0.5
    q_ref[...] = (rope(proj(0)) * scale).astype(BF16)
    k_ref[...] = rope(proj(1)).astype(BF16)
    v_ref[...] = proj(2).astype(BF16)
    ag_ref[...] = _silu(proj(3)).astype(BF16)
    hq_ref[...] = _silu(proj(4)).astype(BF16)
    hf_ref[...] = proj(5).astype(BF16)
    hi_ref[...] = proj(6).astype(BF16)
    hg_ref[...] = _silu(proj(7)).astype(BF16)


def _rope_lane_tables():
    half = ROPE_DIMS // 2
    inv_freq = ROPE_THETA ** (-np.arange(half, dtype=np.float32) * (2.0 / ROPE_DIMS))
    d = np.arange(LANES) % ATTN_HEAD_DIM
    freq = np.where(d < ROPE_DIMS, inv_freq[d % half], 0.0).astype(np.float32)
    sgn_up = np.where(d < half, -1.0, 0.0).astype(np.float32)
    sgn_dn = np.where((d >= half) & (d < ROPE_DIMS), 1.0, 0.0).astype(np.float32)
    return freq[None], sgn_up[None], sgn_dn[None]


def _in_proj(x2, pos2, norm_w, w_bf16, width):
    n, d_model = x2.shape
    rows = PROJ_ROWS
    freq, sgn_up, sgn_dn = _rope_lane_tables()
    row_spec = lambda c: pl.BlockSpec((rows, c), lambda i: (i, 0))
    full_spec = lambda a: pl.BlockSpec(a.shape, lambda i: (0,) * a.ndim)
    out_sds = jax.ShapeDtypeStruct((n, width), BF16)
    return pl.pallas_call(
        functools.partial(_in_proj_kernel, width=width),
        out_shape=[out_sds] * 8,
        grid=(n // rows,),
        in_specs=[row_spec(d_model), row_spec(1), full_spec(freq), full_spec(sgn_up),
                  full_spec(sgn_dn), full_spec(norm_w), full_spec(w_bf16)],
        out_specs=[row_spec(width)] * 8,
        compiler_params=pltpu.CompilerParams(
            dimension_semantics=("parallel",), vmem_limit_bytes=VMEM_LIMIT),
        name="in_proj",
    )(x2, pos2, freq, sgn_up, sgn_dn, norm_w, w_bf16)


def _attn_kernel(q_ref, k_ref, v_ref, bias_ref, bias0_ref, o_ref,
                 qf, kf, vf, o_pat, m_pat, l_pat, *, seq, pad):
    blk = ATTN_BLOCK
    qf[...] = q_ref[...].astype(F32)
    kf[0:pad, :] = jnp.zeros((pad, LANES), F32)
    vf[0:pad, :] = jnp.zeros((pad, LANES), F32)
    kf[pad:pad + seq, :] = k_ref[...].astype(F32)
    vf[pad:pad + seq, :] = v_ref[...].astype(F32)

    lane = lax.broadcasted_iota(jnp.int32, (1, LANES), 1)
    head_a = lane < ATTN_HEAD_DIM
    head_b = jnp.logical_not(head_a)

    def strided(start, size, dil):
        if dil == 1:
            return pl.ds(pl.multiple_of(start, blk), size)
        return pl.ds(start, size, stride=dil)

    for pat, (window, dil) in enumerate(DILATED_PATTERNS):
        assert window // dil == blk
        n_blk = seq // dil // blk
        n_run = min(ATTN_RUN, n_blk)
        n_res = min(ATTN_CHAINS // n_run, dil)
        runs_per_res = n_blk // n_run

        def body(i, carry, pat=pat, dil=dil, n_run=n_run, n_res=n_res, runs_per_res=runs_per_res):
            for j in range(n_res):
                r = (i // runs_per_res) * n_res + j
                n0 = (i % runs_per_res) * n_run
                q_start = r + dil * blk * n0
                k_start = pad + r + dil * blk * (n0 - 1)
                kb = kf[strided(k_start, (n_run + 1) * blk, dil), :]
                k_a = jnp.where(head_a, kb, 0.0).astype(BF16)
                k_b = jnp.where(head_b, kb, 0.0).astype(BF16)
                vb = vf[strided(k_start, (n_run + 1) * blk, dil), :].astype(BF16)
                qb = qf[strided(q_start, n_run * blk, dil), :].astype(BF16)
                for u in range(n_run):
                    bias = bias_ref[...]
                    if u == 0:
                        bias = jnp.where(n0 == 0, bias0_ref[...], bias)
                    q_u = qb[u * blk:(u + 1) * blk]
                    v_u = vb[u * blk:(u + 2) * blk]

                    def one_head(k_h):
                        s = lax.dot_general(q_u, k_h[u * blk:(u + 2) * blk],
                                            (((1,), (1,)), ((), ())),
                                            preferred_element_type=F32) + bias
                        m = jnp.max(s, axis=-1, keepdims=True)
                        p = jnp.exp(s - m)
                        l = jnp.sum(p, axis=-1, keepdims=True)
                        o = jnp.dot(p.astype(BF16), v_u, preferred_element_type=F32)
                        return o, m, l

                    o_a, m_a, l_a = one_head(k_a)
                    o_b, m_b, l_b = one_head(k_b)
                    rows = strided(q_start + u * dil * blk, blk, dil)
                    o_pat[pat, rows, :] = jnp.where(head_a, o_a, o_b)
                    m_pat[pat, rows, :] = jnp.where(head_a, m_a, m_b)
                    l_pat[pat, rows, :] = jnp.where(head_a, l_a, l_b)
            return carry

        lax.fori_loop(0, dil * n_blk // (n_run * n_res), body, 0)

    def merge(i, carry):
        rows = pl.ds(pl.multiple_of(i * blk, blk), blk)
        ms = [m_pat[p, rows, :] for p in range(len(DILATED_PATTERNS))]
        m_top = functools.reduce(jnp.maximum, ms)
        ws = [jnp.exp(m - m_top) for m in ms]
        num = sum(w * o_pat[p, rows, :] for p, w in enumerate(ws))
        den = sum(w * l_pat[p, rows, :] for p, w in enumerate(ws))
        o_ref[rows, :] = (num / den).astype(BF16)
        return carry

    lax.fori_loop(0, seq // blk, merge, 0, unroll=2)


def _band_bias():
    qi = np.arange(ATTN_BLOCK)[:, None]
    kj = np.arange(2 * ATTN_BLOCK)[None, :]
    dist = ATTN_BLOCK + qi - kj
    band = (dist >= 0) & (dist <= ATTN_BLOCK)
    first = band & (kj >= ATTN_BLOCK)
    to_bias = lambda mk: np.where(mk, 0.0, -np.inf).astype(np.float32)
    return to_bias(band), to_bias(first)


def _attention(q, k, v):
    b, seq, width = q.shape
    pad = max(d for _, d in DILATED_PATTERNS) * ATTN_BLOCK
    bias, bias0 = _band_bias()
    n_pat = len(DILATED_PATTERNS)
    col_spec = pl.BlockSpec((None, seq, LANES), lambda i, j: (i, 0, j))
    bias_spec = pl.BlockSpec(bias.shape, lambda i, j: (0, 0))
    return pl.pallas_call(
        functools.partial(_attn_kernel, seq=seq, pad=pad),
        out_shape=jax.ShapeDtypeStruct((b, seq, width), BF16),
        grid=(b, width // LANES),
        in_specs=[col_spec, col_spec, col_spec, bias_spec, bias_spec],
        out_specs=col_spec,
        scratch_shapes=[pltpu.VMEM((seq, LANES), F32),
                        pltpu.VMEM((pad + seq, LANES), F32),
                        pltpu.VMEM((pad + seq, LANES), F32),
                        pltpu.VMEM((n_pat, seq, LANES), F32),
                        pltpu.VMEM((n_pat, seq, LANES), F32),
                        pltpu.VMEM((n_pat, seq, LANES), F32)],
        compiler_params=pltpu.CompilerParams(
            dimension_semantics=("parallel", "parallel"), vmem_limit_bytes=VMEM_LIMIT),
        name="dilated_attn",
    )(q, k, v, bias, bias0)


def _split3(t):
    hi = t.astype(BF16)
    r1 = t - hi.astype(F32)
    mid = r1.astype(BF16)
    lo = (r1 - mid.astype(F32)).astype(BF16)
    return hi, mid, lo


def _hgrn_kernel(hq_ref, hf_ref, hi_ref, lbraw_ref, tri_ref, o_ref, state_ref, *, layer, rows):
    chunk = HGRN_CHUNK
    n_heads = hq_ref.shape[-1] // HGRN_HEAD_DIM

    @pl.when(pl.program_id(1) == 0)
    def _():
        state_ref[...] = jnp.zeros(state_ref.shape, F32)

    raw = lbraw_ref[...]
    e = jnp.exp(raw - jnp.max(raw, axis=0, keepdims=True))
    sm = e / jnp.sum(e, axis=0, keepdims=True)
    lb = jnp.sum(sm[0:layer + 1, :], axis=0, keepdims=True)

    tri = tri_ref[...]
    ti = lax.broadcasted_iota(jnp.int32, (chunk, chunk), 0)
    si = lax.broadcasted_iota(jnp.int32, (chunk, chunk), 1)
    causal = ti >= si

    def chunk_body(c, carry):
        rws = pl.ds(pl.multiple_of(c * chunk, chunk), chunk)
        f = lb + (1.0 - lb) * jax.nn.sigmoid(hf_ref[rws, :].astype(F32))
        hkey = 1.0 - f
        logf = jnp.log(f)
        cum = sum(jnp.dot(tri, part, preferred_element_type=F32) for part in _split3(logf))
        last = cum[chunk - 1:chunk, :]
        q_dec = (hq_ref[rws, :].astype(F32) * jnp.exp(cum)).astype(BF16)
        k_inv = (hkey * jnp.exp(-cum)).astype(BF16)
        k_end = (hkey * jnp.exp(last - cum)).astype(BF16)
        decay = jnp.exp(last)
        v = hi_ref[rws, :]
        outs = []
        for h in range(n_heads):
            cs = slice(h * HGRN_HEAD_DIM, (h + 1) * HGRN_HEAD_DIM)
            att = lax.dot_general(q_dec[:, cs], k_inv[:, cs], (((1,), (1,)), ((), ())),
                                  preferred_element_type=F32)
            att = jnp.where(causal, att, 0.0).astype(BF16)
            st = state_ref[h]
            o = jnp.dot(att, v[:, cs], preferred_element_type=F32)
            o = o + lax.dot_general(q_dec[:, cs], st.astype(BF16), (((1,), (1,)), ((), ())),
                                    preferred_element_type=F32)
            upd = lax.dot_general(v[:, cs], k_end[:, cs], (((0,), (0,)), ((), ())),
                                  preferred_element_type=F32)
            state_ref[h] = decay[:, cs] * st + upd
            outs.append(o)
        o_ref[rws, :] = jnp.concatenate(outs, axis=1).astype(o_ref.dtype)
        return carry

    lax.fori_loop(0, rows // chunk, chunk_body, 0)


def _hgrn(hq, hf, hi, lb_raw, layer):
    b, seq, width = hq.shape
    rows = HGRN_ROWS
    tri = np.tril(np.ones((HGRN_CHUNK, HGRN_CHUNK), np.float32))
    tri = jnp.asarray(tri, BF16)
    row_spec = pl.BlockSpec((None, rows, width), lambda i, j: (i, j, 0))
    full_spec = lambda a: pl.BlockSpec(a.shape, lambda i, j: (0,) * a.ndim)
    n_heads = width // HGRN_HEAD_DIM
    return pl.pallas_call(
        functools.partial(_hgrn_kernel, layer=layer, rows=rows),
        out_shape=jax.ShapeDtypeStruct((b, seq, width), BF16),
        grid=(b, seq // rows),
        in_specs=[row_spec, row_spec, row_spec, full_spec(lb_raw), full_spec(tri)],
        out_specs=row_spec,
        scratch_shapes=[pltpu.VMEM((n_heads, HGRN_HEAD_DIM, HGRN_HEAD_DIM), F32)],
        compiler_params=pltpu.CompilerParams(
            dimension_semantics=("parallel", "arbitrary"), vmem_limit_bytes=VMEM_LIMIT),
        name="hgrn",
    )(hq, hf, hi, lb_raw, tri)


def _out_proj_kernel(x_ref, attn_ref, ag_ref, rec_ref, hg_ref, anw_ref, hnw_ref, fnw_ref,
                     seg_a_ref, seg_h_ref, w_ref, out_ref, *, attn_width):
    def head_norm(t, seg_ref, head_dim):
        sq = (t * t).astype(BF16)
        ms = jnp.dot(sq, seg_ref[...], preferred_element_type=F32) * (1.0 / head_dim)
        return t * lax.rsqrt(ms + NORM_EPS)

    a = attn_ref[...].astype(F32)
    y_a = head_norm(a, seg_a_ref, ATTN_HEAD_DIM) * anw_ref[...] * ag_ref[...].astype(F32)
    r = rec_ref[...].astype(F32)
    y_h = head_norm(r, seg_h_ref, HGRN_HEAD_DIM) * hnw_ref[...] * hg_ref[...].astype(F32)
    mixed = jnp.dot(y_a.astype(BF16), w_ref[0:attn_width, :], preferred_element_type=F32)
    mixed = mixed + jnp.dot(y_h.astype(BF16), w_ref[attn_width:, :], preferred_element_type=F32)
    x = x_ref[...] + mixed
    ms = jnp.mean(x * x, axis=-1, keepdims=True)
    out_ref[...] = (x * lax.rsqrt(ms + NORM_EPS)) * fnw_ref[...]


def _segment_ones(width, head_dim):
    idx = np.arange(width) // head_dim
    return jnp.asarray((idx[:, None] == idx[None, :]).astype(np.float32), BF16)


def _out_proj(x2, attn, ag, rec, hg, anw, hnw, fnw, w_bf16):
    n, d_model = x2.shape
    attn_width = attn.shape[-1]
    hgrn_width = rec.shape[-1]
    rows = PROJ_ROWS
    seg_a = _segment_ones(attn_width, ATTN_HEAD_DIM)
    seg_h = _segment_ones(hgrn_width, HGRN_HEAD_DIM)
    row_spec = lambda c: pl.BlockSpec((rows, c), lambda i: (i, 0))
    full_spec = lambda a: pl.BlockSpec(a.shape, lambda i: (0,) * a.ndim)
    return pl.pallas_call(
        functools.partial(_out_proj_kernel, attn_width=attn_width),
        out_shape=jax.ShapeDtypeStruct((n, d_model), F32),
        grid=(n // rows,),
        in_specs=[row_spec(d_model), row_spec(attn_width), row_spec(attn_width),
                  row_spec(hgrn_width), row_spec(hgrn_width), full_spec(anw), full_spec(hnw),
                  full_spec(fnw), full_spec(seg_a), full_spec(seg_h), full_spec(w_bf16)],
        out_specs=row_spec(d_model),
        compiler_params=pltpu.CompilerParams(
            dimension_semantics=("parallel",), vmem_limit_bytes=VMEM_LIMIT),
        name="out_proj",
    )(x2, attn, ag, rec, hg, anw, hnw, fnw, seg_a, seg_h, w_bf16)


def kernel(x, positions, w_in, w_out, mix_norm_w, attn_out_norm_w, hgrn_out_norm_w,
           hgrn_lb_raw, final_norm_w):
    b, seq, d_model = x.shape
    depth = w_in.shape[0]
    attn_width = attn_out_norm_w.shape[-1]
    hgrn_width = hgrn_out_norm_w.shape[-1]
    assert depth == 1 and attn_width == hgrn_width and w_in.shape[-1] == 8 * attn_width
    assert seq % (max(d for _, d in DILATED_PATTERNS) * ATTN_BLOCK) == 0
    n = b * seq
    layer = 0
    x2 = x.reshape(n, d_model)
    pos2 = positions.reshape(n, 1)
    q, k, v, ag, hq, hf, hi, hg = _in_proj(
        x2, pos2, mix_norm_w[layer][None], w_in[layer].astype(BF16), attn_width)
    to3 = lambda t: t.reshape(b, seq, t.shape[-1])
    attn = _attention(to3(q), to3(k), to3(v))
    rec = _hgrn(to3(hq), to3(hf), to3(hi), hgrn_lb_raw, layer)
    out = _out_proj(x2, attn.reshape(n, attn_width), ag, rec.reshape(n, hgrn_width), hg,
                    attn_out_norm_w[layer][None], hgrn_out_norm_w[layer][None],
                    final_norm_w[None], w_out[layer].astype(BF16))
    return out.reshape(b, seq, d_model)
```

```python
import functools

import numpy as np
import jax
import jax.numpy as jnp
from jax import lax
from jax.experimental import pallas as pl
from jax.experimental.pallas import tpu as pltpu

F32 = jnp.float32
BF16 = jnp.bfloat16

LANES = 128
ATTN_HEAD_DIM = 64
HGRN_HEAD_DIM = 128
DILATED_PATTERNS = ((128, 1), (512, 4), (2048, 16))
ATTN_BLOCK = 128
ATTN_DEINTERLEAVE = 4
ROPE_THETA = 500000.0
ROPE_DIMS = ATTN_HEAD_DIM // 4
HGRN_CHUNK = 64
NORM_EPS = 1e-6
VMEM_LIMIT = 56 * 1024 * 1024

PROJ_ROWS = 512
HGRN_ROWS = 512
ATTN_RUN = 16
ATTN_CHAINS = 16


def _silu(t):
    return t * jax.nn.sigmoid(t)


def _in_proj_kernel(x_ref, pos_ref, freq_ref, sgn_up_ref, sgn_dn_ref, nw_ref, w_ref,
                    q_ref, k_ref, v_ref, ag_ref, hq_ref, hf_ref, hi_ref, hg_ref, *, width):
    x = x_ref[...]
    ms = jnp.mean(x * x, axis=-1, keepdims=True)
    hn = ((x * lax.rsqrt(ms + NORM_EPS)) * nw_ref[...]).astype(BF16)

    ang = pos_ref[...].astype(F32) * freq_ref[...]
    cos = jnp.cos(ang)
    sin = jnp.sin(ang)
    sgn_up = sgn_up_ref[...]
    sgn_dn = sgn_dn_ref[...]
    half = ROPE_DIMS // 2

    def rope(t):
        cols = []
        for j in range(width // LANES):
            tj = t[:, j * LANES:(j + 1) * LANES]
            up = pltpu.roll(tj, LANES - half, 1)
            dn = pltpu.roll(tj, half, 1)
            cols.append(tj * cos + sin * (up * sgn_up + dn * sgn_dn))
        return jnp.concatenate(cols, axis=1)

    def proj(g):
        return jnp.dot(hn, w_ref[:, g * width:(g + 1) * width], preferred_element_type=F32)

    scale = ATTN_HEAD_DIM ** -0.5
    q_ref[...] = (rope(proj(0)) * scale).astype(BF16)
    k_ref[...] = rope(proj(1)).astype(BF16)
    v_ref[...] = proj(2).astype(BF16)
    ag_ref[...] = _silu(proj(3)).astype(BF16)
    hq_ref[...] = _silu(proj(4)).astype(BF16)
    hf_ref[...] = proj(5).astype(BF16)
    hi_ref[...] = proj(6).astype(BF16)
    hg_ref[...] = _silu(proj(7)).astype(BF16)


def _rope_lane_tables():
    half = ROPE_DIMS // 2
    inv_freq = ROPE_THETA ** (-np.arange(half, dtype=np.float32) * (2.0 / ROPE_DIMS))
    d = np.arange(LANES) % ATTN_HEAD_DIM
    freq = np.where(d < ROPE_DIMS, inv_freq[d % half], 0.0).astype(np.float32)
    sgn_up = np.where(d < half, -1.0, 0.0).astype(np.float32)
    sgn_dn = np.where((d >= half) & (d < ROPE_DIMS), 1.0, 0.0).astype(np.float32)
    return freq[None], sgn_up[None], sgn_dn[None]


def _in_proj(x2, pos2, norm_w, w_bf16, width):
    n, d_model = x2.shape
    rows = PROJ_ROWS
    freq, sgn_up, sgn_dn = _rope_lane_tables()
    row_spec = lambda c: pl.BlockSpec((rows, c), lambda i: (i, 0))
    full_spec = lambda a: pl.BlockSpec(a.shape, lambda i: (0,) * a.ndim)
    out_sds = jax.ShapeDtypeStruct((n, width), BF16)
    return pl.pallas_call(
        functools.partial(_in_proj_kernel, width=width),
        out_shape=[out_sds] * 8,
        grid=(n // rows,),
        in_specs=[row_spec(d_model), row_spec(1), full_spec(freq), full_spec(sgn_up),
                  full_spec(sgn_dn), full_spec(norm_w), full_spec(w_bf16)],
        out_specs=[row_spec(width)] * 8,
        compiler_params=pltpu.CompilerParams(
            dimension_semantics=("parallel",), vmem_limit_bytes=VMEM_LIMIT),
        name="in_proj",
    )(x2, pos2, freq, sgn_up, sgn_dn, norm_w, w_bf16)


def _attn_kernel(q_ref, k_ref, v_ref, bias_ref, bias0_ref, o_ref,
                 qf, kf, vf, q4, k4, v4, o_pat, m_pat, l_pat, *, seq):
    blk = ATTN_BLOCK
    dint = ATTN_DEINTERLEAVE
    len4 = seq // dint
    pad4 = blk * max(d for _, d in DILATED_PATTERNS) // dint
    seg4 = pad4 + len4

    qf[...] = q_ref[...].astype(F32)
    for src, dst, dst4 in ((k_ref, kf, k4), (v_ref, vf, v4)):
        dst[0:blk, :] = jnp.zeros((blk, LANES), F32)
        dst[blk:blk + seq, :] = src[...].astype(F32)
        for r in range(dint):
            dst4[r * seg4:r * seg4 + pad4, :] = jnp.zeros((pad4, LANES), F32)
            dst4[r * seg4 + pad4:(r + 1) * seg4, :] = dst[pl.ds(blk + r, len4, stride=dint), :]
    for r in range(dint):
        q4[r * len4:(r + 1) * len4, :] = qf[pl.ds(r, len4, stride=dint), :]

    lane = lax.broadcasted_iota(jnp.int32, (1, LANES), 1)
    head_a = lane < ATTN_HEAD_DIM
    head_b = jnp.logical_not(head_a)

    def rows_of(start, size, stride):
        if stride == 1:
            return pl.ds(pl.multiple_of(start, blk), size)
        return pl.ds(start, size, stride=stride)

    for pat, (window, dil) in enumerate(DILATED_PATTERNS):
        assert window // dil == blk and (dil == 1 or dil % dint == 0)
        n_blk = seq // dil // blk
        n_run = min(ATTN_RUN, n_blk)
        n_res = min(ATTN_CHAINS // n_run, dil)
        runs_per_res = n_blk // n_run
        stride = 1 if dil == 1 else dil // dint
        q_src, k_src, v_src = (qf, kf, vf) if dil == 1 else (q4, k4, v4)

        def body(i, carry, pat=pat, dil=dil, n_run=n_run, n_res=n_res,
                 runs_per_res=runs_per_res, stride=stride, q_src=q_src, k_src=k_src, v_src=v_src):
            for j in range(n_res):
                r = (i // runs_per_res) * n_res + j
                n0 = (i % runs_per_res) * n_run
                if dil == 1:
                    q_start = blk * n0
                    k_start = blk + blk * (n0 - 1)
                else:
                    seg, off = r % dint, r // dint
                    q_start = seg * len4 + off + stride * blk * n0
                    k_start = seg * seg4 + pad4 + off + stride * blk * (n0 - 1)
                k_rows = rows_of(k_start, (n_run + 1) * blk, stride)
                kb = k_src[k_rows, :]
                k_a = jnp.where(head_a, kb, 0.0).astype(BF16)
                k_b = jnp.where(head_b, kb, 0.0).astype(BF16)
                vb = v_src[k_rows, :].astype(BF16)
                qb = q_src[rows_of(q_start, n_run * blk, stride), :].astype(BF16)
                for u in range(n_run):
                    bias = bias_ref[...]
                    if u == 0:
                        bias = jnp.where(n0 == 0, bias0_ref[...], bias)
                    q_u = qb[u * blk:(u + 1) * blk]
                    v_u = vb[u * blk:(u + 2) * blk]

                    def one_head(k_h):
                        s = lax.dot_general(q_u, k_h[u * blk:(u + 2) * blk],
                                            (((1,), (1,)), ((), ())),
                                            preferred_element_type=F32) + bias
                        m = jnp.max(s, axis=-1, keepdims=True)
                        p = jnp.exp(s - m)
                        l = jnp.sum(p, axis=-1, keepdims=True)
                        o = jnp.dot(p.astype(BF16), v_u, preferred_element_type=F32)
                        return o, m, l

                    o_a, m_a, l_a = one_head(k_a)
                    o_b, m_b, l_b = one_head(k_b)
                    rows = rows_of(q_start + u * stride * blk, blk, stride)
                    o_pat[pat, rows, :] = jnp.where(head_a, o_a, o_b)
                    m_pat[pat, rows, :] = jnp.where(head_a, m_a, m_b)
                    l_pat[pat, rows, :] = jnp.where(head_a, l_a, l_b)
            return carry

        lax.fori_loop(0, dil * n_blk // (n_run * n_res), body, 0)

    blocks_per_seg = len4 // blk

    def merge(i, carry):
        seg, c = i // blocks_per_seg, i % blocks_per_seg
        rows4 = pl.ds(pl.multiple_of(seg * len4 + c * blk, blk), blk)
        rows1 = pl.ds(seg + dint * blk * c, blk, stride=dint)
        rows = [rows1 if dil == 1 else rows4 for _, dil in DILATED_PATTERNS]
        ms = [m_pat[p, rw, :] for p, rw in enumerate(rows)]
        m_top = functools.reduce(jnp.maximum, ms)
        ws = [jnp.exp(m - m_top) for m in ms]
        num = sum(w * o_pat[p, rw, :] for p, (w, rw) in enumerate(zip(ws, rows)))
        den = sum(w * l_pat[p, rw, :] for p, (w, rw) in enumerate(zip(ws, rows)))
        o_pat[0, rows1, :] = num / den
        return carry

    assert DILATED_PATTERNS[0][1] == 1
    lax.fori_loop(0, seq // blk, merge, 0)
    o_ref[...] = o_pat[0].astype(BF16)


def _band_bias():
    qi = np.arange(ATTN_BLOCK)[:, None]
    kj = np.arange(2 * ATTN_BLOCK)[None, :]
    dist = ATTN_BLOCK + qi - kj
    band = (dist >= 0) & (dist <= ATTN_BLOCK)
    first = band & (kj >= ATTN_BLOCK)
    to_bias = lambda mk: np.where(mk, 0.0, -np.inf).astype(np.float32)
    return to_bias(band), to_bias(first)


def _attention(q, k, v):
    b, seq, width = q.shape
    bias, bias0 = _band_bias()
    n_pat = len(DILATED_PATTERNS)
    pad4 = ATTN_BLOCK * max(d for _, d in DILATED_PATTERNS) // ATTN_DEINTERLEAVE
    rows4 = seq + ATTN_DEINTERLEAVE * pad4
    col_spec = pl.BlockSpec((None, seq, LANES), lambda i, j: (i, 0, j))
    bias_spec = pl.BlockSpec(bias.shape, lambda i, j: (0, 0))
    return pl.pallas_call(
        functools.partial(_attn_kernel, seq=seq),
        out_shape=jax.ShapeDtypeStruct((b, seq, width), BF16),
        grid=(b, width // LANES),
        in_specs=[col_spec, col_spec, col_spec, bias_spec, bias_spec],
        out_specs=col_spec,
        scratch_shapes=[pltpu.VMEM((seq, LANES), F32),
                        pltpu.VMEM((ATTN_BLOCK + seq, LANES), F32),
                        pltpu.VMEM((ATTN_BLOCK + seq, LANES), F32),
                        pltpu.VMEM((seq, LANES), F32),
                        pltpu.VMEM((rows4, LANES), F32),
                        pltpu.VMEM((rows4, LANES), F32),
                        pltpu.VMEM((n_pat, seq, LANES), F32),
                        pltpu.VMEM((n_pat, seq, LANES), F32),
                        pltpu.VMEM((n_pat, seq, LANES), F32)],
        compiler_params=pltpu.CompilerParams(
            dimension_semantics=("parallel", "parallel"), vmem_limit_bytes=VMEM_LIMIT),
        name="dilated_attn",
    )(q, k, v, bias, bias0)


def _split3(t):
    hi = t.astype(BF16)
    r1 = t - hi.astype(F32)
    mid = r1.astype(BF16)
    lo = (r1 - mid.astype(F32)).astype(BF16)
    return hi, mid, lo


def _hgrn_kernel(hq_ref, hf_ref, hi_ref, lbraw_ref, tri_ref, o_ref, state_ref, *, layer, rows):
    chunk = HGRN_CHUNK
    n_heads = hq_ref.shape[-1] // HGRN_HEAD_DIM

    @pl.when(pl.program_id(1) == 0)
    def _():
        state_ref[...] = jnp.zeros(state_ref.shape, F32)

    raw = lbraw_ref[...]
    e = jnp.exp(raw - jnp.max(raw, axis=0, keepdims=True))
    sm = e / jnp.sum(e, axis=0, keepdims=True)
    lb = jnp.sum(sm[0:layer + 1, :], axis=0, keepdims=True)

    tri = tri_ref[...]
    ti = lax.broadcasted_iota(jnp.int32, (chunk, chunk), 0)
    si = lax.broadcasted_iota(jnp.int32, (chunk, chunk), 1)
    causal = ti >= si

    heads = [slice(h * HGRN_HEAD_DIM, (h + 1) * HGRN_HEAD_DIM) for h in range(n_heads)]
    q_dec, v, decay, o_intra, upd = [], [], [], [], []
    for c in range(rows // chunk):
        rws = slice(c * chunk, (c + 1) * chunk)
        f = lb + (1.0 - lb) * jax.nn.sigmoid(hf_ref[rws, :].astype(F32))
        hkey = 1.0 - f
        logf = jnp.log(f)
        cum = sum(jnp.dot(tri, part, preferred_element_type=F32) for part in _split3(logf))
        dec_c = jnp.exp(cum[chunk - 1:chunk, :])
        q_c = (hq_ref[rws, :].astype(F32) * jnp.exp(cum)).astype(BF16)
        k_inv = hkey * jnp.exp(-cum)
        k_end = (k_inv * dec_c).astype(BF16)
        k_inv = k_inv.astype(BF16)
        v_c = hi_ref[rws, :]
        o_c, u_c = [], []
        for cs in heads:
            att = lax.dot_general(q_c[:, cs], k_inv[:, cs], (((1,), (1,)), ((), ())),
                                  preferred_element_type=F32)
            att = jnp.where(causal, att, 0.0).astype(BF16)
            o_c.append(jnp.dot(att, v_c[:, cs], preferred_element_type=F32))
            u_c.append(lax.dot_general(v_c[:, cs], k_end[:, cs], (((0,), (0,)), ((), ())),
                                       preferred_element_type=F32))
        q_dec.append(q_c)
        v.append(v_c)
        decay.append(dec_c)
        o_intra.append(o_c)
        upd.append(u_c)

    for h, cs in enumerate(heads):
        st = state_ref[h]
        for c in range(rows // chunk):
            o = o_intra[c][h] + lax.dot_general(
                q_dec[c][:, cs], st.astype(BF16), (((1,), (1,)), ((), ())),
                preferred_element_type=F32)
            o_ref[c * chunk:(c + 1) * chunk, cs] = o.astype(o_ref.dtype)
            st = decay[c][:, cs] * st + upd[c][h]
        state_ref[h] = st


def _hgrn(hq, hf, hi, lb_raw, layer):
    b, seq, width = hq.shape
    rows = HGRN_ROWS
    tri = np.tril(np.ones((HGRN_CHUNK, HGRN_CHUNK), np.float32))
    tri = jnp.asarray(tri, BF16)
    row_spec = pl.BlockSpec((None, rows, width), lambda i, j: (i, j, 0))
    full_spec = lambda a: pl.BlockSpec(a.shape, lambda i, j: (0,) * a.ndim)
    n_heads = width // HGRN_HEAD_DIM
    return pl.pallas_call(
        functools.partial(_hgrn_kernel, layer=layer, rows=rows),
        out_shape=jax.ShapeDtypeStruct((b, seq, width), BF16),
        grid=(b, seq // rows),
        in_specs=[row_spec, row_spec, row_spec, full_spec(lb_raw), full_spec(tri)],
        out_specs=row_spec,
        scratch_shapes=[pltpu.VMEM((n_heads, HGRN_HEAD_DIM, HGRN_HEAD_DIM), F32)],
        compiler_params=pltpu.CompilerParams(
            dimension_semantics=("parallel", "arbitrary"), vmem_limit_bytes=VMEM_LIMIT),
        name="hgrn",
    )(hq, hf, hi, lb_raw, tri)


def _out_proj_kernel(x_ref, attn_ref, ag_ref, rec_ref, hg_ref, anw_ref, hnw_ref, fnw_ref,
                     seg_a_ref, seg_h_ref, w_ref, out_ref, *, attn_width):
    def head_norm(t, seg_ref, head_dim):
        sq = (t * t).astype(BF16)
        ms = jnp.dot(sq, seg_ref[...], preferred_element_type=F32) * (1.0 / head_dim)
        return t * lax.rsqrt(ms + NORM_EPS)

    a = attn_ref[...].astype(F32)
    y_a = head_norm(a, seg_a_ref, ATTN_HEAD_DIM) * anw_ref[...] * ag_ref[...].astype(F32)
    r = rec_ref[...].astype(F32)
    y_h = head_norm(r, seg_h_ref, HGRN_HEAD_DIM) * hnw_ref[...] * hg_ref[...].astype(F32)
    mixed = jnp.dot(y_a.astype(BF16), w_ref[0:attn_width, :], preferred_element_type=F32)
    mixed = mixed + jnp.dot(y_h.astype(BF16), w_ref[attn_width:, :], preferred_element_type=F32)
    x = x_ref[...] + mixed
    ms = jnp.mean(x * x, axis=-1, keepdims=True)
    out_ref[...] = (x * lax.rsqrt(ms + NORM_EPS)) * fnw_ref[...]


def _segment_ones(width, head_dim):
    idx = np.arange(width) // head_dim
    return jnp.asarray((idx[:, None] == idx[None, :]).astype(np.float32), BF16)


def _out_proj(x2, attn, ag, rec, hg, anw, hnw, fnw, w_bf16):
    n, d_model = x2.shape
    attn_width = attn.shape[-1]
    hgrn_width = rec.shape[-1]
    rows = PROJ_ROWS
    seg_a = _segment_ones(attn_width, ATTN_HEAD_DIM)
    seg_h = _segment_ones(hgrn_width, HGRN_HEAD_DIM)
    row_spec = lambda c: pl.BlockSpec((rows, c), lambda i: (i, 0))
    full_spec = lambda a: pl.BlockSpec(a.shape, lambda i: (0,) * a.ndim)
    return pl.pallas_call(
        functools.partial(_out_proj_kernel, attn_width=attn_width),
        out_shape=jax.ShapeDtypeStruct((n, d_model), F32),
        grid=(n // rows,),
        in_specs=[row_spec(d_model), row_spec(attn_width), row_spec(attn_width),
                  row_spec(hgrn_width), row_spec(hgrn_width), full_spec(anw), full_spec(hnw),
                  full_spec(fnw), full_spec(seg_a), full_spec(seg_h), full_spec(w_bf16)],
        out_specs=row_spec(d_model),
        compiler_params=pltpu.CompilerParams(
            dimension_semantics=("parallel",), vmem_limit_bytes=VMEM_LIMIT),
        name="out_proj",
    )(x2, attn, ag, rec, hg, anw, hnw, fnw, seg_a, seg_h, w_bf16)


def kernel(x, positions, w_in, w_out, mix_norm_w, attn_out_norm_w, hgrn_out_norm_w,
           hgrn_lb_raw, final_norm_w):
    b, seq, d_model = x.shape
    depth = w_in.shape[0]
    attn_width = attn_out_norm_w.shape[-1]
    hgrn_width = hgrn_out_norm_w.shape[-1]
    assert depth == 1 and attn_width == hgrn_width and w_in.shape[-1] == 8 * attn_width
    assert seq % (max(d for _, d in DILATED_PATTERNS) * ATTN_BLOCK) == 0
    n = b * seq
    layer = 0
    x2 = x.reshape(n, d_model)
    pos2 = positions.reshape(n, 1)
    q, k, v, ag, hq, hf, hi, hg = _in_proj(
        x2, pos2, mix_norm_w[layer][None], w_in[layer].astype(BF16), attn_width)
    to3 = lambda t: t.reshape(b, seq, t.shape[-1])
    attn = _attention(to3(q), to3(k), to3(v))
    rec = _hgrn(to3(hq), to3(hf), to3(hi), hgrn_lb_raw, layer)
    out = _out_proj(x2, attn.reshape(n, attn_width), ag, rec.reshape(n, hgrn_width), hg,
                    attn_out_norm_w[layer][None], hgrn_out_norm_w[layer][None],
                    final_norm_w[None], w_out[layer].astype(BF16))
    return out.reshape(b, seq, d_model)
```

```python
import functools

import numpy as np
import jax
import jax.numpy as jnp
from jax import lax
from jax.experimental import pallas as pl
from jax.experimental.pallas import tpu as pltpu

F32 = jnp.float32
BF16 = jnp.bfloat16

LANES = 128
ATTN_HEAD_DIM = 64
HGRN_HEAD_DIM = 128
DILATED_PATTERNS = ((128, 1), (512, 4), (2048, 16))
ATTN_BLOCK = 128
ATTN_DEINTERLEAVE = 4
ROPE_THETA = 500000.0
ROPE_DIMS = ATTN_HEAD_DIM // 4
HGRN_CHUNK = 64
NORM_EPS = 1e-6
VMEM_LIMIT = 56 * 1024 * 1024

PROJ_ROWS = 512
ATTN_RUN = 16
ATTN_CHAINS = 16


def _silu(t):
    return t * jax.nn.sigmoid(t)


def _split3(t):
    hi = t.astype(BF16)
    r1 = t - hi.astype(F32)
    mid = r1.astype(BF16)
    lo = (r1 - mid.astype(F32)).astype(BF16)
    return hi, mid, lo


def _hgrn_tile(hq, hf, hi, lbraw_ref, tri_ref, state_ref, o_ref, layer, interleave):
    chunk = HGRN_CHUNK
    rows, width = hq.shape
    n_chunks = rows // chunk
    n_heads = width // HGRN_HEAD_DIM
    heads = [slice(h * HGRN_HEAD_DIM, (h + 1) * HGRN_HEAD_DIM) for h in range(n_heads)]
    chunks = [slice(c * chunk, (c + 1) * chunk) for c in range(n_chunks)]

    raw = lbraw_ref[...]
    e = jnp.exp(raw - jnp.max(raw, axis=0, keepdims=True))
    sm = e / jnp.sum(e, axis=0, keepdims=True)
    lb = jnp.sum(sm[0:layer + 1, :], axis=0, keepdims=True)

    tri = tri_ref[...]
    ti = lax.broadcasted_iota(jnp.int32, (chunk, chunk), 0)
    si = lax.broadcasted_iota(jnp.int32, (chunk, chunk), 1)
    causal = ti >= si

    hkey, cum = [], []
    for rws in chunks:
        f = lb + (1.0 - lb) * jax.nn.sigmoid(hf[rws, :])
        hkey.append(1.0 - f)
        cum.append(sum(jnp.dot(tri, part, preferred_element_type=F32)
                       for part in _split3(jnp.log(f))))
    interleave[0]()

    q_dec, decay, att, upd = [], [], [], []
    for c, rws in enumerate(chunks):
        dec_c = jnp.exp(cum[c][chunk - 1:chunk, :])
        q_c = (hq[rws, :] * jnp.exp(cum[c])).astype(BF16)
        k_inv = hkey[c] * jnp.exp(-cum[c])
        k_end = (k_inv * dec_c).astype(BF16)
        k_inv = k_inv.astype(BF16)
        att.append([lax.dot_general(q_c[:, cs], k_inv[:, cs], (((1,), (1,)), ((), ())),
                                    preferred_element_type=F32) for cs in heads])
        upd.append([lax.dot_general(hi[rws, cs], k_end[:, cs], (((0,), (0,)), ((), ())),
                                    preferred_element_type=F32) for cs in heads])
        q_dec.append(q_c)
        decay.append(dec_c)
    interleave[1]()

    o_intra = [[jnp.dot(jnp.where(causal, att[c][h], 0.0).astype(BF16), hi[rws, cs],
                        preferred_element_type=F32) for h, cs in enumerate(heads)]
               for c, rws in enumerate(chunks)]
    interleave[2]()

    for h, cs in enumerate(heads):
        if h == n_heads // 2:
            interleave[3]()
        st = state_ref[h]
        for c, rws in enumerate(chunks):
            o = o_intra[c][h] + lax.dot_general(
                q_dec[c][:, cs], st.astype(BF16), (((1,), (1,)), ((), ())),
                preferred_element_type=F32)
            o_ref[rws, cs] = o.astype(o_ref.dtype)
            st = decay[c][:, cs] * st + upd[c][h]
        state_ref[h] = st


def _in_proj_kernel(x_ref, pos_ref, freq_ref, spread_ref, cos_base_ref, sgn_up_ref, sgn_dn_ref,
                    nw_ref, lbraw_ref, tri_ref, w_ref,
                    q_ref, k_ref, v_ref, ag_ref, rec_ref, hg_ref, state_ref,
                    *, width, layer, steps_per_seq):
    x = x_ref[...]
    ms = jnp.mean(x * x, axis=-1, keepdims=True)
    hn = ((x * lax.rsqrt(ms + NORM_EPS)) * nw_ref[...]).astype(BF16)

    ang = freq_ref[...] * pos_ref[...].astype(F32)
    parts = []
    for t in (jnp.cos(ang), jnp.sin(ang)):
        hi = t.astype(BF16).astype(F32)
        parts += [hi, t - hi]
    tabs = lax.dot_general(jnp.concatenate(parts, axis=0).astype(BF16), spread_ref[...],
                           (((0,), (0,)), ((), ())), preferred_element_type=F32)
    cos = tabs[:, :LANES] + cos_base_ref[...]
    sin = tabs[:, LANES:]
    sgn_up = sgn_up_ref[...]
    sgn_dn = sgn_dn_ref[...]
    half = ROPE_DIMS // 2

    def rope(t):
        cols = []
        for j in range(width // LANES):
            tj = t[:, j * LANES:(j + 1) * LANES]
            up = pltpu.roll(tj, LANES - half, 1)
            dn = pltpu.roll(tj, half, 1)
            cols.append(tj * cos + sin * (up * sgn_up + dn * sgn_dn))
        return jnp.concatenate(cols, axis=1)

    def proj(g):
        return jnp.dot(hn, w_ref[:, g * width:(g + 1) * width], preferred_element_type=F32)

    @pl.when(pl.program_id(0) % steps_per_seq == 0)
    def _():
        state_ref[...] = jnp.zeros(state_ref.shape, F32)

    scale = ATTN_HEAD_DIM ** -0.5

    def emit_q():
        q_ref[...] = (rope(proj(0)) * scale).astype(BF16)

    def emit_k():
        k_ref[...] = rope(proj(1)).astype(BF16)

    def emit_v():
        v_ref[...] = proj(2).astype(BF16)

    def emit_ag():
        ag_ref[...] = _silu(proj(3)).astype(BF16)

    hf = proj(5)
    hq = _silu(proj(4))
    hi = proj(6).astype(BF16)
    _hgrn_tile(hq, hf, hi, lbraw_ref, tri_ref, state_ref, rec_ref, layer,
               (emit_q, emit_k, emit_v, emit_ag))
    hg_ref[...] = _silu(proj(7)).astype(BF16)


def _rope_tables():
    half = ROPE_DIMS // 2
    inv_freq = ROPE_THETA ** (-np.arange(half, dtype=np.float32) * (2.0 / ROPE_DIMS))
    d = np.arange(LANES) % ATTN_HEAD_DIM
    rotary = d < ROPE_DIMS
    lane_uses = (np.arange(half)[:, None] == (d % half)[None, :]) & rotary[None, :]
    spread = np.zeros((4 * half, 2 * LANES), np.float32)
    for t in range(4):
        table = t // 2
        spread[t * half:(t + 1) * half, table * LANES:(table + 1) * LANES] = lane_uses
    cos_base = np.where(rotary, 0.0, 1.0).astype(np.float32)
    sgn_up = np.where(d < half, -1.0, 0.0).astype(np.float32)
    sgn_dn = np.where((d >= half) & rotary, 1.0, 0.0).astype(np.float32)
    return (inv_freq.astype(np.float32)[:, None], jnp.asarray(spread, BF16), cos_base[None],
            sgn_up[None], sgn_dn[None])


def _in_proj(x2, positions, norm_w, w_bf16, lb_raw, width, layer, seq):
    n, d_model = x2.shape
    rows = PROJ_ROWS
    pos3 = positions.reshape(n // rows, 1, rows)
    consts = _rope_tables()
    tri = jnp.asarray(np.tril(np.ones((HGRN_CHUNK, HGRN_CHUNK), np.float32)), BF16)
    row_spec = lambda c: pl.BlockSpec((rows, c), lambda i: (i, 0))
    full_spec = lambda a: pl.BlockSpec(a.shape, lambda i: (0,) * a.ndim)
    pos_spec = pl.BlockSpec((None, 1, rows), lambda i: (i, 0, 0))
    out_sds = jax.ShapeDtypeStruct((n, width), BF16)
    n_out = 6
    small = [*consts, norm_w, lb_raw, tri, w_bf16]
    return pl.pallas_call(
        functools.partial(_in_proj_kernel, width=width, layer=layer, steps_per_seq=seq // rows),
        out_shape=[out_sds] * n_out,
        grid=(n // rows,),
        in_specs=[row_spec(d_model), pos_spec] + [full_spec(a) for a in small],
        out_specs=[row_spec(width)] * n_out,
        scratch_shapes=[pltpu.VMEM((width // HGRN_HEAD_DIM, HGRN_HEAD_DIM, HGRN_HEAD_DIM), F32)],
        compiler_params=pltpu.CompilerParams(
            dimension_semantics=("arbitrary",), vmem_limit_bytes=VMEM_LIMIT),
        name="in_proj",
    )(x2, pos3, *small)


def _attn_kernel(q_ref, k_ref, v_ref, bias_ref, bias0_ref, o_ref,
                 qf, kf, vf, q4, k4, v4, o_pat, m_pat, l_pat, *, seq):
    blk = ATTN_BLOCK
    dint = ATTN_DEINTERLEAVE
    len4 = seq // dint
    pad4 = blk * max(d for _, d in DILATED_PATTERNS) // dint
    seg4 = pad4 + len4

    qf[...] = q_ref[...].astype(F32)
    for src, dst, dst4 in ((k_ref, kf, k4), (v_ref, vf, v4)):
        dst[0:blk, :] = jnp.zeros((blk, LANES), F32)
        dst[blk:blk + seq, :] = src[...].astype(F32)
        for r in range(dint):
            dst4[r * seg4:r * seg4 + pad4, :] = jnp.zeros((pad4, LANES), F32)
            dst4[r * seg4 + pad4:(r + 1) * seg4, :] = dst[pl.ds(blk + r, len4, stride=dint), :]
    for r in range(dint):
        q4[r * len4:(r + 1) * len4, :] = qf[pl.ds(r, len4, stride=dint), :]

    lane = lax.broadcasted_iota(jnp.int32, (1, LANES), 1)
    head_a = lane < ATTN_HEAD_DIM
    head_b = jnp.logical_not(head_a)

    def rows_of(start, size, stride):
        if stride == 1:
            return pl.ds(pl.multiple_of(start, blk), size)
        return pl.ds(start, size, stride=stride)

    for pat, (window, dil) in enumerate(DILATED_PATTERNS):
        assert window // dil == blk and (dil == 1 or dil % dint == 0)
        n_blk = seq // dil // blk
        n_run = min(ATTN_RUN, n_blk)
        n_res = min(ATTN_CHAINS // n_run, dil)
        runs_per_res = n_blk // n_run
        stride = 1 if dil == 1 else dil // dint
        q_src, k_src, v_src = (qf, kf, vf) if dil == 1 else (q4, k4, v4)

        def body(i, carry, pat=pat, dil=dil, n_run=n_run, n_res=n_res,
                 runs_per_res=runs_per_res, stride=stride, q_src=q_src, k_src=k_src, v_src=v_src):
            for j in range(n_res):
                r = (i // runs_per_res) * n_res + j
                n0 = (i % runs_per_res) * n_run
                if dil == 1:
                    q_start = blk * n0
                    k_start = blk + blk * (n0 - 1)
                else:
                    seg, off = r % dint, r // dint
                    q_start = seg * len4 + off + stride * blk * n0
                    k_start = seg * seg4 + pad4 + off + stride * blk * (n0 - 1)
                k_rows = rows_of(k_start, (n_run + 1) * blk, stride)
                kb = k_src[k_rows, :]
                k_a = jnp.where(head_a, kb, 0.0).astype(BF16)
                k_b = jnp.where(head_b, kb, 0.0).astype(BF16)
                vb = v_src[k_rows, :].astype(BF16)
                qb = q_src[rows_of(q_start, n_run * blk, stride), :].astype(BF16)
                for u in range(n_run):
                    bias = bias_ref[...]
                    if u == 0:
                        bias = jnp.where(n0 == 0, bias0_ref[...], bias)
                    q_u = qb[u * blk:(u + 1) * blk]
                    v_u = vb[u * blk:(u + 2) * blk]

                    def one_head(k_h):
                        s = lax.dot_general(q_u, k_h[u * blk:(u + 2) * blk],
                                            (((1,), (1,)), ((), ())),
                                            preferred_element_type=F32) + bias
                        m = jnp.max(s, axis=-1, keepdims=True)
                        p = jnp.exp(s - m)
                        l = jnp.sum(p, axis=-1, keepdims=True)
                        o = jnp.dot(p.astype(BF16), v_u, preferred_element_type=F32)
                        return o, m, l

                    o_a, m_a, l_a = one_head(k_a)
                    o_b, m_b, l_b = one_head(k_b)
                    rows = rows_of(q_start + u * stride * blk, blk, stride)
                    o_pat[pat, rows, :] = jnp.where(head_a, o_a, o_b)
                    m_pat[pat, rows, :] = jnp.where(head_a, m_a, m_b)
                    l_pat[pat, rows, :] = jnp.where(head_a, l_a, l_b)
            return carry

        lax.fori_loop(0, dil * n_blk // (n_run * n_res), body, 0)

    blocks_per_seg = len4 // blk

    def merge(i, carry):
        seg, c = i // blocks_per_seg, i % blocks_per_seg
        rows4 = pl.ds(pl.multiple_of(seg * len4 + c * blk, blk), blk)
        rows1 = pl.ds(seg + dint * blk * c, blk, stride=dint)
        rows = [rows1 if dil == 1 else rows4 for _, dil in DILATED_PATTERNS]
        ms = [m_pat[p, rw, :] for p, rw in enumerate(rows)]
        m_top = functools.reduce(jnp.maximum, ms)
        ws = [jnp.exp(m - m_top) for m in ms]
        num = sum(w * o_pat[p, rw, :] for p, (w, rw) in enumerate(zip(ws, rows)))
        den = sum(w * l_pat[p, rw, :] for p, (w, rw) in enumerate(zip(ws, rows)))
        o_pat[0, rows1, :] = num / den
        return carry

    assert DILATED_PATTERNS[0][1] == 1
    lax.fori_loop(0, seq // blk, merge, 0)
    o_ref[...] = o_pat[0].astype(BF16)


def _band_bias():
    qi = np.arange(ATTN_BLOCK)[:, None]
    kj = np.arange(2 * ATTN_BLOCK)[None, :]
    dist = ATTN_BLOCK + qi - kj
    band = (dist >= 0) & (dist <= ATTN_BLOCK)
    first = band & (kj >= ATTN_BLOCK)
    to_bias = lambda mk: np.where(mk, 0.0, -np.inf).astype(np.float32)
    return to_bias(band), to_bias(first)


def _attention(q, k, v):
    b, seq, width = q.shape
    bias, bias0 = _band_bias()
    n_pat = len(DILATED_PATTERNS)
    pad4 = ATTN_BLOCK * max(d for _, d in DILATED_PATTERNS) // ATTN_DEINTERLEAVE
    rows4 = seq + ATTN_DEINTERLEAVE * pad4
    col_spec = pl.BlockSpec((None, seq, LANES), lambda i, j: (i, 0, j))
    bias_spec = pl.BlockSpec(bias.shape, lambda i, j: (0, 0))
    return pl.pallas_call(
        functools.partial(_attn_kernel, seq=seq),
        out_shape=jax.ShapeDtypeStruct((b, seq, width), BF16),
        grid=(b, width // LANES),
        in_specs=[col_spec, col_spec, col_spec, bias_spec, bias_spec],
        out_specs=col_spec,
        scratch_shapes=[pltpu.VMEM((seq, LANES), F32),
                        pltpu.VMEM((ATTN_BLOCK + seq, LANES), F32),
                        pltpu.VMEM((ATTN_BLOCK + seq, LANES), F32),
                        pltpu.VMEM((seq, LANES), F32),
                        pltpu.VMEM((rows4, LANES), F32),
                        pltpu.VMEM((rows4, LANES), F32),
                        pltpu.VMEM((n_pat, seq, LANES), F32),
                        pltpu.VMEM((n_pat, seq, LANES), F32),
                        pltpu.VMEM((n_pat, seq, LANES), F32)],
        compiler_params=pltpu.CompilerParams(
            dimension_semantics=("parallel", "parallel"), vmem_limit_bytes=VMEM_LIMIT),
        name="dilated_attn",
    )(q, k, v, bias, bias0)


def _out_proj_kernel(x_ref, attn_ref, ag_ref, rec_ref, hg_ref, anw_ref, hnw_ref, fnw_ref,
                     seg_a_ref, seg_h_ref, w_ref, out_ref, *, attn_width):
    def head_norm(t, seg_ref, head_dim):
        sq = (t * t).astype(BF16)
        ms = jnp.dot(sq, seg_ref[...], preferred_element_type=F32) * (1.0 / head_dim)
        return t * lax.rsqrt(ms + NORM_EPS)

    a = attn_ref[...].astype(F32)
    y_a = head_norm(a, seg_a_ref, ATTN_HEAD_DIM) * anw_ref[...] * ag_ref[...].astype(F32)
    r = rec_ref[...].astype(F32)
    y_h = head_norm(r, seg_h_ref, HGRN_HEAD_DIM) * hnw_ref[...] * hg_ref[...].astype(F32)
    mixed = jnp.dot(y_a.astype(BF16), w_ref[0:attn_width, :], preferred_element_type=F32)
    mixed = mixed + jnp.dot(y_h.astype(BF16), w_ref[attn_width:, :], preferred_element_type=F32)
    x = x_ref[...] + mixed
    ms = jnp.mean(x * x, axis=-1, keepdims=True)
    out_ref[...] = (x * lax.rsqrt(ms + NORM_EPS)) * fnw_ref[...]


def _segment_ones(width, head_dim):
    idx = np.arange(width) // head_dim
    return jnp.asarray((idx[:, None] == idx[None, :]).astype(np.float32), BF16)


def _out_proj(x2, attn, ag, rec, hg, anw, hnw, fnw, w_bf16):
    n, d_model = x2.shape
    attn_width = attn.shape[-1]
    hgrn_width = rec.shape[-1]
    rows = PROJ_ROWS
    seg_a = _segment_ones(attn_width, ATTN_HEAD_DIM)
    seg_h = _segment_ones(hgrn_width, HGRN_HEAD_DIM)
    row_spec = lambda c: pl.BlockSpec((rows, c), lambda i: (i, 0))
    full_spec = lambda a: pl.BlockSpec(a.shape, lambda i: (0,) * a.ndim)
    return pl.pallas_call(
        functools.partial(_out_proj_kernel, attn_width=attn_width),
        out_shape=jax.ShapeDtypeStruct((n, d_model), F32),
        grid=(n // rows,),
        in_specs=[row_spec(d_model), row_spec(attn_width), row_spec(attn_width),
                  row_spec(hgrn_width), row_spec(hgrn_width), full_spec(anw), full_spec(hnw),
                  full_spec(fnw), full_spec(seg_a), full_spec(seg_h), full_spec(w_bf16)],
        out_specs=row_spec(d_model),
        compiler_params=pltpu.CompilerParams(
            dimension_semantics=("parallel",), vmem_limit_bytes=VMEM_LIMIT),
        name="out_proj",
    )(x2, attn, ag, rec, hg, anw, hnw, fnw, seg_a, seg_h, w_bf16)


def kernel(x, positions, w_in, w_out, mix_norm_w, attn_out_norm_w, hgrn_out_norm_w,
           hgrn_lb_raw, final_norm_w):
    b, seq, d_model = x.shape
    depth = w_in.shape[0]
    attn_width = attn_out_norm_w.shape[-1]
    hgrn_width = hgrn_out_norm_w.shape[-1]
    assert depth == 1 and attn_width == hgrn_width and w_in.shape[-1] == 8 * attn_width
    assert seq % (max(d for _, d in DILATED_PATTERNS) * ATTN_BLOCK) == 0
    n = b * seq
    layer = 0
    x2 = x.reshape(n, d_model)
    q, k, v, ag, rec, hg = _in_proj(
        x2, positions, mix_norm_w[layer][None], w_in[layer].astype(BF16), hgrn_lb_raw,
        attn_width, layer, seq)
    to3 = lambda t: t.reshape(b, seq, t.shape[-1])
    attn = _attention(to3(q), to3(k), to3(v))
    out = _out_proj(x2, attn.reshape(n, attn_width), ag, rec, hg,
                    attn_out_norm_w[layer][None], hgrn_out_norm_w[layer][None],
                    final_norm_w[None], w_out[layer].astype(BF16))
    return out.reshape(b, seq, d_model)
```

```python
import functools

import numpy as np
import jax
import jax.numpy as jnp
from jax import lax
from jax.experimental import pallas as pl
from jax.experimental.pallas import tpu as pltpu

F32 = jnp.float32
BF16 = jnp.bfloat16

LANES = 128
ATTN_HEAD_DIM = 64
HGRN_HEAD_DIM = 128
DILATED_PATTERNS = ((128, 1), (512, 4), (2048, 16))
ATTN_BLOCK = 128
ATTN_DEINTERLEAVE = 4
ROPE_THETA = 500000.0
ROPE_DIMS = ATTN_HEAD_DIM // 4
HGRN_CHUNK = 64
NORM_EPS = 1e-6
VMEM_LIMIT = 56 * 1024 * 1024

PROJ_ROWS = 512
ATTN_RUN = 16
ATTN_CHAINS = 16


def _silu(t):
    return t * jax.nn.sigmoid(t)


def _split3(t):
    hi = t.astype(BF16)
    r1 = t - hi.astype(F32)
    mid = r1.astype(BF16)
    lo = (r1 - mid.astype(F32)).astype(BF16)
    return hi, mid, lo


def _hgrn_tile(hq, hf, hi, gate, lbraw_ref, tri_ref, norm_w_ref, state_ref, o_ref, layer, interleave):
    chunk = HGRN_CHUNK
    rows, width = hq.shape
    n_chunks = rows // chunk
    n_heads = width // HGRN_HEAD_DIM
    heads = [slice(h * HGRN_HEAD_DIM, (h + 1) * HGRN_HEAD_DIM) for h in range(n_heads)]
    chunks = [slice(c * chunk, (c + 1) * chunk) for c in range(n_chunks)]

    raw = lbraw_ref[...]
    e = jnp.exp(raw - jnp.max(raw, axis=0, keepdims=True))
    sm = e / jnp.sum(e, axis=0, keepdims=True)
    lb = jnp.sum(sm[0:layer + 1, :], axis=0, keepdims=True)

    tri = tri_ref[...]
    ti = lax.broadcasted_iota(jnp.int32, (chunk, chunk), 0)
    si = lax.broadcasted_iota(jnp.int32, (chunk, chunk), 1)
    causal = ti >= si

    hkey, cum = [], []
    for rws in chunks:
        f = lb + (1.0 - lb) * jax.nn.sigmoid(hf[rws, :])
        hkey.append(1.0 - f)
        cum.append(sum(jnp.dot(tri, part, preferred_element_type=F32)
                       for part in _split3(jnp.log(f))))
    interleave[0]()

    q_dec, decay, att, upd = [], [], [], []
    for c, rws in enumerate(chunks):
        dec_c = jnp.exp(cum[c][chunk - 1:chunk, :])
        q_c = (hq[rws, :] * jnp.exp(cum[c])).astype(BF16)
        k_inv = hkey[c] * jnp.exp(-cum[c])
        k_end = (k_inv * dec_c).astype(BF16)
        k_inv = k_inv.astype(BF16)
        att.append([lax.dot_general(q_c[:, cs], k_inv[:, cs], (((1,), (1,)), ((), ())),
                                    preferred_element_type=F32) for cs in heads])
        upd.append([lax.dot_general(hi[rws, cs], k_end[:, cs], (((0,), (0,)), ((), ())),
                                    preferred_element_type=F32) for cs in heads])
        q_dec.append(q_c)
        decay.append(dec_c)
    interleave[1]()

    o_intra = [[jnp.dot(jnp.where(causal, att[c][h], 0.0).astype(BF16), hi[rws, cs],
                        preferred_element_type=F32) for h, cs in enumerate(heads)]
               for c, rws in enumerate(chunks)]
    interleave[2]()

    for h, cs in enumerate(heads):
        if h == n_heads // 2:
            interleave[3]()
        st = state_ref[h]
        for c, rws in enumerate(chunks):
            o = o_intra[c][h] + lax.dot_general(
                q_dec[c][:, cs], st.astype(BF16), (((1,), (1,)), ((), ())),
                preferred_element_type=F32)
            ms = jnp.mean(o * o, axis=-1, keepdims=True)
            y = (o * lax.rsqrt(ms + NORM_EPS)) * norm_w_ref[:, cs] * gate[rws, cs]
            o_ref[rws, cs] = y.astype(o_ref.dtype)
            st = decay[c][:, cs] * st + upd[c][h]
        state_ref[h] = st


def _in_proj_kernel(x_ref, pos_ref, freq_ref, spread_ref, cos_base_ref, sgn_up_ref, sgn_dn_ref,
                    nw_ref, lbraw_ref, tri_ref, hnw_ref, w_ref,
                    q_ref, k_ref, v_ref, ag_ref, yh_ref, state_ref,
                    *, width, layer, steps_per_seq):
    x = x_ref[...]
    ms = jnp.mean(x * x, axis=-1, keepdims=True)
    hn = ((x * lax.rsqrt(ms + NORM_EPS)) * nw_ref[...]).astype(BF16)

    ang = freq_ref[...] * pos_ref[...].astype(F32)
    parts = []
    for t in (jnp.cos(ang), jnp.sin(ang)):
        hi = t.astype(BF16).astype(F32)
        parts += [hi, t - hi]
    tabs = lax.dot_general(jnp.concatenate(parts, axis=0).astype(BF16), spread_ref[...],
                           (((0,), (0,)), ((), ())), preferred_element_type=F32)
    cos = tabs[:, :LANES] + cos_base_ref[...]
    sin = tabs[:, LANES:]
    sgn_up = sgn_up_ref[...]
    sgn_dn = sgn_dn_ref[...]
    half = ROPE_DIMS // 2

    def rope(t):
        cols = []
        for j in range(width // LANES):
            tj = t[:, j * LANES:(j + 1) * LANES]
            up = pltpu.roll(tj, LANES - half, 1)
            dn = pltpu.roll(tj, half, 1)
            cols.append(tj * cos + sin * (up * sgn_up + dn * sgn_dn))
        return jnp.concatenate(cols, axis=1)

    def proj(g):
        return jnp.dot(hn, w_ref[:, g * width:(g + 1) * width], preferred_element_type=F32)

    @pl.when(pl.program_id(0) % steps_per_seq == 0)
    def _():
        state_ref[...] = jnp.zeros(state_ref.shape, F32)

    scale = ATTN_HEAD_DIM ** -0.5

    def emit_q():
        q_ref[...] = (rope(proj(0)) * scale).astype(BF16)

    def emit_k():
        k_ref[...] = rope(proj(1)).astype(BF16)

    def emit_v():
        v_ref[...] = proj(2).astype(BF16)

    def emit_ag():
        ag_ref[...] = _silu(proj(3)).astype(BF16)

    hf = proj(5)
    hq = _silu(proj(4))
    hi = proj(6).astype(BF16)
    gate = _silu(proj(7))
    _hgrn_tile(hq, hf, hi, gate, lbraw_ref, tri_ref, hnw_ref, state_ref, yh_ref, layer,
               (emit_q, emit_k, emit_v, emit_ag))


def _rope_tables():
    half = ROPE_DIMS // 2
    inv_freq = ROPE_THETA ** (-np.arange(half, dtype=np.float32) * (2.0 / ROPE_DIMS))
    d = np.arange(LANES) % ATTN_HEAD_DIM
    rotary = d < ROPE_DIMS
    lane_uses = (np.arange(half)[:, None] == (d % half)[None, :]) & rotary[None, :]
    spread = np.zeros((4 * half, 2 * LANES), np.float32)
    for t in range(4):
        table = t // 2
        spread[t * half:(t + 1) * half, table * LANES:(table + 1) * LANES] = lane_uses
    cos_base = np.where(rotary, 0.0, 1.0).astype(np.float32)
    sgn_up = np.where(d < half, -1.0, 0.0).astype(np.float32)
    sgn_dn = np.where((d >= half) & rotary, 1.0, 0.0).astype(np.float32)
    return (inv_freq.astype(np.float32)[:, None], jnp.asarray(spread, BF16), cos_base[None],
            sgn_up[None], sgn_dn[None])


def _in_proj(x2, positions, norm_w, w_bf16, lb_raw, hgrn_norm_w, width, layer, seq):
    n, d_model = x2.shape
    rows = PROJ_ROWS
    pos3 = positions.reshape(n // rows, 1, rows)
    consts = _rope_tables()
    tri = jnp.asarray(np.tril(np.ones((HGRN_CHUNK, HGRN_CHUNK), np.float32)), BF16)
    row_spec = lambda c: pl.BlockSpec((rows, c), lambda i: (i, 0))
    full_spec = lambda a: pl.BlockSpec(a.shape, lambda i: (0,) * a.ndim)
    pos_spec = pl.BlockSpec((None, 1, rows), lambda i: (i, 0, 0))
    out_sds = jax.ShapeDtypeStruct((n, width), BF16)
    n_out = 5
    small = [*consts, norm_w, lb_raw, tri, hgrn_norm_w, w_bf16]
    return pl.pallas_call(
        functools.partial(_in_proj_kernel, width=width, layer=layer, steps_per_seq=seq // rows),
        out_shape=[out_sds] * n_out,
        grid=(n // rows,),
        in_specs=[row_spec(d_model), pos_spec] + [full_spec(a) for a in small],
        out_specs=[row_spec(width)] * n_out,
        scratch_shapes=[pltpu.VMEM((width // HGRN_HEAD_DIM, HGRN_HEAD_DIM, HGRN_HEAD_DIM), F32)],
        compiler_params=pltpu.CompilerParams(
            dimension_semantics=("arbitrary",), vmem_limit_bytes=VMEM_LIMIT),
        name="in_proj",
    )(x2, pos3, *small)


def _attn_kernel(q_ref, k_ref, v_ref, bias_ref, bias0_ref, seg_ref, o_ref,
                 qf, kf, vf, q4, k4, v4, o_pat, m_pat, l_pat, *, seq):
    blk = ATTN_BLOCK
    dint = ATTN_DEINTERLEAVE
    len4 = seq // dint
    pad4 = blk * max(d for _, d in DILATED_PATTERNS) // dint
    seg4 = pad4 + len4

    qf[...] = q_ref[...].astype(F32)
    for src, dst, dst4 in ((k_ref, kf, k4), (v_ref, vf, v4)):
        dst[0:blk, :] = jnp.zeros((blk, LANES), F32)
        dst[blk:blk + seq, :] = src[...].astype(F32)
        for r in range(dint):
            dst4[r * seg4:r * seg4 + pad4, :] = jnp.zeros((pad4, LANES), F32)
            dst4[r * seg4 + pad4:(r + 1) * seg4, :] = dst[pl.ds(blk + r, len4, stride=dint), :]
    for r in range(dint):
        q4[r * len4:(r + 1) * len4, :] = qf[pl.ds(r, len4, stride=dint), :]

    lane = lax.broadcasted_iota(jnp.int32, (1, LANES), 1)
    head_a = lane < ATTN_HEAD_DIM
    head_b = jnp.logical_not(head_a)

    def rows_of(start, size, stride):
        if stride == 1:
            return pl.ds(pl.multiple_of(start, blk), size)
        return pl.ds(start, size, stride=stride)

    for pat, (window, dil) in enumerate(DILATED_PATTERNS):
        assert window // dil == blk and (dil == 1 or dil % dint == 0)
        n_blk = seq // dil // blk
        n_run = min(ATTN_RUN, n_blk)
        n_res = min(ATTN_CHAINS // n_run, dil)
        runs_per_res = n_blk // n_run
        stride = 1 if dil == 1 else dil // dint
        q_src, k_src, v_src = (qf, kf, vf) if dil == 1 else (q4, k4, v4)

        def body(i, carry, pat=pat, dil=dil, n_run=n_run, n_res=n_res,
                 runs_per_res=runs_per_res, stride=stride, q_src=q_src, k_src=k_src, v_src=v_src):
            for j in range(n_res):
                r = (i // runs_per_res) * n_res + j
                n0 = (i % runs_per_res) * n_run
                if dil == 1:
                    q_start = blk * n0
                    k_start = blk + blk * (n0 - 1)
                else:
                    seg, off = r % dint, r // dint
                    q_start = seg * len4 + off + stride * blk * n0
                    k_start = seg * seg4 + pad4 + off + stride * blk * (n0 - 1)
                k_rows = rows_of(k_start, (n_run + 1) * blk, stride)
                kb = k_src[k_rows, :]
                k_a = jnp.where(head_a, kb, 0.0).astype(BF16)
                k_b = jnp.where(head_b, kb, 0.0).astype(BF16)
                vb = v_src[k_rows, :]
                v_a = jnp.where(head_a, vb, 1.0).astype(BF16)
                v_b = jnp.where(head_b, vb, 1.0).astype(BF16)
                qb = q_src[rows_of(q_start, n_run * blk, stride), :].astype(BF16)
                for u in range(n_run):
                    bias = bias_ref[...]
                    if u == 0:
                        bias = jnp.where(n0 == 0, bias0_ref[...], bias)
                    q_u = qb[u * blk:(u + 1) * blk]
                    keys = slice(u * blk, (u + 2) * blk)

                    def one_head(k_h, v_h):
                        s = lax.dot_general(q_u, k_h[keys], (((1,), (1,)), ((), ())),
                                            preferred_element_type=F32) + bias
                        m = jnp.max(s, axis=-1, keepdims=True)
                        p = jnp.exp(s - m).astype(BF16)
                        return jnp.dot(p, v_h[keys], preferred_element_type=F32), m

                    ol_a, m_a = one_head(k_a, v_a)
                    ol_b, m_b = one_head(k_b, v_b)
                    rows = rows_of(q_start + u * stride * blk, blk, stride)
                    o_pat[pat, rows, :] = jnp.where(head_a, ol_a, ol_b)
                    m_pat[pat, rows, :] = jnp.where(head_a, m_a, m_b)
                    l_pat[pat, rows, :] = jnp.where(head_a, ol_b, ol_a)
            return carry

        lax.fori_loop(0, dil * n_blk // (n_run * n_res), body, 0)

    blocks_per_seg = len4 // blk

    def merge(i, carry):
        seg, c = i // blocks_per_seg, i % blocks_per_seg
        rows4 = pl.ds(pl.multiple_of(seg * len4 + c * blk, blk), blk)
        rows1 = pl.ds(seg + dint * blk * c, blk, stride=dint)
        rows = [rows1 if dil == 1 else rows4 for _, dil in DILATED_PATTERNS]
        ms = [m_pat[p, rw, :] for p, rw in enumerate(rows)]
        m_top = functools.reduce(jnp.maximum, ms)
        ws = [jnp.exp(m - m_top) for m in ms]
        num = sum(w * o_pat[p, rw, :] for p, (w, rw) in enumerate(zip(ws, rows)))
        den = sum(w * pltpu.roll(l_pat[p, rw, :], ATTN_HEAD_DIM, 1)
                  for p, (w, rw) in enumerate(zip(ws, rows)))
        out = num / den
        ssq = jnp.dot((out * out).astype(BF16), seg_ref[...], preferred_element_type=F32)
        out = out * lax.rsqrt(ssq * (1.0 / ATTN_HEAD_DIM) + NORM_EPS)
        qf[rows1, :] = out
        return carry

    assert DILATED_PATTERNS[0][1] == 1
    lax.fori_loop(0, seq // blk, merge, 0, unroll=4)
    o_ref[...] = qf[...].astype(BF16)


def _band_bias():
    qi = np.arange(ATTN_BLOCK)[:, None]
    kj = np.arange(2 * ATTN_BLOCK)[None, :]
    dist = ATTN_BLOCK + qi - kj
    band = (dist >= 0) & (dist <= ATTN_BLOCK)
    first = band & (kj >= ATTN_BLOCK)
    to_bias = lambda mk: np.where(mk, 0.0, -np.inf).astype(np.float32)
    return to_bias(band), to_bias(first)


def _attention(q, k, v):
    b, seq, width = q.shape
    bias, bias0 = _band_bias()
    n_pat = len(DILATED_PATTERNS)
    pad4 = ATTN_BLOCK * max(d for _, d in DILATED_PATTERNS) // ATTN_DEINTERLEAVE
    rows4 = seq + ATTN_DEINTERLEAVE * pad4
    head_of_lane = np.arange(LANES) // ATTN_HEAD_DIM
    seg = jnp.asarray((head_of_lane[:, None] == head_of_lane[None, :]).astype(np.float32), BF16)
    col_spec = pl.BlockSpec((None, seq, LANES), lambda i, j: (i, 0, j))
    bias_spec = pl.BlockSpec(bias.shape, lambda i, j: (0, 0))
    seg_spec = pl.BlockSpec(seg.shape, lambda i, j: (0, 0))
    return pl.pallas_call(
        functools.partial(_attn_kernel, seq=seq),
        out_shape=jax.ShapeDtypeStruct((b, seq, width), BF16),
        grid=(b, width // LANES),
        in_specs=[col_spec, col_spec, col_spec, bias_spec, bias_spec, seg_spec],
        out_specs=col_spec,
        scratch_shapes=[pltpu.VMEM((seq, LANES), F32),
                        pltpu.VMEM((ATTN_BLOCK + seq, LANES), F32),
                        pltpu.VMEM((ATTN_BLOCK + seq, LANES), F32),
                        pltpu.VMEM((seq, LANES), F32),
                        pltpu.VMEM((rows4, LANES), F32),
                        pltpu.VMEM((rows4, LANES), F32),
                        pltpu.VMEM((n_pat, seq, LANES), F32),
                        pltpu.VMEM((n_pat, seq, LANES), F32),
                        pltpu.VMEM((n_pat, seq, LANES), F32)],
        compiler_params=pltpu.CompilerParams(
            dimension_semantics=("parallel", "parallel"), vmem_limit_bytes=VMEM_LIMIT),
        name="dilated_attn",
    )(q, k, v, bias, bias0, seg)


def _out_proj_kernel(x_ref, attn_ref, ag_ref, yh_ref, anw_ref, fnw_ref, w_ref, out_ref,
                     *, attn_width):
    y_a = (attn_ref[...].astype(F32) * anw_ref[...]) * ag_ref[...].astype(F32)
    mixed = jnp.dot(y_a.astype(BF16), w_ref[0:attn_width, :], preferred_element_type=F32)
    mixed = mixed + jnp.dot(yh_ref[...], w_ref[attn_width:, :], preferred_element_type=F32)
    x = x_ref[...] + mixed
    ms = jnp.mean(x * x, axis=-1, keepdims=True)
    out_ref[...] = (x * lax.rsqrt(ms + NORM_EPS)) * fnw_ref[...]


def _out_proj(x2, attn, ag, yh, anw, fnw, w_bf16):
    n, d_model = x2.shape
    attn_width = attn.shape[-1]
    rows = PROJ_ROWS
    row_spec = lambda c: pl.BlockSpec((rows, c), lambda i: (i, 0))
    full_spec = lambda a: pl.BlockSpec(a.shape, lambda i: (0,) * a.ndim)
    return pl.pallas_call(
        functools.partial(_out_proj_kernel, attn_width=attn_width),
        out_shape=jax.ShapeDtypeStruct((n, d_model), F32),
        grid=(n // rows,),
        in_specs=[row_spec(d_model), row_spec(attn_width), row_spec(attn_width),
                  row_spec(yh.shape[-1]), full_spec(anw), full_spec(fnw), full_spec(w_bf16)],
        out_specs=row_spec(d_model),
        compiler_params=pltpu.CompilerParams(
            dimension_semantics=("parallel",), vmem_limit_bytes=VMEM_LIMIT),
        name="out_proj",
    )(x2, attn, ag, yh, anw, fnw, w_bf16)


def kernel(x, positions, w_in, w_out, mix_norm_w, attn_out_norm_w, hgrn_out_norm_w,
           hgrn_lb_raw, final_norm_w):
    b, seq, d_model = x.shape
    depth = w_in.shape[0]
    attn_width = attn_out_norm_w.shape[-1]
    hgrn_width = hgrn_out_norm_w.shape[-1]
    assert depth == 1 and attn_width == hgrn_width and w_in.shape[-1] == 8 * attn_width
    assert seq % (max(d for _, d in DILATED_PATTERNS) * ATTN_BLOCK) == 0
    n = b * seq
    layer = 0
    x2 = x.reshape(n, d_model)
    q, k, v, ag, yh = _in_proj(
        x2, positions, mix_norm_w[layer][None], w_in[layer].astype(BF16), hgrn_lb_raw,
        hgrn_out_norm_w[layer][None], attn_width, layer, seq)
    to3 = lambda t: t.reshape(b, seq, t.shape[-1])
    attn = _attention(to3(q), to3(k), to3(v))
    out = _out_proj(x2, attn.reshape(n, attn_width), ag, yh, attn_out_norm_w[layer][None],
                    final_norm_w[None], w_out[layer].astype(BF16))
    return out.reshape(b, seq, d_model)
```

```python
import functools

import numpy as np
import jax
import jax.numpy as jnp
from jax import lax
from jax.experimental import pallas as pl
from jax.experimental.pallas import tpu as pltpu

F32 = jnp.float32
BF16 = jnp.bfloat16

LANES = 128
ATTN_HEAD_DIM = 64
HGRN_HEAD_DIM = 128
DILATED_PATTERNS = ((128, 1), (512, 4), (2048, 16))
ATTN_BLOCK = 128
ATTN_DEINTERLEAVE = 4
ROPE_THETA = 500000.0
ROPE_DIMS = ATTN_HEAD_DIM // 4
HGRN_CHUNK = 64
NORM_EPS = 1e-6
VMEM_LIMIT = 56 * 1024 * 1024

PROJ_ROWS = 512
OUT_ROWS = 1024
ATTN_STEP_BLOCKS = 4


def _silu(t):
    return t * jax.nn.sigmoid(t)


def _split3(t):
    hi = t.astype(BF16)
    r1 = t - hi.astype(F32)
    mid = r1.astype(BF16)
    lo = (r1 - mid.astype(F32)).astype(BF16)
    return hi, mid, lo


def _hgrn_tile(hq, hf, hi, gate, lbraw_ref, tri_ref, norm_w_ref, state_ref, o_ref, layer, interleave):
    chunk = HGRN_CHUNK
    rows, width = hq.shape
    n_chunks = rows // chunk
    n_heads = width // HGRN_HEAD_DIM
    heads = [slice(h * HGRN_HEAD_DIM, (h + 1) * HGRN_HEAD_DIM) for h in range(n_heads)]
    chunks = [slice(c * chunk, (c + 1) * chunk) for c in range(n_chunks)]

    raw = lbraw_ref[...]
    e = jnp.exp(raw - jnp.max(raw, axis=0, keepdims=True))
    sm = e / jnp.sum(e, axis=0, keepdims=True)
    lb = jnp.sum(sm[0:layer + 1, :], axis=0, keepdims=True)

    tri = tri_ref[...]
    ti = lax.broadcasted_iota(jnp.int32, (chunk, chunk), 0)
    si = lax.broadcasted_iota(jnp.int32, (chunk, chunk), 1)
    causal = ti >= si

    hkey, cum = [], []
    for rws in chunks:
        f = lb + (1.0 - lb) * jax.nn.sigmoid(hf[rws, :])
        hkey.append(1.0 - f)
        cum.append(sum(jnp.dot(tri, part, preferred_element_type=F32)
                       for part in _split3(jnp.log(f))))
    interleave[0]()

    q_dec, decay, att, upd = [], [], [], []
    for c, rws in enumerate(chunks):
        dec_c = jnp.exp(cum[c][chunk - 1:chunk, :])
        q_c = (hq[rws, :] * jnp.exp(cum[c])).astype(BF16)
        k_inv = hkey[c] * jnp.exp(-cum[c])
        k_end = (k_inv * dec_c).astype(BF16)
        k_inv = k_inv.astype(BF16)
        att.append([lax.dot_general(q_c[:, cs], k_inv[:, cs], (((1,), (1,)), ((), ())),
                                    preferred_element_type=F32) for cs in heads])
        upd.append([lax.dot_general(hi[rws, cs], k_end[:, cs], (((0,), (0,)), ((), ())),
                                    preferred_element_type=F32) for cs in heads])
        q_dec.append(q_c)
        decay.append(dec_c)
    interleave[1]()

    o_intra = [[jnp.dot(jnp.where(causal, att[c][h], 0.0).astype(BF16), hi[rws, cs],
                        preferred_element_type=F32) for h, cs in enumerate(heads)]
               for c, rws in enumerate(chunks)]
    interleave[2]()

    for h, cs in enumerate(heads):
        if h == n_heads // 2:
            interleave[3]()
        st = state_ref[h]
        for c, rws in enumerate(chunks):
            o = o_intra[c][h] + lax.dot_general(
                q_dec[c][:, cs], st.astype(BF16), (((1,), (1,)), ((), ())),
                preferred_element_type=F32)
            ms = jnp.mean(o * o, axis=-1, keepdims=True)
            y = (o * lax.rsqrt(ms + NORM_EPS)) * norm_w_ref[:, cs] * gate[rws, cs]
            o_ref[rws, cs] = y.astype(o_ref.dtype)
            st = decay[c][:, cs] * st + upd[c][h]
        state_ref[h] = st


def _in_proj_kernel(x_ref, pos_ref, freq_ref, spread_ref, cos_base_ref, sgn_up_ref, sgn_dn_ref,
                    nw_ref, lbraw_ref, tri_ref, hnw_ref, w_ref,
                    q_ref, k_ref, v_ref, ag_ref, yh_ref, state_ref,
                    *, width, layer, steps_per_seq):
    x = x_ref[...]
    ms = jnp.mean(x * x, axis=-1, keepdims=True)
    hn = ((x * lax.rsqrt(ms + NORM_EPS)) * nw_ref[...]).astype(BF16)

    ang = freq_ref[...] * pos_ref[...].astype(F32)
    parts = []
    for t in (jnp.cos(ang), jnp.sin(ang)):
        hi = t.astype(BF16).astype(F32)
        parts += [hi, t - hi]
    tabs = lax.dot_general(jnp.concatenate(parts, axis=0).astype(BF16), spread_ref[...],
                           (((0,), (0,)), ((), ())), preferred_element_type=F32)
    cos = tabs[:, :LANES] + cos_base_ref[...]
    sin = tabs[:, LANES:]
    sgn_up = sgn_up_ref[...]
    sgn_dn = sgn_dn_ref[...]
    half = ROPE_DIMS // 2

    def rope(t):
        cols = []
        for j in range(width // LANES):
            tj = t[:, j * LANES:(j + 1) * LANES]
            up = pltpu.roll(tj, LANES - half, 1)
            dn = pltpu.roll(tj, half, 1)
            cols.append(tj * cos + sin * (up * sgn_up + dn * sgn_dn))
        return jnp.concatenate(cols, axis=1)

    def proj(g):
        return jnp.dot(hn, w_ref[:, g * width:(g + 1) * width], preferred_element_type=F32)

    @pl.when(pl.program_id(0) % steps_per_seq == 0)
    def _():
        state_ref[...] = jnp.zeros(state_ref.shape, F32)

    scale = ATTN_HEAD_DIM ** -0.5

    def emit_q():
        q_ref[...] = (rope(proj(0)) * scale).astype(BF16)

    def emit_k():
        k_ref[...] = rope(proj(1)).astype(BF16)

    def emit_v():
        v_ref[...] = proj(2).astype(BF16)

    def emit_ag():
        ag_ref[...] = _silu(proj(3)).astype(BF16)

    hf = proj(5)
    hq = _silu(proj(4))
    hi = proj(6).astype(BF16)
    gate = _silu(proj(7))
    _hgrn_tile(hq, hf, hi, gate, lbraw_ref, tri_ref, hnw_ref, state_ref, yh_ref, layer,
               (emit_q, emit_k, emit_v, emit_ag))


def _rope_tables():
    half = ROPE_DIMS // 2
    inv_freq = ROPE_THETA ** (-np.arange(half, dtype=np.float32) * (2.0 / ROPE_DIMS))
    d = np.arange(LANES) % ATTN_HEAD_DIM
    rotary = d < ROPE_DIMS
    lane_uses = (np.arange(half)[:, None] == (d % half)[None, :]) & rotary[None, :]
    spread = np.zeros((4 * half, 2 * LANES), np.float32)
    for t in range(4):
        table = t // 2
        spread[t * half:(t + 1) * half, table * LANES:(table + 1) * LANES] = lane_uses
    cos_base = np.where(rotary, 0.0, 1.0).astype(np.float32)
    sgn_up = np.where(d < half, -1.0, 0.0).astype(np.float32)
    sgn_dn = np.where((d >= half) & rotary, 1.0, 0.0).astype(np.float32)
    return (inv_freq.astype(np.float32)[:, None], jnp.asarray(spread, BF16), cos_base[None],
            sgn_up[None], sgn_dn[None])


def _in_proj(x2, positions, norm_w, w_bf16, lb_raw, hgrn_norm_w, width, layer, seq):
    n, d_model = x2.shape
    rows = PROJ_ROWS
    pos3 = positions.reshape(n // rows, 1, rows)
    consts = _rope_tables()
    tri = jnp.asarray(np.tril(np.ones((HGRN_CHUNK, HGRN_CHUNK), np.float32)), BF16)
    row_spec = lambda c: pl.BlockSpec((rows, c), lambda i: (i, 0))
    full_spec = lambda a: pl.BlockSpec(a.shape, lambda i: (0,) * a.ndim)
    pos_spec = pl.BlockSpec((None, 1, rows), lambda i: (i, 0, 0))
    out_sds = jax.ShapeDtypeStruct((n, width), BF16)
    n_out = 5
    small = [*consts, norm_w, lb_raw, tri, hgrn_norm_w, w_bf16]
    return pl.pallas_call(
        functools.partial(_in_proj_kernel, width=width, layer=layer, steps_per_seq=seq // rows),
        out_shape=[out_sds] * n_out,
        grid=(n // rows,),
        in_specs=[row_spec(d_model), pos_spec] + [full_spec(a) for a in small],
        out_specs=[row_spec(width)] * n_out,
        scratch_shapes=[pltpu.VMEM((width // HGRN_HEAD_DIM, HGRN_HEAD_DIM, HGRN_HEAD_DIM), F32)],
        compiler_params=pltpu.CompilerParams(
            dimension_semantics=("arbitrary",), vmem_limit_bytes=VMEM_LIMIT),
        name="in_proj",
    )(x2, pos3, *small)


def _attn_kernel(q_ref, k_ref, v_ref, bias_ref, bias0_ref, seg_ref, o_ref,
                 qf, kf, vf, q4, k4, v4, o_pat, m_pat, l_pat, p_even, p_odd, *, seq):
    blk = ATTN_BLOCK
    dint = ATTN_DEINTERLEAVE
    len4 = seq // dint
    pad4 = blk * max(d for _, d in DILATED_PATTERNS) // dint
    seg4 = pad4 + len4

    qf[...] = q_ref[...].astype(F32)
    for src, dst, dst4 in ((k_ref, kf, k4), (v_ref, vf, v4)):
        dst[0:blk, :] = jnp.zeros((blk, LANES), F32)
        dst[blk:blk + seq, :] = src[...].astype(F32)
        for r in range(dint):
            dst4[r * seg4:r * seg4 + pad4, :] = jnp.zeros((pad4, LANES), F32)
            dst4[r * seg4 + pad4:(r + 1) * seg4, :] = dst[pl.ds(blk + r, len4, stride=dint), :]
    for r in range(dint):
        q4[r * len4:(r + 1) * len4, :] = qf[pl.ds(r, len4, stride=dint), :]

    lane = lax.broadcasted_iota(jnp.int32, (1, LANES), 1)
    head_a = lane < ATTN_HEAD_DIM
    head_b = jnp.logical_not(head_a)

    def rows_of(start, size, stride):
        if stride == 1:
            return pl.ds(pl.multiple_of(start, blk), size)
        return pl.ds(start, size, stride=stride)

    for pat, (window, dil) in enumerate(DILATED_PATTERNS):
        assert window // dil == blk and (dil == 1 or dil % dint == 0)
        n_blk = seq // dil // blk
        n_run = min(ATTN_STEP_BLOCKS, n_blk)
        n_res = min(ATTN_STEP_BLOCKS // n_run, dil)
        runs_per_res = n_blk // n_run
        n_steps = dil * n_blk // (n_run * n_res)
        stride = 1 if dil == 1 else dil // dint
        q_src, k_src, v_src = (qf, kf, vf) if dil == 1 else (q4, k4, v4)

        def runs_of(i, dil=dil, n_run=n_run, n_res=n_res, runs_per_res=runs_per_res,
                    stride=stride):
            for j in range(n_res):
                r = (i // runs_per_res) * n_res + j
                n0 = (i % runs_per_res) * n_run
                if dil == 1:
                    yield n0, blk * n0, blk + blk * (n0 - 1)
                else:
                    seg, off = r % dint, r // dint
                    yield (n0, seg * len4 + off + stride * blk * n0,
                           seg * seg4 + pad4 + off + stride * blk * (n0 - 1))

        def scores(i, p_buf, pat=pat, n_run=n_run, stride=stride, q_src=q_src, k_src=k_src,
                   runs_of=runs_of):
            for j, (n0, q_start, k_start) in enumerate(runs_of(i)):
                kb = k_src[rows_of(k_start, (n_run + 1) * blk, stride), :]
                k_a = jnp.where(head_a, kb, 0.0).astype(BF16)
                k_b = jnp.where(head_b, kb, 0.0).astype(BF16)
                qb = q_src[rows_of(q_start, n_run * blk, stride), :].astype(BF16)
                for u in range(n_run):
                    bias = bias_ref[...]
                    if u == 0:
                        bias = jnp.where(n0 == 0, bias0_ref[...], bias)
                    q_u = qb[u * blk:(u + 1) * blk]
                    ms = []
                    for h, k_h in enumerate((k_a, k_b)):
                        s = lax.dot_general(q_u, k_h[u * blk:(u + 2) * blk],
                                            (((1,), (1,)), ((), ())),
                                            preferred_element_type=F32) + bias
                        m = jnp.max(s, axis=-1, keepdims=True)
                        p_buf[(j * n_run + u) * 2 + h] = jnp.exp(s - m).astype(BF16)
                        ms.append(m)
                    rows = rows_of(q_start + u * stride * blk, blk, stride)
                    m_pat[pat, rows, :] = jnp.where(head_a, ms[0], ms[1])

        def values(i, p_buf, pat=pat, n_run=n_run, stride=stride, v_src=v_src, runs_of=runs_of):
            for j, (n0, q_start, k_start) in enumerate(runs_of(i)):
                vb = v_src[rows_of(k_start, (n_run + 1) * blk, stride), :]
                v_a = jnp.where(head_a, vb, 1.0).astype(BF16)
                v_b = jnp.where(head_b, vb, 1.0).astype(BF16)
                for u in range(n_run):
                    ol = [jnp.dot(p_buf[(j * n_run + u) * 2 + h],
                                  v_h[u * blk:(u + 2) * blk], preferred_element_type=F32)
                          for h, v_h in enumerate((v_a, v_b))]
                    rows = rows_of(q_start + u * stride * blk, blk, stride)
                    o_pat[pat, rows, :] = jnp.where(head_a, ol[0], ol[1])
                    l_pat[pat, rows, :] = jnp.where(head_a, ol[1], ol[0])

        def body(t, carry, scores=scores, values=values):
            odd = 2 * t + 1
            scores(odd, p_odd)
            values(odd - 1, p_even)
            scores(odd + 1, p_even)
            values(odd, p_odd)
            return carry

        assert n_steps % 2 == 0
        scores(0, p_even)
        lax.fori_loop(0, (n_steps - 2) // 2, body, 0)
        scores(n_steps - 1, p_odd)
        values(n_steps - 2, p_even)
        values(n_steps - 1, p_odd)

    blocks_per_seg = len4 // blk

    def merge(i, carry):
        seg, c = i // blocks_per_seg, i % blocks_per_seg
        rows4 = pl.ds(pl.multiple_of(seg * len4 + c * blk, blk), blk)
        rows1 = pl.ds(seg + dint * blk * c, blk, stride=dint)
        rows = [rows1 if dil == 1 else rows4 for _, dil in DILATED_PATTERNS]
        ms = [m_pat[p, rw, :] for p, rw in enumerate(rows)]
        m_top = functools.reduce(jnp.maximum, ms)
        ws = [jnp.exp(m - m_top) for m in ms]
        num = sum(w * o_pat[p, rw, :] for p, (w, rw) in enumerate(zip(ws, rows)))
        den = sum(w * pltpu.roll(l_pat[p, rw, :], ATTN_HEAD_DIM, 1)
                  for p, (w, rw) in enumerate(zip(ws, rows)))
        out = num / den
        ssq = jnp.dot((out * out).astype(BF16), seg_ref[...], preferred_element_type=F32)
        out = out * lax.rsqrt(ssq * (1.0 / ATTN_HEAD_DIM) + NORM_EPS)
        qf[rows1, :] = out
        return carry

    assert DILATED_PATTERNS[0][1] == 1
    lax.fori_loop(0, seq // blk, merge, 0, unroll=4)
    o_ref[...] = qf[...].astype(BF16)


def _band_bias():
    qi = np.arange(ATTN_BLOCK)[:, None]
    kj = np.arange(2 * ATTN_BLOCK)[None, :]
    dist = ATTN_BLOCK + qi - kj
    band = (dist >= 0) & (dist <= ATTN_BLOCK)
    first = band & (kj >= ATTN_BLOCK)
    to_bias = lambda mk: np.where(mk, 0.0, -np.inf).astype(np.float32)
    return to_bias(band), to_bias(first)


def _attention(q, k, v):
    b, seq, width = q.shape
    bias, bias0 = _band_bias()
    n_pat = len(DILATED_PATTERNS)
    pad4 = ATTN_BLOCK * max(d for _, d in DILATED_PATTERNS) // ATTN_DEINTERLEAVE
    rows4 = seq + ATTN_DEINTERLEAVE * pad4
    head_of_lane = np.arange(LANES) // ATTN_HEAD_DIM
    seg = jnp.asarray((head_of_lane[:, None] == head_of_lane[None, :]).astype(np.float32), BF16)
    col_spec = pl.BlockSpec((None, seq, LANES), lambda i, j: (i, 0, j))
    bias_spec = pl.BlockSpec(bias.shape, lambda i, j: (0, 0))
    seg_spec = pl.BlockSpec(seg.shape, lambda i, j: (0, 0))
    return pl.pallas_call(
        functools.partial(_attn_kernel, seq=seq),
        out_shape=jax.ShapeDtypeStruct((b, seq, width), BF16),
        grid=(b, width // LANES),
        in_specs=[col_spec, col_spec, col_spec, bias_spec, bias_spec, seg_spec],
        out_specs=col_spec,
        scratch_shapes=[pltpu.VMEM((seq, LANES), F32),
                        pltpu.VMEM((ATTN_BLOCK + seq, LANES), F32),
                        pltpu.VMEM((ATTN_BLOCK + seq, LANES), F32),
                        pltpu.VMEM((seq, LANES), F32),
                        pltpu.VMEM((rows4, LANES), F32),
                        pltpu.VMEM((rows4, LANES), F32),
                        pltpu.VMEM((n_pat, seq, LANES), F32),
                        pltpu.VMEM((n_pat, seq, LANES), F32),
                        pltpu.VMEM((n_pat, seq, LANES), F32),
                        pltpu.VMEM((2 * ATTN_STEP_BLOCKS, ATTN_BLOCK, 2 * ATTN_BLOCK), BF16),
                        pltpu.VMEM((2 * ATTN_STEP_BLOCKS, ATTN_BLOCK, 2 * ATTN_BLOCK), BF16)],
        compiler_params=pltpu.CompilerParams(
            dimension_semantics=("parallel", "parallel"), vmem_limit_bytes=VMEM_LIMIT),
        name="dilated_attn",
    )(q, k, v, bias, bias0, seg)


def _out_proj_kernel(x_ref, attn_ref, ag_ref, yh_ref, anw_ref, fnw_ref, w_ref, out_ref,
                     *, attn_width):
    y_a = (attn_ref[...].astype(F32) * anw_ref[...]) * ag_ref[...].astype(F32)
    mixed = jnp.dot(y_a.astype(BF16), w_ref[0:attn_width, :], preferred_element_type=F32)
    mixed = mixed + jnp.dot(yh_ref[...], w_ref[attn_width:, :], preferred_element_type=F32)
    x = x_ref[...] + mixed
    ms = jnp.mean(x * x, axis=-1, keepdims=True)
    out_ref[...] = (x * lax.rsqrt(ms + NORM_EPS)) * fnw_ref[...]


def _out_proj(x2, attn, ag, yh, anw, fnw, w_bf16):
    n, d_model = x2.shape
    attn_width = attn.shape[-1]
    rows = OUT_ROWS
    row_spec = lambda c: pl.BlockSpec((rows, c), lambda i: (i, 0))
    full_spec = lambda a: pl.BlockSpec(a.shape, lambda i: (0,) * a.ndim)
    return pl.pallas_call(
        functools.partial(_out_proj_kernel, attn_width=attn_width),
        out_shape=jax.ShapeDtypeStruct((n, d_model), F32),
        grid=(n // rows,),
        in_specs=[row_spec(d_model), row_spec(attn_width), row_spec(attn_width),
                  row_spec(yh.shape[-1]), full_spec(anw), full_spec(fnw), full_spec(w_bf16)],
        out_specs=row_spec(d_model),
        compiler_params=pltpu.CompilerParams(
            dimension_semantics=("parallel",), vmem_limit_bytes=VMEM_LIMIT),
        name="out_proj",
    )(x2, attn, ag, yh, anw, fnw, w_bf16)


def kernel(x, positions, w_in, w_out, mix_norm_w, attn_out_norm_w, hgrn_out_norm_w,
           hgrn_lb_raw, final_norm_w):
    b, seq, d_model = x.shape
    depth = w_in.shape[0]
    attn_width = attn_out_norm_w.shape[-1]
    hgrn_width = hgrn_out_norm_w.shape[-1]
    assert depth == 1 and attn_width == hgrn_width and w_in.shape[-1] == 8 * attn_width
    assert seq % (max(d for _, d in DILATED_PATTERNS) * ATTN_BLOCK) == 0
    n = b * seq
    layer = 0
    x2 = x.reshape(n, d_model)
    q, k, v, ag, yh = _in_proj(
        x2, positions, mix_norm_w[layer][None], w_in[layer].astype(BF16), hgrn_lb_raw,
        hgrn_out_norm_w[layer][None], attn_width, layer, seq)
    to3 = lambda t: t.reshape(b, seq, t.shape[-1])
    attn = _attention(to3(q), to3(k), to3(v))
    out = _out_proj(x2, attn.reshape(n, attn_width), ag, yh, attn_out_norm_w[layer][None],
                    final_norm_w[None], w_out[layer].astype(BF16))
    return out.reshape(b, seq, d_model)
```

```python
import functools

import numpy as np
import jax
import jax.numpy as jnp
from jax import lax
from jax.experimental import pallas as pl
from jax.experimental.pallas import tpu as pltpu

F32 = jnp.float32
BF16 = jnp.bfloat16

LANES = 128
ATTN_HEAD_DIM = 64
HGRN_HEAD_DIM = 128
DILATED_PATTERNS = ((128, 1), (512, 4), (2048, 16))
ATTN_BLOCK = 128
ATTN_DEINTERLEAVE = 4
ROPE_THETA = 500000.0
ROPE_DIMS = ATTN_HEAD_DIM // 4
HGRN_CHUNK = 64
NORM_EPS = 1e-6
VMEM_LIMIT = 56 * 1024 * 1024

PROJ_ROWS = 512
OUT_ROWS = 1024
ATTN_RUN = 32
ATTN_CHAINS = 32


def _silu(t):
    return t * jax.nn.sigmoid(t)


def _split3(t):
    hi = t.astype(BF16)
    r1 = t - hi.astype(F32)
    mid = r1.astype(BF16)
    lo = (r1 - mid.astype(F32)).astype(BF16)
    return hi, mid, lo


def _hgrn_tile(hq, hf, hi, gate, lbraw_ref, tri_ref, norm_w_ref, state_ref, o_ref, layer, interleave):
    chunk = HGRN_CHUNK
    rows, width = hq.shape
    n_chunks = rows // chunk
    n_heads = width // HGRN_HEAD_DIM
    heads = [slice(h * HGRN_HEAD_DIM, (h + 1) * HGRN_HEAD_DIM) for h in range(n_heads)]
    chunks = [slice(c * chunk, (c + 1) * chunk) for c in range(n_chunks)]

    raw = lbraw_ref[...]
    e = jnp.exp(raw - jnp.max(raw, axis=0, keepdims=True))
    sm = e / jnp.sum(e, axis=0, keepdims=True)
    lb = jnp.sum(sm[0:layer + 1, :], axis=0, keepdims=True)

    tri = tri_ref[...]
    ti = lax.broadcasted_iota(jnp.int32, (chunk, chunk), 0)
    si = lax.broadcasted_iota(jnp.int32, (chunk, chunk), 1)
    causal = ti >= si

    hkey, cum = [], []
    for rws in chunks:
        f = lb + (1.0 - lb) * jax.nn.sigmoid(hf[rws, :])
        hkey.append(1.0 - f)
        cum.append(sum(jnp.dot(tri, part, preferred_element_type=F32)
                       for part in _split3(jnp.log(f))))
    interleave[0]()

    q_dec, decay, att, upd = [], [], [], []
    for c, rws in enumerate(chunks):
        dec_c = jnp.exp(cum[c][chunk - 1:chunk, :])
        q_c = (hq[rws, :] * jnp.exp(cum[c])).astype(BF16)
        k_inv = hkey[c] * jnp.exp(-cum[c])
        k_end = (k_inv * dec_c).astype(BF16)
        k_inv = k_inv.astype(BF16)
        att.append([lax.dot_general(q_c[:, cs], k_inv[:, cs], (((1,), (1,)), ((), ())),
                                    preferred_element_type=F32) for cs in heads])
        upd.append([lax.dot_general(hi[rws, cs], k_end[:, cs], (((0,), (0,)), ((), ())),
                                    preferred_element_type=F32) for cs in heads])
        q_dec.append(q_c)
        decay.append(dec_c)
    interleave[1]()

    o_intra = [[jnp.dot(jnp.where(causal, att[c][h], 0.0).astype(BF16), hi[rws, cs],
                        preferred_element_type=F32) for h, cs in enumerate(heads)]
               for c, rws in enumerate(chunks)]
    interleave[2]()

    for h, cs in enumerate(heads):
        if h == n_heads // 2:
            interleave[3]()
        st = state_ref[h]
        for c, rws in enumerate(chunks):
            o = o_intra[c][h] + lax.dot_general(
                q_dec[c][:, cs], st.astype(BF16), (((1,), (1,)), ((), ())),
                preferred_element_type=F32)
            ms = jnp.mean(o * o, axis=-1, keepdims=True)
            y = (o * lax.rsqrt(ms + NORM_EPS)) * norm_w_ref[:, cs] * gate[rws, cs]
            o_ref[rws, cs] = y.astype(o_ref.dtype)
            st = decay[c][:, cs] * st + upd[c][h]
        state_ref[h] = st


def _in_proj_kernel(x_ref, pos_ref, freq_ref, spread_ref, cos_base_ref, sgn_up_ref, sgn_dn_ref,
                    nw_ref, lbraw_ref, tri_ref, hnw_ref, w_ref,
                    q_ref, k_ref, v_ref, ag_ref, yh_ref, state_ref,
                    *, width, layer, steps_per_seq):
    x = x_ref[...]
    ms = jnp.mean(x * x, axis=-1, keepdims=True)
    hn = ((x * lax.rsqrt(ms + NORM_EPS)) * nw_ref[...]).astype(BF16)

    ang = freq_ref[...] * pos_ref[...].astype(F32)
    parts = []
    for t in (jnp.cos(ang), jnp.sin(ang)):
        hi = t.astype(BF16).astype(F32)
        parts += [hi, t - hi]
    tabs = lax.dot_general(jnp.concatenate(parts, axis=0).astype(BF16), spread_ref[...],
                           (((0,), (0,)), ((), ())), preferred_element_type=F32)
    cos = tabs[:, :LANES] + cos_base_ref[...]
    sin = tabs[:, LANES:]
    sgn_up = sgn_up_ref[...]
    sgn_dn = sgn_dn_ref[...]
    half = ROPE_DIMS // 2

    def rope(t):
        cols = []
        for j in range(width // LANES):
            tj = t[:, j * LANES:(j + 1) * LANES]
            up = pltpu.roll(tj, LANES - half, 1)
            dn = pltpu.roll(tj, half, 1)
            cols.append(tj * cos + sin * (up * sgn_up + dn * sgn_dn))
        return jnp.concatenate(cols, axis=1)

    def proj(g):
        return jnp.dot(hn, w_ref[:, g * width:(g + 1) * width], preferred_element_type=F32)

    @pl.when(pl.program_id(0) % steps_per_seq == 0)
    def _():
        state_ref[...] = jnp.zeros(state_ref.shape, F32)

    scale = ATTN_HEAD_DIM ** -0.5

    def emit_q():
        q_ref[...] = (rope(proj(0)) * scale).astype(BF16)

    def emit_k():
        k_ref[...] = rope(proj(1)).astype(BF16)

    def emit_v():
        v_ref[...] = proj(2).astype(BF16)

    def emit_ag():
        ag_ref[...] = _silu(proj(3)).astype(BF16)

    hf = proj(5)
    hq = _silu(proj(4))
    hi = proj(6).astype(BF16)
    gate = _silu(proj(7))
    _hgrn_tile(hq, hf, hi, gate, lbraw_ref, tri_ref, hnw_ref, state_ref, yh_ref, layer,
               (emit_q, emit_k, emit_v, emit_ag))


def _rope_tables():
    half = ROPE_DIMS // 2
    inv_freq = ROPE_THETA ** (-np.arange(half, dtype=np.float32) * (2.0 / ROPE_DIMS))
    d = np.arange(LANES) % ATTN_HEAD_DIM
    rotary = d < ROPE_DIMS
    lane_uses = (np.arange(half)[:, None] == (d % half)[None, :]) & rotary[None, :]
    spread = np.zeros((4 * half, 2 * LANES), np.float32)
    for t in range(4):
        table = t // 2
        spread[t * half:(t + 1) * half, table * LANES:(table + 1) * LANES] = lane_uses
    cos_base = np.where(rotary, 0.0, 1.0).astype(np.float32)
    sgn_up = np.where(d < half, -1.0, 0.0).astype(np.float32)
    sgn_dn = np.where((d >= half) & rotary, 1.0, 0.0).astype(np.float32)
    return (inv_freq.astype(np.float32)[:, None], jnp.asarray(spread, BF16), cos_base[None],
            sgn_up[None], sgn_dn[None])


def _in_proj(x2, positions, norm_w, w_bf16, lb_raw, hgrn_norm_w, width, layer, seq):
    n, d_model = x2.shape
    rows = PROJ_ROWS
    pos3 = positions.reshape(n // rows, 1, rows)
    consts = _rope_tables()
    tri = jnp.asarray(np.tril(np.ones((HGRN_CHUNK, HGRN_CHUNK), np.float32)), BF16)
    row_spec = lambda c: pl.BlockSpec((rows, c), lambda i: (i, 0))
    full_spec = lambda a: pl.BlockSpec(a.shape, lambda i: (0,) * a.ndim)
    pos_spec = pl.BlockSpec((None, 1, rows), lambda i: (i, 0, 0))
    out_sds = jax.ShapeDtypeStruct((n, width), BF16)
    n_out = 5
    small = [*consts, norm_w, lb_raw, tri, hgrn_norm_w, w_bf16]
    return pl.pallas_call(
        functools.partial(_in_proj_kernel, width=width, layer=layer, steps_per_seq=seq // rows),
        out_shape=[out_sds] * n_out,
        grid=(n // rows,),
        in_specs=[row_spec(d_model), pos_spec] + [full_spec(a) for a in small],
        out_specs=[row_spec(width)] * n_out,
        scratch_shapes=[pltpu.VMEM((width // HGRN_HEAD_DIM, HGRN_HEAD_DIM, HGRN_HEAD_DIM), F32)],
        compiler_params=pltpu.CompilerParams(
            dimension_semantics=("arbitrary",), vmem_limit_bytes=VMEM_LIMIT),
        name="in_proj",
    )(x2, pos3, *small)


def _attn_kernel(q_ref, k_ref, v_ref, bias_ref, bias0_ref, seg_ref, o_ref,
                 qf, kf, vf, q4, k4, v4, o_pat, m_pat, l_pat, *, seq):
    blk = ATTN_BLOCK
    dint = ATTN_DEINTERLEAVE
    len4 = seq // dint
    pad4 = blk * max(d for _, d in DILATED_PATTERNS) // dint
    seg4 = pad4 + len4

    qf[...] = q_ref[...].astype(F32)
    for src, dst, dst4 in ((k_ref, kf, k4), (v_ref, vf, v4)):
        dst[0:blk, :] = jnp.zeros((blk, LANES), F32)
        dst[blk:blk + seq, :] = src[...].astype(F32)
        for r in range(dint):
            dst4[r * seg4:r * seg4 + pad4, :] = jnp.zeros((pad4, LANES), F32)
            dst4[r * seg4 + pad4:(r + 1) * seg4, :] = dst[pl.ds(blk + r, len4, stride=dint), :]
    for r in range(dint):
        q4[r * len4:(r + 1) * len4, :] = qf[pl.ds(r, len4, stride=dint), :]

    lane = lax.broadcasted_iota(jnp.int32, (1, LANES), 1)
    head_a = lane < ATTN_HEAD_DIM
    head_b = jnp.logical_not(head_a)

    def rows_of(start, size, stride):
        if stride == 1:
            return pl.ds(pl.multiple_of(start, blk), size)
        return pl.ds(start, size, stride=stride)

    for pat, (window, dil) in enumerate(DILATED_PATTERNS):
        assert window // dil == blk and (dil == 1 or dil % dint == 0)
        n_blk = seq // dil // blk
        n_run = min(ATTN_RUN, n_blk)
        n_res = min(ATTN_CHAINS // n_run, dil)
        runs_per_res = n_blk // n_run
        stride = 1 if dil == 1 else dil // dint
        q_src, k_src, v_src = (qf, kf, vf) if dil == 1 else (q4, k4, v4)

        def body(i, carry, pat=pat, dil=dil, n_run=n_run, n_res=n_res,
                 runs_per_res=runs_per_res, stride=stride, q_src=q_src, k_src=k_src, v_src=v_src):
            for j in range(n_res):
                r = (i // runs_per_res) * n_res + j
                n0 = (i % runs_per_res) * n_run
                if dil == 1:
                    q_start = blk * n0
                    k_start = blk + blk * (n0 - 1)
                else:
                    seg, off = r % dint, r // dint
                    q_start = seg * len4 + off + stride * blk * n0
                    k_start = seg * seg4 + pad4 + off + stride * blk * (n0 - 1)
                k_rows = rows_of(k_start, (n_run + 1) * blk, stride)
                kb = k_src[k_rows, :]
                k_a = jnp.where(head_a, kb, 0.0).astype(BF16)
                k_b = jnp.where(head_b, kb, 0.0).astype(BF16)
                vb = v_src[k_rows, :]
                v_a = jnp.where(head_a, vb, 1.0).astype(BF16)
                v_b = jnp.where(head_b, vb, 1.0).astype(BF16)
                qb = q_src[rows_of(q_start, n_run * blk, stride), :].astype(BF16)
                for u in range(n_run):
                    bias = bias_ref[...]
                    if u == 0:
                        bias = jnp.where(n0 == 0, bias0_ref[...], bias)
                    q_u = qb[u * blk:(u + 1) * blk]
                    keys = slice(u * blk, (u + 2) * blk)

                    def one_head(k_h, v_h):
                        s = lax.dot_general(q_u, k_h[keys], (((1,), (1,)), ((), ())),
                                            preferred_element_type=F32) + bias
                        m = jnp.max(s, axis=-1, keepdims=True)
                        p = jnp.exp(s - m).astype(BF16)
                        return jnp.dot(p, v_h[keys], preferred_element_type=F32), m

                    ol_a, m_a = one_head(k_a, v_a)
                    ol_b, m_b = one_head(k_b, v_b)
                    rows = rows_of(q_start + u * stride * blk, blk, stride)
                    o_pat[pat, rows, :] = jnp.where(head_a, ol_a, ol_b)
                    m_pat[pat, rows, :] = jnp.where(head_a, m_a, m_b)
                    l_pat[pat, rows, :] = jnp.where(head_a, ol_b, ol_a)
            return carry

        lax.fori_loop(0, dil * n_blk // (n_run * n_res), body, 0)

    blocks_per_seg = len4 // blk

    def merge(i, carry):
        seg, c = i // blocks_per_seg, i % blocks_per_seg
        rows4 = pl.ds(pl.multiple_of(seg * len4 + c * blk, blk), blk)
        rows1 = pl.ds(seg + dint * blk * c, blk, stride=dint)
        rows = [rows1 if dil == 1 else rows4 for _, dil in DILATED_PATTERNS]
        ms = [m_pat[p, rw, :] for p, rw in enumerate(rows)]
        m_top = functools.reduce(jnp.maximum, ms)
        ws = [jnp.exp(m - m_top) for m in ms]
        num = sum(w * o_pat[p, rw, :] for p, (w, rw) in enumerate(zip(ws, rows)))
        den = sum(w * pltpu.roll(l_pat[p, rw, :], ATTN_HEAD_DIM, 1)
                  for p, (w, rw) in enumerate(zip(ws, rows)))
        out = num / den
        ssq = jnp.dot((out * out).astype(BF16), seg_ref[...], preferred_element_type=F32)
        out = out * lax.rsqrt(ssq * (1.0 / ATTN_HEAD_DIM) + NORM_EPS)
        qf[rows1, :] = out
        return carry

    assert DILATED_PATTERNS[0][1] == 1
    lax.fori_loop(0, seq // blk, merge, 0, unroll=4)
    o_ref[...] = qf[...].astype(BF16)


def _band_bias():
    qi = np.arange(ATTN_BLOCK)[:, None]
    kj = np.arange(2 * ATTN_BLOCK)[None, :]
    dist = ATTN_BLOCK + qi - kj
    band = (dist >= 0) & (dist <= ATTN_BLOCK)
    first = band & (kj >= ATTN_BLOCK)
    to_bias = lambda mk: np.where(mk, 0.0, -np.inf).astype(np.float32)
    return to_bias(band), to_bias(first)


def _attention(q, k, v):
    b, seq, width = q.shape
    bias, bias0 = _band_bias()
    n_pat = len(DILATED_PATTERNS)
    pad4 = ATTN_BLOCK * max(d for _, d in DILATED_PATTERNS) // ATTN_DEINTERLEAVE
    rows4 = seq + ATTN_DEINTERLEAVE * pad4
    head_of_lane = np.arange(LANES) // ATTN_HEAD_DIM
    seg = jnp.asarray((head_of_lane[:, None] == head_of_lane[None, :]).astype(np.float32), BF16)
    col_spec = pl.BlockSpec((None, seq, LANES), lambda i, j: (i, 0, j))
    bias_spec = pl.BlockSpec(bias.shape, lambda i, j: (0, 0))
    seg_spec = pl.BlockSpec(seg.shape, lambda i, j: (0, 0))
    return pl.pallas_call(
        functools.partial(_attn_kernel, seq=seq),
        out_shape=jax.ShapeDtypeStruct((b, seq, width), BF16),
        grid=(b, width // LANES),
        in_specs=[col_spec, col_spec, col_spec, bias_spec, bias_spec, seg_spec],
        out_specs=col_spec,
        scratch_shapes=[pltpu.VMEM((seq, LANES), F32),
                        pltpu.VMEM((ATTN_BLOCK + seq, LANES), F32),
                        pltpu.VMEM((ATTN_BLOCK + seq, LANES), F32),
                        pltpu.VMEM((seq, LANES), F32),
                        pltpu.VMEM((rows4, LANES), F32),
                        pltpu.VMEM((rows4, LANES), F32),
                        pltpu.VMEM((n_pat, seq, LANES), F32),
                        pltpu.VMEM((n_pat, seq, LANES), F32),
                        pltpu.VMEM((n_pat, seq, LANES), F32)],
        compiler_params=pltpu.CompilerParams(
            dimension_semantics=("parallel", "parallel"), vmem_limit_bytes=VMEM_LIMIT),
        name="dilated_attn",
    )(q, k, v, bias, bias0, seg)


def _out_proj_kernel(x_ref, attn_ref, ag_ref, yh_ref, anw_ref, fnw_ref, w_ref, out_ref,
                     *, attn_width):
    y_a = (attn_ref[...].astype(F32) * anw_ref[...]) * ag_ref[...].astype(F32)
    mixed = jnp.dot(y_a.astype(BF16), w_ref[0:attn_width, :], preferred_element_type=F32)
    mixed = mixed + jnp.dot(yh_ref[...], w_ref[attn_width:, :], preferred_element_type=F32)
    x = x_ref[...] + mixed
    ms = jnp.mean(x * x, axis=-1, keepdims=True)
    out_ref[...] = (x * lax.rsqrt(ms + NORM_EPS)) * fnw_ref[...]


def _out_proj(x2, attn, ag, yh, anw, fnw, w_bf16):
    n, d_model = x2.shape
    attn_width = attn.shape[-1]
    rows = OUT_ROWS
    row_spec = lambda c: pl.BlockSpec((rows, c), lambda i: (i, 0))
    full_spec = lambda a: pl.BlockSpec(a.shape, lambda i: (0,) * a.ndim)
    return pl.pallas_call(
        functools.partial(_out_proj_kernel, attn_width=attn_width),
        out_shape=jax.ShapeDtypeStruct((n, d_model), F32),
        grid=(n // rows,),
        in_specs=[row_spec(d_model), row_spec(attn_width), row_spec(attn_width),
                  row_spec(yh.shape[-1]), full_spec(anw), full_spec(fnw), full_spec(w_bf16)],
        out_specs=row_spec(d_model),
        compiler_params=pltpu.CompilerParams(
            dimension_semantics=("parallel",), vmem_limit_bytes=VMEM_LIMIT),
        name="out_proj",
    )(x2, attn, ag, yh, anw, fnw, w_bf16)


def kernel(x, positions, w_in, w_out, mix_norm_w, attn_out_norm_w, hgrn_out_norm_w,
           hgrn_lb_raw, final_norm_w):
    b, seq, d_model = x.shape
    depth = w_in.shape[0]
    attn_width = attn_out_norm_w.shape[-1]
    hgrn_width = hgrn_out_norm_w.shape[-1]
    assert depth == 1 and attn_width == hgrn_width and w_in.shape[-1] == 8 * attn_width
    assert seq % (max(d for _, d in DILATED_PATTERNS) * ATTN_BLOCK) == 0
    n = b * seq
    layer = 0
    x2 = x.reshape(n, d_model)
    q, k, v, ag, yh = _in_proj(
        x2, positions, mix_norm_w[layer][None], w_in[layer].astype(BF16), hgrn_lb_raw,
        hgrn_out_norm_w[layer][None], attn_width, layer, seq)
    to3 = lambda t: t.reshape(b, seq, t.shape[-1])
    attn = _attention(to3(q), to3(k), to3(v))
    out = _out_proj(x2, attn.reshape(n, attn_width), ag, yh, attn_out_norm_w[layer][None],
                    final_norm_w[None], w_out[layer].astype(BF16))
    return out.reshape(b, seq, d_model)
```

```python
import functools

import numpy as np
import jax
import jax.numpy as jnp
from jax import lax
from jax.experimental import pallas as pl
from jax.experimental.pallas import tpu as pltpu

F32 = jnp.float32
BF16 = jnp.bfloat16

LANES = 128
ATTN_HEAD_DIM = 64
HGRN_HEAD_DIM = 128
DILATED_PATTERNS = ((128, 1), (512, 4), (2048, 16))
ATTN_BLOCK = 128
ATTN_DEINTERLEAVE = 4
ROPE_THETA = 500000.0
ROPE_DIMS = ATTN_HEAD_DIM // 4
HGRN_CHUNK = 64
NORM_EPS = 1e-6
VMEM_LIMIT = 56 * 1024 * 1024

PROJ_ROWS = 512
PROJ_TILES = 2
OUT_ROWS = 1024
ATTN_RUN = 32
ATTN_CHAINS = 32


def _silu(t):
    return t * jax.nn.sigmoid(t)


def _split3(t):
    hi = t.astype(BF16)
    r1 = t - hi.astype(F32)
    mid = r1.astype(BF16)
    lo = (r1 - mid.astype(F32)).astype(BF16)
    return hi, mid, lo


def _hgrn_tile(hq, hf, hi, gate, lbraw_ref, tri_ref, norm_w_ref, state_ref, o_ref, layer, interleave):
    chunk = HGRN_CHUNK
    rows, width = hq.shape
    n_chunks = rows // chunk
    n_heads = width // HGRN_HEAD_DIM
    heads = [slice(h * HGRN_HEAD_DIM, (h + 1) * HGRN_HEAD_DIM) for h in range(n_heads)]
    chunks = [slice(c * chunk, (c + 1) * chunk) for c in range(n_chunks)]

    raw = lbraw_ref[...]
    e = jnp.exp(raw - jnp.max(raw, axis=0, keepdims=True))
    sm = e / jnp.sum(e, axis=0, keepdims=True)
    lb = jnp.sum(sm[0:layer + 1, :], axis=0, keepdims=True)

    tri = tri_ref[...]
    ti = lax.broadcasted_iota(jnp.int32, (chunk, chunk), 0)
    si = lax.broadcasted_iota(jnp.int32, (chunk, chunk), 1)
    causal = ti >= si

    hkey, cum = [], []
    for rws in chunks:
        f = lb + (1.0 - lb) * jax.nn.sigmoid(hf[rws, :])
        hkey.append(1.0 - f)
        cum.append(sum(jnp.dot(tri, part, preferred_element_type=F32)
                       for part in _split3(jnp.log(f))))
    interleave[0]()

    q_dec, decay, att, upd = [], [], [], []
    for c, rws in enumerate(chunks):
        dec_c = jnp.exp(cum[c][chunk - 1:chunk, :])
        q_c = (hq[rws, :] * jnp.exp(cum[c])).astype(BF16)
        k_inv = hkey[c] * jnp.exp(-cum[c])
        k_end = (k_inv * dec_c).astype(BF16)
        k_inv = k_inv.astype(BF16)
        att.append([lax.dot_general(q_c[:, cs], k_inv[:, cs], (((1,), (1,)), ((), ())),
                                    preferred_element_type=F32) for cs in heads])
        upd.append([lax.dot_general(hi[rws, cs], k_end[:, cs], (((0,), (0,)), ((), ())),
                                    preferred_element_type=F32) for cs in heads])
        q_dec.append(q_c)
        decay.append(dec_c)
    interleave[1]()

    o_intra = [[jnp.dot(jnp.where(causal, att[c][h], 0.0).astype(BF16), hi[rws, cs],
                        preferred_element_type=F32) for h, cs in enumerate(heads)]
               for c, rws in enumerate(chunks)]
    interleave[2]()

    for h, cs in enumerate(heads):
        if h == n_heads // 2:
            interleave[3]()
        st = state_ref[h]
        for c, rws in enumerate(chunks):
            o = o_intra[c][h] + lax.dot_general(
                q_dec[c][:, cs], st.astype(BF16), (((1,), (1,)), ((), ())),
                preferred_element_type=F32)
            ms = jnp.mean(o * o, axis=-1, keepdims=True)
            y = (o * lax.rsqrt(ms + NORM_EPS)) * norm_w_ref[:, cs] * gate[rws, cs]
            o_ref[rws, cs] = y.astype(o_ref.dtype)
            st = decay[c][:, cs] * st + upd[c][h]
        state_ref[h] = st


def _in_proj_kernel(x_ref, pos_ref, freq_ref, spread_ref, cos_base_ref, sgn_up_ref, sgn_dn_ref,
                    nw_ref, lbraw_ref, tri_ref, hnw_ref, w_ref,
                    q_ref, k_ref, v_ref, ag_ref, yh_ref, state_ref,
                    *, width, layer, steps_per_seq):
    @pl.when(pl.program_id(0) % steps_per_seq == 0)
    def _():
        state_ref[...] = jnp.zeros(state_ref.shape, F32)

    for t in range(PROJ_TILES):
        rows = pl.ds(t * PROJ_ROWS, PROJ_ROWS)
        _in_proj_tile(x_ref.at[rows], pos_ref[:, t * PROJ_ROWS:(t + 1) * PROJ_ROWS],
                      freq_ref, spread_ref, cos_base_ref, sgn_up_ref, sgn_dn_ref,
                      nw_ref, lbraw_ref, tri_ref, hnw_ref, w_ref,
                      q_ref.at[rows], k_ref.at[rows], v_ref.at[rows], ag_ref.at[rows],
                      yh_ref.at[rows], state_ref, width, layer)


def _in_proj_tile(x_ref, pos, freq_ref, spread_ref, cos_base_ref, sgn_up_ref, sgn_dn_ref,
                  nw_ref, lbraw_ref, tri_ref, hnw_ref, w_ref,
                  q_ref, k_ref, v_ref, ag_ref, yh_ref, state_ref, width, layer):
    x = x_ref[...]
    ms = jnp.mean(x * x, axis=-1, keepdims=True)
    hn = ((x * lax.rsqrt(ms + NORM_EPS)) * nw_ref[...]).astype(BF16)

    ang = freq_ref[...] * pos.astype(F32)
    parts = []
    for t in (jnp.cos(ang), jnp.sin(ang)):
        hi = t.astype(BF16).astype(F32)
        parts += [hi, t - hi]
    tabs = lax.dot_general(jnp.concatenate(parts, axis=0).astype(BF16), spread_ref[...],
                           (((0,), (0,)), ((), ())), preferred_element_type=F32)
    cos = tabs[:, :LANES] + cos_base_ref[...]
    sin = tabs[:, LANES:]
    sgn_up = sgn_up_ref[...]
    sgn_dn = sgn_dn_ref[...]
    half = ROPE_DIMS // 2

    def rope(t):
        cols = []
        for j in range(width // LANES):
            tj = t[:, j * LANES:(j + 1) * LANES]
            up = pltpu.roll(tj, LANES - half, 1)
            dn = pltpu.roll(tj, half, 1)
            cols.append(tj * cos + sin * (up * sgn_up + dn * sgn_dn))
        return jnp.concatenate(cols, axis=1)

    def proj(g):
        return jnp.dot(hn, w_ref[:, g * width:(g + 1) * width], preferred_element_type=F32)

    scale = ATTN_HEAD_DIM ** -0.5

    def emit_q():
        q_ref[...] = (rope(proj(0)) * scale).astype(BF16)

    def emit_k():
        k_ref[...] = rope(proj(1)).astype(BF16)

    def emit_v():
        v_ref[...] = proj(2).astype(BF16)

    def emit_ag():
        ag_ref[...] = _silu(proj(3)).astype(BF16)

    hf = proj(5)
    hq = _silu(proj(4))
    hi = proj(6).astype(BF16)
    gate = _silu(proj(7))
    _hgrn_tile(hq, hf, hi, gate, lbraw_ref, tri_ref, hnw_ref, state_ref, yh_ref, layer,
               (emit_q, emit_k, emit_v, emit_ag))


def _rope_tables():
    half = ROPE_DIMS // 2
    inv_freq = ROPE_THETA ** (-np.arange(half, dtype=np.float32) * (2.0 / ROPE_DIMS))
    d = np.arange(LANES) % ATTN_HEAD_DIM
    rotary = d < ROPE_DIMS
    lane_uses = (np.arange(half)[:, None] == (d % half)[None, :]) & rotary[None, :]
    spread = np.zeros((4 * half, 2 * LANES), np.float32)
    for t in range(4):
        table = t // 2
        spread[t * half:(t + 1) * half, table * LANES:(table + 1) * LANES] = lane_uses
    cos_base = np.where(rotary, 0.0, 1.0).astype(np.float32)
    sgn_up = np.where(d < half, -1.0, 0.0).astype(np.float32)
    sgn_dn = np.where((d >= half) & rotary, 1.0, 0.0).astype(np.float32)
    return (inv_freq.astype(np.float32)[:, None], jnp.asarray(spread, BF16), cos_base[None],
            sgn_up[None], sgn_dn[None])


def _in_proj(x2, positions, norm_w, w_bf16, lb_raw, hgrn_norm_w, width, layer, seq):
    n, d_model = x2.shape
    rows = PROJ_ROWS * PROJ_TILES
    pos3 = positions.reshape(n // rows, 1, rows)
    consts = _rope_tables()
    tri = jnp.asarray(np.tril(np.ones((HGRN_CHUNK, HGRN_CHUNK), np.float32)), BF16)
    row_spec = lambda c: pl.BlockSpec((rows, c), lambda i: (i, 0))
    full_spec = lambda a: pl.BlockSpec(a.shape, lambda i: (0,) * a.ndim)
    pos_spec = pl.BlockSpec((None, 1, rows), lambda i: (i, 0, 0))
    out_sds = jax.ShapeDtypeStruct((n, width), BF16)
    n_out = 5
    small = [*consts, norm_w, lb_raw, tri, hgrn_norm_w, w_bf16]
    return pl.pallas_call(
        functools.partial(_in_proj_kernel, width=width, layer=layer, steps_per_seq=seq // rows),
        out_shape=[out_sds] * n_out,
        grid=(n // rows,),
        in_specs=[row_spec(d_model), pos_spec] + [full_spec(a) for a in small],
        out_specs=[row_spec(width)] * n_out,
        scratch_shapes=[pltpu.VMEM((width // HGRN_HEAD_DIM, HGRN_HEAD_DIM, HGRN_HEAD_DIM), F32)],
        compiler_params=pltpu.CompilerParams(
            dimension_semantics=("arbitrary",), vmem_limit_bytes=VMEM_LIMIT),
        name="in_proj",
    )(x2, pos3, *small)


def _attn_kernel(q_ref, k_ref, v_ref, gate_ref, bias_ref, bias0_ref, seg_ref, nw_ref, o_ref,
                 qf, kf, vf, q4, k4, v4, o_pat, m_pat, l_pat, *, seq):
    blk = ATTN_BLOCK
    dint = ATTN_DEINTERLEAVE
    len4 = seq // dint
    pad4 = blk * max(d for _, d in DILATED_PATTERNS) // dint
    seg4 = pad4 + len4

    qf[...] = q_ref[...].astype(F32)
    for src, dst, dst4 in ((k_ref, kf, k4), (v_ref, vf, v4)):
        dst[0:blk, :] = jnp.zeros((blk, LANES), F32)
        dst[blk:blk + seq, :] = src[...].astype(F32)
        for r in range(dint):
            dst4[r * seg4:r * seg4 + pad4, :] = jnp.zeros((pad4, LANES), F32)
            dst4[r * seg4 + pad4:(r + 1) * seg4, :] = dst[pl.ds(blk + r, len4, stride=dint), :]
    for r in range(dint):
        q4[r * len4:(r + 1) * len4, :] = qf[pl.ds(r, len4, stride=dint), :]

    lane = lax.broadcasted_iota(jnp.int32, (1, LANES), 1)
    head_a = lane < ATTN_HEAD_DIM
    head_b = jnp.logical_not(head_a)

    def rows_of(start, size, stride):
        if stride == 1:
            return pl.ds(pl.multiple_of(start, blk), size)
        return pl.ds(start, size, stride=stride)

    for pat, (window, dil) in enumerate(DILATED_PATTERNS):
        assert window // dil == blk and (dil == 1 or dil % dint == 0)
        n_blk = seq // dil // blk
        n_run = min(ATTN_RUN, n_blk)
        n_res = min(ATTN_CHAINS // n_run, dil)
        runs_per_res = n_blk // n_run
        stride = 1 if dil == 1 else dil // dint
        q_src, k_src, v_src = (qf, kf, vf) if dil == 1 else (q4, k4, v4)

        def body(i, carry, pat=pat, dil=dil, n_run=n_run, n_res=n_res,
                 runs_per_res=runs_per_res, stride=stride, q_src=q_src, k_src=k_src, v_src=v_src):
            for j in range(n_res):
                r = (i // runs_per_res) * n_res + j
                n0 = (i % runs_per_res) * n_run
                if dil == 1:
                    q_start = blk * n0
                    k_start = blk + blk * (n0 - 1)
                else:
                    seg, off = r % dint, r // dint
                    q_start = seg * len4 + off + stride * blk * n0
                    k_start = seg * seg4 + pad4 + off + stride * blk * (n0 - 1)
                k_rows = rows_of(k_start, (n_run + 1) * blk, stride)
                kb = k_src[k_rows, :]
                k_a = jnp.where(head_a, kb, 0.0).astype(BF16)
                k_b = jnp.where(head_b, kb, 0.0).astype(BF16)
                vb = v_src[k_rows, :]
                v_a = jnp.where(head_a, vb, 1.0).astype(BF16)
                v_b = jnp.where(head_b, vb, 1.0).astype(BF16)
                qb = q_src[rows_of(q_start, n_run * blk, stride), :].astype(BF16)
                for u in range(n_run):
                    bias = bias_ref[...]
                    if u == 0:
                        bias = jnp.where(n0 == 0, bias0_ref[...], bias)
                    q_u = qb[u * blk:(u + 1) * blk]
                    keys = slice(u * blk, (u + 2) * blk)

                    def one_head(k_h, v_h):
                        s = lax.dot_general(q_u, k_h[keys], (((1,), (1,)), ((), ())),
                                            preferred_element_type=F32) + bias
                        m = jnp.max(s, axis=-1, keepdims=True)
                        p = jnp.exp(s - m).astype(BF16)
                        return jnp.dot(p, v_h[keys], preferred_element_type=F32), m

                    ol_a, m_a = one_head(k_a, v_a)
                    ol_b, m_b = one_head(k_b, v_b)
                    rows = rows_of(q_start + u * stride * blk, blk, stride)
                    o_pat[pat, rows, :] = jnp.where(head_a, ol_a, ol_b)
                    m_pat[pat, rows, :] = jnp.where(head_a, m_a, m_b)
                    l_pat[pat, rows, :] = jnp.where(head_a, ol_b, ol_a)
            return carry

        lax.fori_loop(0, dil * n_blk // (n_run * n_res), body, 0)

    blocks_per_seg = len4 // blk

    def merge(i, carry):
        seg, c = i // blocks_per_seg, i % blocks_per_seg
        rows4 = pl.ds(pl.multiple_of(seg * len4 + c * blk, blk), blk)
        rows1 = pl.ds(seg + dint * blk * c, blk, stride=dint)
        rows = [rows1 if dil == 1 else rows4 for _, dil in DILATED_PATTERNS]
        ms = [m_pat[p, rw, :] for p, rw in enumerate(rows)]
        m_top = functools.reduce(jnp.maximum, ms)
        ws = [jnp.exp(m - m_top) for m in ms]
        num = sum(w * o_pat[p, rw, :] for p, (w, rw) in enumerate(zip(ws, rows)))
        den = sum(w * pltpu.roll(l_pat[p, rw, :], ATTN_HEAD_DIM, 1)
                  for p, (w, rw) in enumerate(zip(ws, rows)))
        out = num / den
        ssq = jnp.dot((out * out).astype(BF16), seg_ref[...], preferred_element_type=F32)
        out = out * lax.rsqrt(ssq * (1.0 / ATTN_HEAD_DIM) + NORM_EPS)
        qf[rows1, :] = out
        return carry

    assert DILATED_PATTERNS[0][1] == 1
    lax.fori_loop(0, seq // blk, merge, 0, unroll=True)
    o_ref[...] = ((qf[...] * nw_ref[...]) * gate_ref[...].astype(F32)).astype(BF16)


def _band_bias():
    qi = np.arange(ATTN_BLOCK)[:, None]
    kj = np.arange(2 * ATTN_BLOCK)[None, :]
    dist = ATTN_BLOCK + qi - kj
    band = (dist >= 0) & (dist <= ATTN_BLOCK)
    first = band & (kj >= ATTN_BLOCK)
    to_bias = lambda mk: np.where(mk, 0.0, -np.inf).astype(np.float32)
    return to_bias(band), to_bias(first)


def _attention(q, k, v, gate, norm_w):
    b, seq, width = q.shape
    bias, bias0 = _band_bias()
    n_pat = len(DILATED_PATTERNS)
    pad4 = ATTN_BLOCK * max(d for _, d in DILATED_PATTERNS) // ATTN_DEINTERLEAVE
    rows4 = seq + ATTN_DEINTERLEAVE * pad4
    head_of_lane = np.arange(LANES) // ATTN_HEAD_DIM
    seg = jnp.asarray((head_of_lane[:, None] == head_of_lane[None, :]).astype(np.float32), BF16)
    col_spec = pl.BlockSpec((None, seq, LANES), lambda i, j: (i, 0, j))
    bias_spec = pl.BlockSpec(bias.shape, lambda i, j: (0, 0))
    seg_spec = pl.BlockSpec(seg.shape, lambda i, j: (0, 0))
    nw_spec = pl.BlockSpec((1, LANES), lambda i, j: (0, j))
    return pl.pallas_call(
        functools.partial(_attn_kernel, seq=seq),
        out_shape=jax.ShapeDtypeStruct((b, seq, width), BF16),
        grid=(b, width // LANES),
        in_specs=[col_spec, col_spec, col_spec, col_spec, bias_spec, bias_spec, seg_spec, nw_spec],
        out_specs=col_spec,
        scratch_shapes=[pltpu.VMEM((seq, LANES), F32),
                        pltpu.VMEM((ATTN_BLOCK + seq, LANES), F32),
                        pltpu.VMEM((ATTN_BLOCK + seq, LANES), F32),
                        pltpu.VMEM((seq, LANES), F32),
                        pltpu.VMEM((rows4, LANES), F32),
                        pltpu.VMEM((rows4, LANES), F32),
                        pltpu.VMEM((n_pat, seq, LANES), F32),
                        pltpu.VMEM((n_pat, seq, LANES), F32),
                        pltpu.VMEM((n_pat, seq, LANES), F32)],
        compiler_params=pltpu.CompilerParams(
            dimension_semantics=("parallel", "parallel"), vmem_limit_bytes=VMEM_LIMIT),
        name="dilated_attn",
    )(q, k, v, gate, bias, bias0, seg, norm_w)


def _out_proj_kernel(x_ref, ya_ref, yh_ref, fnw_ref, w_ref, out_ref, *, attn_width):
    mixed = jnp.dot(ya_ref[...], w_ref[0:attn_width, :], preferred_element_type=F32)
    mixed = mixed + jnp.dot(yh_ref[...], w_ref[attn_width:, :], preferred_element_type=F32)
    x = x_ref[...] + mixed
    ms = jnp.mean(x * x, axis=-1, keepdims=True)
    out_ref[...] = (x * lax.rsqrt(ms + NORM_EPS)) * fnw_ref[...]


def _out_proj(x2, ya, yh, fnw, w_bf16):
    n, d_model = x2.shape
    attn_width = ya.shape[-1]
    rows = OUT_ROWS
    row_spec = lambda c: pl.BlockSpec((rows, c), lambda i: (i, 0))
    full_spec = lambda a: pl.BlockSpec(a.shape, lambda i: (0,) * a.ndim)
    return pl.pallas_call(
        functools.partial(_out_proj_kernel, attn_width=attn_width),
        out_shape=jax.ShapeDtypeStruct((n, d_model), F32),
        grid=(n // rows,),
        in_specs=[row_spec(d_model), row_spec(attn_width), row_spec(yh.shape[-1]),
                  full_spec(fnw), full_spec(w_bf16)],
        out_specs=row_spec(d_model),
        compiler_params=pltpu.CompilerParams(
            dimension_semantics=("parallel",), vmem_limit_bytes=VMEM_LIMIT),
        name="out_proj",
    )(x2, ya, yh, fnw, w_bf16)


def kernel(x, positions, w_in, w_out, mix_norm_w, attn_out_norm_w, hgrn_out_norm_w,
           hgrn_lb_raw, final_norm_w):
    b, seq, d_model = x.shape
    depth = w_in.shape[0]
    attn_width = attn_out_norm_w.shape[-1]
    hgrn_width = hgrn_out_norm_w.shape[-1]
    assert depth == 1 and attn_width == hgrn_width and w_in.shape[-1] == 8 * attn_width
    assert seq % (max(d for _, d in DILATED_PATTERNS) * ATTN_BLOCK) == 0
    n = b * seq
    layer = 0
    x2 = x.reshape(n, d_model)
    q, k, v, ag, yh = _in_proj(
        x2, positions, mix_norm_w[layer][None], w_in[layer].astype(BF16), hgrn_lb_raw,
        hgrn_out_norm_w[layer][None], attn_width, layer, seq)
    to3 = lambda t: t.reshape(b, seq, t.shape[-1])
    ya = _attention(to3(q), to3(k), to3(v), to3(ag), attn_out_norm_w[layer][None])
    out = _out_proj(x2, ya.reshape(n, attn_width), yh, final_norm_w[None],
                    w_out[layer].astype(BF16))
    return out.reshape(b, seq, d_model)
```

```python
import functools

import numpy as np
import jax
import jax.numpy as jnp
from jax import lax
from jax.experimental import pallas as pl
from jax.experimental.pallas import tpu as pltpu

F32 = jnp.float32
BF16 = jnp.bfloat16

LANES = 128
ATTN_HEAD_DIM = 64
HGRN_HEAD_DIM = 128
DILATED_PATTERNS = ((128, 1), (512, 4), (2048, 16))
ATTN_BLOCK = 128
ATTN_DEINTERLEAVE = 4
ROPE_THETA = 500000.0
ROPE_DIMS = ATTN_HEAD_DIM // 4
HGRN_CHUNK = 64
NORM_EPS = 1e-6
VMEM_LIMIT = 56 * 1024 * 1024

PROJ_ROWS = 512
PROJ_TILES = 2
OUT_ROWS = 1024
ATTN_RUN = 32
ATTN_CHAINS = 32


def _silu(t):
    return t * jax.nn.sigmoid(t)


def _split3(t):
    hi = t.astype(BF16)
    r1 = t - hi.astype(F32)
    mid = r1.astype(BF16)
    lo = (r1 - mid.astype(F32)).astype(BF16)
    return hi, mid, lo


def _hgrn_tile(hq, hf, hi, gate, lbraw_ref, tri_ref, norm_w_ref, state_ref, o_ref, layer, interleave):
    chunk = HGRN_CHUNK
    rows, width = hq.shape
    n_chunks = rows // chunk
    n_heads = width // HGRN_HEAD_DIM
    heads = [slice(h * HGRN_HEAD_DIM, (h + 1) * HGRN_HEAD_DIM) for h in range(n_heads)]
    chunks = [slice(c * chunk, (c + 1) * chunk) for c in range(n_chunks)]

    raw = lbraw_ref[...]
    e = jnp.exp(raw - jnp.max(raw, axis=0, keepdims=True))
    sm = e / jnp.sum(e, axis=0, keepdims=True)
    lb = jnp.sum(sm[0:layer + 1, :], axis=0, keepdims=True)

    tri = tri_ref[...]
    ti = lax.broadcasted_iota(jnp.int32, (chunk, chunk), 0)
    si = lax.broadcasted_iota(jnp.int32, (chunk, chunk), 1)
    causal = ti >= si

    hkey, cum = [], []
    for rws in chunks:
        f = lb + (1.0 - lb) * jax.nn.sigmoid(hf[rws, :])
        hkey.append(1.0 - f)
        cum.append(sum(jnp.dot(tri, part, preferred_element_type=F32)
                       for part in _split3(jnp.log(f))))
    interleave[0]()

    q_dec, decay, att, upd = [], [], [], []
    for c, rws in enumerate(chunks):
        dec_c = jnp.exp(cum[c][chunk - 1:chunk, :])
        q_c = (hq[rws, :] * jnp.exp(cum[c])).astype(BF16)
        k_inv = hkey[c] * jnp.exp(-cum[c])
        k_end = (k_inv * dec_c).astype(BF16)
        k_inv = k_inv.astype(BF16)
        att.append([lax.dot_general(q_c[:, cs], k_inv[:, cs], (((1,), (1,)), ((), ())),
                                    preferred_element_type=F32) for cs in heads])
        upd.append([lax.dot_general(hi[rws, cs], k_end[:, cs], (((0,), (0,)), ((), ())),
                                    preferred_element_type=F32) for cs in heads])
        q_dec.append(q_c)
        decay.append(dec_c)
    interleave[1]()

    o_intra = [[jnp.dot(jnp.where(causal, att[c][h], 0.0).astype(BF16), hi[rws, cs],
                        preferred_element_type=F32) for h, cs in enumerate(heads)]
               for c, rws in enumerate(chunks)]
    interleave[2]()

    for h, cs in enumerate(heads):
        if h == n_heads // 2:
            interleave[3]()
        st = state_ref[h]
        for c, rws in enumerate(chunks):
            o = o_intra[c][h] + lax.dot_general(
                q_dec[c][:, cs], st.astype(BF16), (((1,), (1,)), ((), ())),
                preferred_element_type=F32)
            ms = jnp.mean(o * o, axis=-1, keepdims=True)
            y = (o * lax.rsqrt(ms + NORM_EPS)) * norm_w_ref[:, cs] * gate[rws, cs]
            o_ref[rws, cs] = y.astype(o_ref.dtype)
            st = decay[c][:, cs] * st + upd[c][h]
        state_ref[h] = st


def _in_proj_kernel(x_ref, pos_ref, freq_ref, spread_ref, cos_base_ref, sgn_up_ref, sgn_dn_ref,
                    nw_ref, lbraw_ref, tri_ref, hnw_ref, w_ref,
                    q_ref, k_ref, v_ref, ag_ref, yh_ref, state_ref,
                    *, width, layer, steps_per_seq):
    @pl.when(pl.program_id(0) % steps_per_seq == 0)
    def _():
        state_ref[...] = jnp.zeros(state_ref.shape, F32)

    for t in range(PROJ_TILES):
        rows = pl.ds(t * PROJ_ROWS, PROJ_ROWS)
        _in_proj_tile(x_ref.at[rows], pos_ref[:, t * PROJ_ROWS:(t + 1) * PROJ_ROWS],
                      freq_ref, spread_ref, cos_base_ref, sgn_up_ref, sgn_dn_ref,
                      nw_ref, lbraw_ref, tri_ref, hnw_ref, w_ref,
                      q_ref.at[rows], k_ref.at[rows], v_ref.at[rows], ag_ref.at[rows],
                      yh_ref.at[rows], state_ref, width, layer)


def _in_proj_tile(x_ref, pos, freq_ref, spread_ref, cos_base_ref, sgn_up_ref, sgn_dn_ref,
                  nw_ref, lbraw_ref, tri_ref, hnw_ref, w_ref,
                  q_ref, k_ref, v_ref, ag_ref, yh_ref, state_ref, width, layer):
    x = x_ref[...]
    ms = jnp.mean(x * x, axis=-1, keepdims=True)
    hn = ((x * lax.rsqrt(ms + NORM_EPS)) * nw_ref[...]).astype(BF16)

    ang = freq_ref[...] * pos.astype(F32)
    parts = []
    for t in (jnp.cos(ang), jnp.sin(ang)):
        hi = t.astype(BF16).astype(F32)
        parts += [hi, t - hi]
    tabs = lax.dot_general(jnp.concatenate(parts, axis=0).astype(BF16), spread_ref[...],
                           (((0,), (0,)), ((), ())), preferred_element_type=F32)
    cos = tabs[:, :LANES] + cos_base_ref[...]
    sin = tabs[:, LANES:]
    sgn_up = sgn_up_ref[...]
    sgn_dn = sgn_dn_ref[...]
    half = ROPE_DIMS // 2

    def rope(t):
        cols = []
        for j in range(width // LANES):
            tj = t[:, j * LANES:(j + 1) * LANES]
            up = pltpu.roll(tj, LANES - half, 1)
            dn = pltpu.roll(tj, half, 1)
            cols.append(tj * cos + sin * (up * sgn_up + dn * sgn_dn))
        return jnp.concatenate(cols, axis=1)

    def proj(g):
        return jnp.dot(hn, w_ref[:, g * width:(g + 1) * width], preferred_element_type=F32)

    scale = ATTN_HEAD_DIM ** -0.5

    def emit_q():
        q_ref[...] = (rope(proj(0)) * scale).astype(BF16)

    def emit_k():
        k_ref[...] = rope(proj(1)).astype(BF16)

    def emit_v():
        v_ref[...] = proj(2).astype(BF16)

    def emit_ag():
        ag_ref[...] = _silu(proj(3)).astype(BF16)

    hf = proj(5)
    hq = _silu(proj(4))
    hi = proj(6).astype(BF16)
    gate = _silu(proj(7))
    _hgrn_tile(hq, hf, hi, gate, lbraw_ref, tri_ref, hnw_ref, state_ref, yh_ref, layer,
               (emit_q, emit_k, emit_v, emit_ag))


def _rope_tables():
    half = ROPE_DIMS // 2
    inv_freq = ROPE_THETA ** (-np.arange(half, dtype=np.float32) * (2.0 / ROPE_DIMS))
    d = np.arange(LANES) % ATTN_HEAD_DIM
    rotary = d < ROPE_DIMS
    lane_uses = (np.arange(half)[:, None] == (d % half)[None, :]) & rotary[None, :]
    spread = np.zeros((4 * half, 2 * LANES), np.float32)
    for t in range(4):
        table = t // 2
        spread[t * half:(t + 1) * half, table * LANES:(table + 1) * LANES] = lane_uses
    cos_base = np.where(rotary, 0.0, 1.0).astype(np.float32)
    sgn_up = np.where(d < half, -1.0, 0.0).astype(np.float32)
    sgn_dn = np.where((d >= half) & rotary, 1.0, 0.0).astype(np.float32)
    return (inv_freq.astype(np.float32)[:, None], jnp.asarray(spread, BF16), cos_base[None],
            sgn_up[None], sgn_dn[None])


def _in_proj(x2, positions, norm_w, w_bf16, lb_raw, hgrn_norm_w, width, layer, seq):
    n, d_model = x2.shape
    rows = PROJ_ROWS * PROJ_TILES
    pos3 = positions.reshape(n // rows, 1, rows)
    consts = _rope_tables()
    tri = jnp.asarray(np.tril(np.ones((HGRN_CHUNK, HGRN_CHUNK), np.float32)), BF16)
    row_spec = lambda c: pl.BlockSpec((rows, c), lambda i: (i, 0))
    full_spec = lambda a: pl.BlockSpec(a.shape, lambda i: (0,) * a.ndim)
    pos_spec = pl.BlockSpec((None, 1, rows), lambda i: (i, 0, 0))
    out_sds = jax.ShapeDtypeStruct((n, width), BF16)
    n_out = 5
    small = [*consts, norm_w, lb_raw, tri, hgrn_norm_w, w_bf16]
    return pl.pallas_call(
        functools.partial(_in_proj_kernel, width=width, layer=layer, steps_per_seq=seq // rows),
        out_shape=[out_sds] * n_out,
        grid=(n // rows,),
        in_specs=[row_spec(d_model), pos_spec] + [full_spec(a) for a in small],
        out_specs=[row_spec(width)] * n_out,
        scratch_shapes=[pltpu.VMEM((width // HGRN_HEAD_DIM, HGRN_HEAD_DIM, HGRN_HEAD_DIM), F32)],
        compiler_params=pltpu.CompilerParams(
            dimension_semantics=("arbitrary",), vmem_limit_bytes=VMEM_LIMIT),
        name="in_proj",
    )(x2, pos3, *small)


def _attn_kernel(q_ref, k_ref, v_ref, gate_ref, bias_ref, bias0_ref, seg_ref, nw_ref, o_ref,
                 qf, kf, vf, q4, k4, v4, o_pat, m_pat, l_pat, *, seq):
    blk = ATTN_BLOCK
    dint = ATTN_DEINTERLEAVE
    len4 = seq // dint
    pad4 = blk * max(d for _, d in DILATED_PATTERNS) // dint
    seg4 = pad4 + len4

    qf[...] = q_ref[...].astype(F32)
    for src, dst, dst4 in ((k_ref, kf, k4), (v_ref, vf, v4)):
        dst[0:blk, :] = jnp.zeros((blk, LANES), F32)
        dst[blk:blk + seq, :] = src[...].astype(F32)
        for r in range(dint):
            dst4[r * seg4:r * seg4 + pad4, :] = jnp.zeros((pad4, LANES), F32)
            dst4[r * seg4 + pad4:(r + 1) * seg4, :] = dst[pl.ds(blk + r, len4, stride=dint), :]
    for r in range(dint):
        q4[r * len4:(r + 1) * len4, :] = qf[pl.ds(r, len4, stride=dint), :]

    lane = lax.broadcasted_iota(jnp.int32, (1, LANES), 1)
    head_a = lane < ATTN_HEAD_DIM
    head_b = jnp.logical_not(head_a)

    def rows_of(start, size, stride):
        if stride == 1:
            return pl.ds(pl.multiple_of(start, blk), size)
        return pl.ds(start, size, stride=stride)

    for pat, (window, dil) in enumerate(DILATED_PATTERNS):
        assert window // dil == blk and (dil == 1 or dil % dint == 0)
        n_blk = seq // dil // blk
        n_run = min(ATTN_RUN, n_blk)
        n_res = min(ATTN_CHAINS // n_run, dil)
        runs_per_res = n_blk // n_run
        stride = 1 if dil == 1 else dil // dint
        q_src, k_src, v_src = (qf, kf, vf) if dil == 1 else (q4, k4, v4)

        def body(i, carry, pat=pat, dil=dil, n_run=n_run, n_res=n_res,
                 runs_per_res=runs_per_res, stride=stride, q_src=q_src, k_src=k_src, v_src=v_src):
            for j in range(n_res):
                r = (i // runs_per_res) * n_res + j
                n0 = (i % runs_per_res) * n_run
                if dil == 1:
                    q_start = blk * n0
                    k_start = blk + blk * (n0 - 1)
                else:
                    seg, off = r % dint, r // dint
                    q_start = seg * len4 + off + stride * blk * n0
                    k_start = seg * seg4 + pad4 + off + stride * blk * (n0 - 1)
                k_rows = rows_of(k_start, (n_run + 1) * blk, stride)
                kb = k_src[k_rows, :]
                k_a = jnp.where(head_a, kb, 0.0).astype(BF16)
                k_b = jnp.where(head_b, kb, 0.0).astype(BF16)
                vb = v_src[k_rows, :]
                v_a = jnp.where(head_a, vb, 1.0).astype(BF16)
                v_b = jnp.where(head_b, vb, 1.0).astype(BF16)
                qb = q_src[rows_of(q_start, n_run * blk, stride), :].astype(BF16)
                for u in range(n_run):
                    bias = bias_ref[...]
                    if u == 0:
                        bias = jnp.where(n0 == 0, bias0_ref[...], bias)
                    q_u = qb[u * blk:(u + 1) * blk]
                    keys = slice(u * blk, (u + 2) * blk)

                    def one_head(k_h, v_h):
                        s = lax.dot_general(q_u, k_h[keys], (((1,), (1,)), ((), ())),
                                            preferred_element_type=F32) + bias
                        m = jnp.max(s, axis=-1, keepdims=True)
                        p = jnp.exp(s - m).astype(BF16)
                        return jnp.dot(p, v_h[keys], preferred_element_type=F32), m

                    ol_a, m_a = one_head(k_a, v_a)
                    ol_b, m_b = one_head(k_b, v_b)
                    rows = rows_of(q_start + u * stride * blk, blk, stride)
                    o_pat[pat, rows, :] = jnp.where(head_a, ol_a, ol_b)
                    m_pat[pat, rows, :] = jnp.where(head_a, m_a, m_b)
                    l_pat[pat, rows, :] = jnp.where(head_a, ol_b, ol_a)
            return carry

        lax.fori_loop(0, dil * n_blk // (n_run * n_res), body, 0)

    blocks_per_seg = len4 // blk

    def merge(i, carry):
        seg, c = i // blocks_per_seg, i % blocks_per_seg
        rows4 = pl.ds(pl.multiple_of(seg * len4 + c * blk, blk), blk)
        rows1 = pl.ds(seg + dint * blk * c, blk, stride=dint)
        rows = [rows1 if dil == 1 else rows4 for _, dil in DILATED_PATTERNS]
        ms = [m_pat[p, rw, :] for p, rw in enumerate(rows)]
        m_top = functools.reduce(jnp.maximum, ms)
        ws = [jnp.exp(m - m_top) for m in ms]
        num = sum(w * o_pat[p, rw, :] for p, (w, rw) in enumerate(zip(ws, rows)))
        den = sum(w * pltpu.roll(l_pat[p, rw, :], ATTN_HEAD_DIM, 1)
                  for p, (w, rw) in enumerate(zip(ws, rows)))
        out = num / den
        ssq = jnp.dot((out * out).astype(BF16), seg_ref[...], preferred_element_type=F32)
        out = out * lax.rsqrt(ssq * (1.0 / ATTN_HEAD_DIM) + NORM_EPS)
        qf[rows1, :] = out
        return carry

    assert DILATED_PATTERNS[0][1] == 1
    lax.fori_loop(0, seq // blk, merge, 0, unroll=4)
    o_ref[...] = ((qf[...] * nw_ref[...]) * gate_ref[...].astype(F32)).astype(BF16)


def _band_bias():
    qi = np.arange(ATTN_BLOCK)[:, None]
    kj = np.arange(2 * ATTN_BLOCK)[None, :]
    dist = ATTN_BLOCK + qi - kj
    band = (dist >= 0) & (dist <= ATTN_BLOCK)
    first = band & (kj >= ATTN_BLOCK)
    to_bias = lambda mk: np.where(mk, 0.0, -np.inf).astype(np.float32)
    return to_bias(band), to_bias(first)


def _attention(q, k, v, gate, norm_w):
    b, seq, width = q.shape
    bias, bias0 = _band_bias()
    n_pat = len(DILATED_PATTERNS)
    pad4 = ATTN_BLOCK * max(d for _, d in DILATED_PATTERNS) // ATTN_DEINTERLEAVE
    rows4 = seq + ATTN_DEINTERLEAVE * pad4
    head_of_lane = np.arange(LANES) // ATTN_HEAD_DIM
    seg = jnp.asarray((head_of_lane[:, None] == head_of_lane[None, :]).astype(np.float32), BF16)
    col_spec = pl.BlockSpec((None, seq, LANES), lambda i, j: (i, 0, j))
    bias_spec = pl.BlockSpec(bias.shape, lambda i, j: (0, 0))
    seg_spec = pl.BlockSpec(seg.shape, lambda i, j: (0, 0))
    nw_spec = pl.BlockSpec((1, LANES), lambda i, j: (0, j))
    return pl.pallas_call(
        functools.partial(_attn_kernel, seq=seq),
        out_shape=jax.ShapeDtypeStruct((b, seq, width), BF16),
        grid=(b, width // LANES),
        in_specs=[col_spec, col_spec, col_spec, col_spec, bias_spec, bias_spec, seg_spec, nw_spec],
        out_specs=col_spec,
        scratch_shapes=[pltpu.VMEM((seq, LANES), F32),
                        pltpu.VMEM((ATTN_BLOCK + seq, LANES), F32),
                        pltpu.VMEM((ATTN_BLOCK + seq, LANES), F32),
                        pltpu.VMEM((seq, LANES), F32),
                        pltpu.VMEM((rows4, LANES), F32),
                        pltpu.VMEM((rows4, LANES), F32),
                        pltpu.VMEM((n_pat, seq, LANES), F32),
                        pltpu.VMEM((n_pat, seq, LANES), F32),
                        pltpu.VMEM((n_pat, seq, LANES), F32)],
        compiler_params=pltpu.CompilerParams(
            dimension_semantics=("parallel", "parallel"), vmem_limit_bytes=VMEM_LIMIT),
        name="dilated_attn",
    )(q, k, v, gate, bias, bias0, seg, norm_w)


def _out_proj_kernel(x_ref, ya_ref, yh_ref, fnw_ref, w_ref, out_ref, *, attn_width):
    mixed = jnp.dot(ya_ref[...], w_ref[0:attn_width, :], preferred_element_type=F32)
    mixed = mixed + jnp.dot(yh_ref[...], w_ref[attn_width:, :], preferred_element_type=F32)
    x = x_ref[...] + mixed
    ms = jnp.mean(x * x, axis=-1, keepdims=True)
    out_ref[...] = (x * lax.rsqrt(ms + NORM_EPS)) * fnw_ref[...]


def _out_proj(x2, ya, yh, fnw, w_bf16):
    n, d_model = x2.shape
    attn_width = ya.shape[-1]
    rows = OUT_ROWS
    row_spec = lambda c: pl.BlockSpec((rows, c), lambda i: (i, 0))
    full_spec = lambda a: pl.BlockSpec(a.shape, lambda i: (0,) * a.ndim)
    return pl.pallas_call(
        functools.partial(_out_proj_kernel, attn_width=attn_width),
        out_shape=jax.ShapeDtypeStruct((n, d_model), F32),
        grid=(n // rows,),
        in_specs=[row_spec(d_model), row_spec(attn_width), row_spec(yh.shape[-1]),
                  full_spec(fnw), full_spec(w_bf16)],
        out_specs=row_spec(d_model),
        compiler_params=pltpu.CompilerParams(
            dimension_semantics=("parallel",), vmem_limit_bytes=VMEM_LIMIT),
        name="out_proj",
    )(x2, ya, yh, fnw, w_bf16)


def kernel(x, positions, w_in, w_out, mix_norm_w, attn_out_norm_w, hgrn_out_norm_w,
           hgrn_lb_raw, final_norm_w):
    b, seq, d_model = x.shape
    depth = w_in.shape[0]
    attn_width = attn_out_norm_w.shape[-1]
    hgrn_width = hgrn_out_norm_w.shape[-1]
    assert depth == 1 and attn_width == hgrn_width and w_in.shape[-1] == 8 * attn_width
    assert seq % (max(d for _, d in DILATED_PATTERNS) * ATTN_BLOCK) == 0
    n = b * seq
    layer = 0
    x2 = x.reshape(n, d_model)
    q, k, v, ag, yh = _in_proj(
        x2, positions, mix_norm_w[layer][None], w_in[layer].astype(BF16), hgrn_lb_raw,
        hgrn_out_norm_w[layer][None], attn_width, layer, seq)
    to3 = lambda t: t.reshape(b, seq, t.shape[-1])
    ya = _attention(to3(q), to3(k), to3(v), to3(ag), attn_out_norm_w[layer][None])
    out = _out_proj(x2, ya.reshape(n, attn_width), yh, final_norm_w[None],
                    w_out[layer].astype(BF16))
    return out.reshape(b, seq, d_model)
```

```python
import functools

import numpy as np
import jax
import jax.numpy as jnp
from jax import lax
from jax.experimental import pallas as pl
from jax.experimental.pallas import tpu as pltpu

F32 = jnp.float32
BF16 = jnp.bfloat16

LANES = 128
SUBLANES = 8
ATTN_HEAD_DIM = 64
HGRN_HEAD_DIM = 128
DILATED_PATTERNS = ((128, 1), (512, 4), (2048, 16))
ATTN_BLOCK = 128
ATTN_DEINTERLEAVE = 4
ROPE_THETA = 500000.0
ROPE_DIMS = ATTN_HEAD_DIM // 4
HGRN_CHUNK = 64
NORM_EPS = 1e-6
LOG2_E = 1.4426950408889634
VMEM_LIMIT = 56 * 1024 * 1024

PROJ_ROWS = 512
PROJ_TILES = 2
OUT_ROWS = 1024
ATTN_RUN = 32
ATTN_CHAINS = 32


def _silu(t):
    return t * jax.nn.sigmoid(t)


def _cumsum_rows(t):
    n_rows, width = t.shape
    row = lax.broadcasted_iota(jnp.int32, (n_rows, width), 0)
    shift = 1
    while shift < n_rows:
        if shift < SUBLANES:
            prev = jnp.where(row >= shift, pltpu.roll(t, shift, 0), 0.0)
        else:
            prev = jnp.concatenate([jnp.zeros((shift, width), t.dtype), t[:n_rows - shift]], axis=0)
        t = t + prev
        shift *= 2
    return t


def _hgrn_tile(hq, hf, hi, gate, lbraw_ref, norm_w_ref, state_ref, o_ref, layer, interleave):
    chunk = HGRN_CHUNK
    rows, width = hq.shape
    n_chunks = rows // chunk
    n_heads = width // HGRN_HEAD_DIM
    heads = [slice(h * HGRN_HEAD_DIM, (h + 1) * HGRN_HEAD_DIM) for h in range(n_heads)]
    chunks = [slice(c * chunk, (c + 1) * chunk) for c in range(n_chunks)]

    raw = lbraw_ref[...]
    e = jnp.exp(raw - jnp.max(raw, axis=0, keepdims=True))
    sm = e / jnp.sum(e, axis=0, keepdims=True)
    lb = jnp.sum(sm[0:layer + 1, :], axis=0, keepdims=True)

    ti = lax.broadcasted_iota(jnp.int32, (chunk, chunk), 0)
    si = lax.broadcasted_iota(jnp.int32, (chunk, chunk), 1)
    causal = ti >= si

    hkey, cum = [], []
    for rws in chunks:
        f = lb + (1.0 - lb) * jax.nn.sigmoid(hf[rws, :])
        hkey.append(1.0 - f)
        cum.append(_cumsum_rows(jnp.log(f)))
    interleave[0]()

    q_dec, decay, att, upd = [], [], [], []
    for c, rws in enumerate(chunks):
        dec_c = jnp.exp(cum[c][chunk - 1:chunk, :])
        q_c = (hq[rws, :] * jnp.exp(cum[c])).astype(BF16)
        k_inv = hkey[c] * jnp.exp(-cum[c])
        k_end = (k_inv * dec_c).astype(BF16)
        k_inv = k_inv.astype(BF16)
        att.append([lax.dot_general(q_c[:, cs], k_inv[:, cs], (((1,), (1,)), ((), ())),
                                    preferred_element_type=F32) for cs in heads])
        upd.append([lax.dot_general(hi[rws, cs], k_end[:, cs], (((0,), (0,)), ((), ())),
                                    preferred_element_type=F32) for cs in heads])
        q_dec.append(q_c)
        decay.append(dec_c)
    interleave[1]()

    o_intra = [[jnp.dot(jnp.where(causal, att[c][h], 0.0).astype(BF16), hi[rws, cs],
                        preferred_element_type=F32) for h, cs in enumerate(heads)]
               for c, rws in enumerate(chunks)]
    interleave[2]()

    for h, cs in enumerate(heads):
        if h == n_heads // 2:
            interleave[3]()
        st = state_ref[h]
        for c, rws in enumerate(chunks):
            o = o_intra[c][h] + lax.dot_general(
                q_dec[c][:, cs], st.astype(BF16), (((1,), (1,)), ((), ())),
                preferred_element_type=F32)
            ms = jnp.mean(o * o, axis=-1, keepdims=True)
            y = (o * lax.rsqrt(ms + NORM_EPS)) * norm_w_ref[:, cs] * gate[rws, cs]
            o_ref[rws, cs] = y.astype(o_ref.dtype)
            st = decay[c][:, cs] * st + upd[c][h]
        state_ref[h] = st


def _in_proj_kernel(x_ref, pos_ref, freq_ref, spread_ref, cos_base_ref, sgn_up_ref, sgn_dn_ref,
                    nw_ref, lbraw_ref, hnw_ref, w_ref,
                    q_ref, k_ref, v_ref, ag_ref, yh_ref, state_ref,
                    *, width, layer, steps_per_seq):
    @pl.when(pl.program_id(0) % steps_per_seq == 0)
    def _():
        state_ref[...] = jnp.zeros(state_ref.shape, F32)

    for t in range(PROJ_TILES):
        rows = pl.ds(t * PROJ_ROWS, PROJ_ROWS)
        _in_proj_tile(x_ref.at[rows], pos_ref[:, t * PROJ_ROWS:(t + 1) * PROJ_ROWS],
                      freq_ref, spread_ref, cos_base_ref, sgn_up_ref, sgn_dn_ref,
                      nw_ref, lbraw_ref, hnw_ref, w_ref,
                      q_ref.at[rows], k_ref.at[rows], v_ref.at[rows], ag_ref.at[rows],
                      yh_ref.at[rows], state_ref, width, layer)


def _in_proj_tile(x_ref, pos, freq_ref, spread_ref, cos_base_ref, sgn_up_ref, sgn_dn_ref,
                  nw_ref, lbraw_ref, hnw_ref, w_ref,
                  q_ref, k_ref, v_ref, ag_ref, yh_ref, state_ref, width, layer):
    x = x_ref[...]
    ms = jnp.mean(x * x, axis=-1, keepdims=True)
    hn = ((x * lax.rsqrt(ms + NORM_EPS)) * nw_ref[...]).astype(BF16)

    ang = freq_ref[...] * pos.astype(F32)
    parts = []
    for t in (jnp.cos(ang), jnp.sin(ang)):
        hi = t.astype(BF16).astype(F32)
        parts += [hi, t - hi]
    tabs = lax.dot_general(jnp.concatenate(parts, axis=0).astype(BF16), spread_ref[...],
                           (((0,), (0,)), ((), ())), preferred_element_type=F32)
    cos = tabs[:, :LANES] + cos_base_ref[...]
    sin = tabs[:, LANES:]
    sgn_up = sgn_up_ref[...]
    sgn_dn = sgn_dn_ref[...]
    half = ROPE_DIMS // 2

    def rope(t):
        cols = []
        for j in range(width // LANES):
            tj = t[:, j * LANES:(j + 1) * LANES]
            up = pltpu.roll(tj, LANES - half, 1)
            dn = pltpu.roll(tj, half, 1)
            cols.append(tj * cos + sin * (up * sgn_up + dn * sgn_dn))
        return jnp.concatenate(cols, axis=1)

    def proj(g):
        return jnp.dot(hn, w_ref[:, g * width:(g + 1) * width], preferred_element_type=F32)

    scale = ATTN_HEAD_DIM ** -0.5 * LOG2_E

    def emit_q():
        q_ref[...] = (rope(proj(0)) * scale).astype(BF16)

    def emit_k():
        k_ref[...] = rope(proj(1)).astype(BF16)

    def emit_v():
        v_ref[...] = proj(2).astype(BF16)

    def emit_ag():
        ag_ref[...] = _silu(proj(3)).astype(BF16)

    hf = proj(5)
    hq = _silu(proj(4))
    hi = proj(6).astype(BF16)
    gate = _silu(proj(7))
    _hgrn_tile(hq, hf, hi, gate, lbraw_ref, hnw_ref, state_ref, yh_ref, layer,
               (emit_q, emit_k, emit_v, emit_ag))


def _rope_tables():
    half = ROPE_DIMS // 2
    inv_freq = ROPE_THETA ** (-np.arange(half, dtype=np.float32) * (2.0 / ROPE_DIMS))
    d = np.arange(LANES) % ATTN_HEAD_DIM
    rotary = d < ROPE_DIMS
    lane_uses = (np.arange(half)[:, None] == (d % half)[None, :]) & rotary[None, :]
    spread = np.zeros((4 * half, 2 * LANES), np.float32)
    for t in range(4):
        table = t // 2
        spread[t * half:(t + 1) * half, table * LANES:(table + 1) * LANES] = lane_uses
    cos_base = np.where(rotary, 0.0, 1.0).astype(np.float32)
    sgn_up = np.where(d < half, -1.0, 0.0).astype(np.float32)
    sgn_dn = np.where((d >= half) & rotary, 1.0, 0.0).astype(np.float32)
    return (inv_freq.astype(np.float32)[:, None], jnp.asarray(spread, BF16), cos_base[None],
            sgn_up[None], sgn_dn[None])


def _in_proj(x2, positions, norm_w, w_bf16, lb_raw, hgrn_norm_w, width, layer, seq):
    n, d_model = x2.shape
    rows = PROJ_ROWS * PROJ_TILES
    pos3 = positions.reshape(n // rows, 1, rows)
    consts = _rope_tables()
    row_spec = lambda c: pl.BlockSpec((rows, c), lambda i: (i, 0))
    full_spec = lambda a: pl.BlockSpec(a.shape, lambda i: (0,) * a.ndim)
    pos_spec = pl.BlockSpec((None, 1, rows), lambda i: (i, 0, 0))
    out_sds = jax.ShapeDtypeStruct((n, width), BF16)
    n_out = 5
    small = [*consts, norm_w, lb_raw, hgrn_norm_w, w_bf16]
    return pl.pallas_call(
        functools.partial(_in_proj_kernel, width=width, layer=layer, steps_per_seq=seq // rows),
        out_shape=[out_sds] * n_out,
        grid=(n // rows,),
        in_specs=[row_spec(d_model), pos_spec] + [full_spec(a) for a in small],
        out_specs=[row_spec(width)] * n_out,
        scratch_shapes=[pltpu.VMEM((width // HGRN_HEAD_DIM, HGRN_HEAD_DIM, HGRN_HEAD_DIM), F32)],
        compiler_params=pltpu.CompilerParams(
            dimension_semantics=("arbitrary",), vmem_limit_bytes=VMEM_LIMIT),
        name="in_proj",
    )(x2, pos3, *small)


def _attn_kernel(q_ref, k_ref, v_ref, gate_ref, bias_ref, bias0_ref, seg_ref, nw_ref, o_ref,
                 qf, kf, vf, q4, k4, v4, o_pat, m_pat, l_pat, *, seq):
    blk = ATTN_BLOCK
    dint = ATTN_DEINTERLEAVE
    len4 = seq // dint
    pad4 = blk * max(d for _, d in DILATED_PATTERNS) // dint
    seg4 = pad4 + len4

    qf[...] = q_ref[...].astype(F32)
    for src, dst, dst4 in ((k_ref, kf, k4), (v_ref, vf, v4)):
        dst[0:blk, :] = jnp.zeros((blk, LANES), F32)
        dst[blk:blk + seq, :] = src[...].astype(F32)
        for r in range(dint):
            dst4[r * seg4:r * seg4 + pad4, :] = jnp.zeros((pad4, LANES), F32)
            dst4[r * seg4 + pad4:(r + 1) * seg4, :] = dst[pl.ds(blk + r, len4, stride=dint), :]
    for r in range(dint):
        q4[r * len4:(r + 1) * len4, :] = qf[pl.ds(r, len4, stride=dint), :]

    lane = lax.broadcasted_iota(jnp.int32, (1, LANES), 1)
    head_a = lane < ATTN_HEAD_DIM
    head_b = jnp.logical_not(head_a)

    def rows_of(start, size, stride):
        if stride == 1:
            return pl.ds(pl.multiple_of(start, blk), size)
        return pl.ds(start, size, stride=stride)

    for pat, (window, dil) in enumerate(DILATED_PATTERNS):
        assert window // dil == blk and (dil == 1 or dil % dint == 0)
        n_blk = seq // dil // blk
        n_run = min(ATTN_RUN, n_blk)
        n_res = min(ATTN_CHAINS // n_run, dil)
        runs_per_res = n_blk // n_run
        stride = 1 if dil == 1 else dil // dint
        q_src, k_src, v_src = (qf, kf, vf) if dil == 1 else (q4, k4, v4)

        def body(i, carry, pat=pat, dil=dil, n_run=n_run, n_res=n_res,
                 runs_per_res=runs_per_res, stride=stride, q_src=q_src, k_src=k_src, v_src=v_src):
            for j in range(n_res):
                r = (i // runs_per_res) * n_res + j
                n0 = (i % runs_per_res) * n_run
                if dil == 1:
                    q_start = blk * n0
                    k_start = blk + blk * (n0 - 1)
                else:
                    seg, off = r % dint, r // dint
                    q_start = seg * len4 + off + stride * blk * n0
                    k_start = seg * seg4 + pad4 + off + stride * blk * (n0 - 1)
                k_rows = rows_of(k_start, (n_run + 1) * blk, stride)
                kb = k_src[k_rows, :]
                k_a = jnp.where(head_a, kb, 0.0).astype(BF16)
                k_b = jnp.where(head_b, kb, 0.0).astype(BF16)
                vb = v_src[k_rows, :]
                v_a = jnp.where(head_a, vb, 1.0).astype(BF16)
                v_b = jnp.where(head_b, vb, 1.0).astype(BF16)
                qb = q_src[rows_of(q_start, n_run * blk, stride), :].astype(BF16)
                for u in range(n_run):
                    bias = bias_ref[...]
                    if u == 0:
                        bias = jnp.where(n0 == 0, bias0_ref[...], bias)
                    q_u = qb[u * blk:(u + 1) * blk]
                    keys = slice(u * blk, (u + 2) * blk)

                    def one_head(k_h, v_h):
                        s = lax.dot_general(q_u, k_h[keys], (((1,), (1,)), ((), ())),
                                            preferred_element_type=F32) + bias
                        m = jnp.max(s, axis=-1, keepdims=True)
                        p = jnp.exp2(s - m).astype(BF16)
                        return jnp.dot(p, v_h[keys], preferred_element_type=F32), m

                    ol_a, m_a = one_head(k_a, v_a)
                    ol_b, m_b = one_head(k_b, v_b)
                    rows = rows_of(q_start + u * stride * blk, blk, stride)
                    o_pat[pat, rows, :] = jnp.where(head_a, ol_a, ol_b)
                    m_pat[pat, rows, :] = jnp.where(head_a, m_a, m_b)
                    l_pat[pat, rows, :] = jnp.where(head_a, ol_b, ol_a)
            return carry

        lax.fori_loop(0, dil * n_blk // (n_run * n_res), body, 0)

    blocks_per_seg = len4 // blk

    def merge(i, carry):
        seg, c = i // blocks_per_seg, i % blocks_per_seg
        rows4 = pl.ds(pl.multiple_of(seg * len4 + c * blk, blk), blk)
        rows1 = pl.ds(seg + dint * blk * c, blk, stride=dint)
        rows = [rows1 if dil == 1 else rows4 for _, dil in DILATED_PATTERNS]
        ms = [m_pat[p, rw, :] for p, rw in enumerate(rows)]
        m_top = functools.reduce(jnp.maximum, ms)
        ws = [jnp.exp2(m - m_top) for m in ms]
        num = sum(w * o_pat[p, rw, :] for p, (w, rw) in enumerate(zip(ws, rows)))
        den = sum(w * pltpu.roll(l_pat[p, rw, :], ATTN_HEAD_DIM, 1)
                  for p, (w, rw) in enumerate(zip(ws, rows)))
        out = num / den
        ssq = jnp.dot((out * out).astype(BF16), seg_ref[...], preferred_element_type=F32)
        out = out * lax.rsqrt(ssq * (1.0 / ATTN_HEAD_DIM) + NORM_EPS)
        qf[rows1, :] = out
        return carry

    assert DILATED_PATTERNS[0][1] == 1
    lax.fori_loop(0, seq // blk, merge, 0, unroll=4)
    o_ref[...] = ((qf[...] * nw_ref[...]) * gate_ref[...].astype(F32)).astype(BF16)


def _band_bias():
    qi = np.arange(ATTN_BLOCK)[:, None]
    kj = np.arange(2 * ATTN_BLOCK)[None, :]
    dist = ATTN_BLOCK + qi - kj
    band = (dist >= 0) & (dist <= ATTN_BLOCK)
    first = band & (kj >= ATTN_BLOCK)
    to_bias = lambda mk: np.where(mk, 0.0, -np.inf).astype(np.float32)
    return to_bias(band), to_bias(first)


def _attention(q, k, v, gate, norm_w):
    b, seq, width = q.shape
    bias, bias0 = _band_bias()
    n_pat = len(DILATED_PATTERNS)
    pad4 = ATTN_BLOCK * max(d for _, d in DILATED_PATTERNS) // ATTN_DEINTERLEAVE
    rows4 = seq + ATTN_DEINTERLEAVE * pad4
    head_of_lane = np.arange(LANES) // ATTN_HEAD_DIM
    seg = jnp.asarray((head_of_lane[:, None] == head_of_lane[None, :]).astype(np.float32), BF16)
    col_spec = pl.BlockSpec((None, seq, LANES), lambda i, j: (i, 0, j))
    bias_spec = pl.BlockSpec(bias.shape, lambda i, j: (0, 0))
    seg_spec = pl.BlockSpec(seg.shape, lambda i, j: (0, 0))
    nw_spec = pl.BlockSpec((1, LANES), lambda i, j: (0, j))
    return pl.pallas_call(
        functools.partial(_attn_kernel, seq=seq),
        out_shape=jax.ShapeDtypeStruct((b, seq, width), BF16),
        grid=(b, width // LANES),
        in_specs=[col_spec, col_spec, col_spec, col_spec, bias_spec, bias_spec, seg_spec, nw_spec],
        out_specs=col_spec,
        scratch_shapes=[pltpu.VMEM((seq, LANES), F32),
                        pltpu.VMEM((ATTN_BLOCK + seq, LANES), F32),
                        pltpu.VMEM((ATTN_BLOCK + seq, LANES), F32),
                        pltpu.VMEM((seq, LANES), F32),
                        pltpu.VMEM((rows4, LANES), F32),
                        pltpu.VMEM((rows4, LANES), F32),
                        pltpu.VMEM((n_pat, seq, LANES), F32),
                        pltpu.VMEM((n_pat, seq, LANES), F32),
                        pltpu.VMEM((n_pat, seq, LANES), F32)],
        compiler_params=pltpu.CompilerParams(
            dimension_semantics=("parallel", "parallel"), vmem_limit_bytes=VMEM_LIMIT),
        name="dilated_attn",
    )(q, k, v, gate, bias, bias0, seg, norm_w)


def _out_proj_kernel(x_ref, ya_ref, yh_ref, fnw_ref, w_ref, out_ref, *, attn_width):
    mixed = jnp.dot(ya_ref[...], w_ref[0:attn_width, :], preferred_element_type=F32)
    mixed = mixed + jnp.dot(yh_ref[...], w_ref[attn_width:, :], preferred_element_type=F32)
    x = x_ref[...] + mixed
    ms = jnp.mean(x * x, axis=-1, keepdims=True)
    out_ref[...] = (x * lax.rsqrt(ms + NORM_EPS)) * fnw_ref[...]


def _out_proj(x2, ya, yh, fnw, w_bf16):
    n, d_model = x2.shape
    attn_width = ya.shape[-1]
    rows = OUT_ROWS
    row_spec = lambda c: pl.BlockSpec((rows, c), lambda i: (i, 0))
    full_spec = lambda a: pl.BlockSpec(a.shape, lambda i: (0,) * a.ndim)
    return pl.pallas_call(
        functools.partial(_out_proj_kernel, attn_width=attn_width),
        out_shape=jax.ShapeDtypeStruct((n, d_model), F32),
        grid=(n // rows,),
        in_specs=[row_spec(d_model), row_spec(attn_width), row_spec(yh.shape[-1]),
                  full_spec(fnw), full_spec(w_bf16)],
        out_specs=row_spec(d_model),
        compiler_params=pltpu.CompilerParams(
            dimension_semantics=("parallel",), vmem_limit_bytes=VMEM_LIMIT),
        name="out_proj",
    )(x2, ya, yh, fnw, w_bf16)


def kernel(x, positions, w_in, w_out, mix_norm_w, attn_out_norm_w, hgrn_out_norm_w,
           hgrn_lb_raw, final_norm_w):
    b, seq, d_model = x.shape
    depth = w_in.shape[0]
    attn_width = attn_out_norm_w.shape[-1]
    hgrn_width = hgrn_out_norm_w.shape[-1]
    assert depth == 1 and attn_width == hgrn_width and w_in.shape[-1] == 8 * attn_width
    assert seq % (max(d for _, d in DILATED_PATTERNS) * ATTN_BLOCK) == 0
    n = b * seq
    layer = 0
    x2 = x.reshape(n, d_model)
    q, k, v, ag, yh = _in_proj(
        x2, positions, mix_norm_w[layer][None], w_in[layer].astype(BF16), hgrn_lb_raw,
        hgrn_out_norm_w[layer][None], attn_width, layer, seq)
    to3 = lambda t: t.reshape(b, seq, t.shape[-1])
    ya = _attention(to3(q), to3(k), to3(v), to3(ag), attn_out_norm_w[layer][None])
    out = _out_proj(x2, ya.reshape(n, attn_width), yh, final_norm_w[None],
                    w_out[layer].astype(BF16))
    return out.reshape(b, seq, d_model)
```

```python
import functools

import numpy as np
import jax
import jax.numpy as jnp
from jax import lax
from jax.experimental import pallas as pl
from jax.experimental.pallas import tpu as pltpu

F32 = jnp.float32
BF16 = jnp.bfloat16

LANES = 128
SUBLANES = 8
ATTN_HEAD_DIM = 64
HGRN_HEAD_DIM = 128
DILATED_PATTERNS = ((128, 1), (512, 4), (2048, 16))
ATTN_BLOCK = 128
ATTN_DEINTERLEAVE = 4
ROPE_THETA = 500000.0
ROPE_DIMS = ATTN_HEAD_DIM // 4
HGRN_CHUNK = 64
NORM_EPS = 1e-6
LOG2_E = 1.4426950408889634
VMEM_LIMIT = 56 * 1024 * 1024

PROJ_ROWS = 512
PROJ_TILES = 2
OUT_ROWS = 1024
ATTN_RUN = 32
ATTN_CHAINS = 32


def _silu(t):
    return t * jax.nn.sigmoid(t)


def _cumsum_rows(t):
    n_rows, width = t.shape
    row = lax.broadcasted_iota(jnp.int32, (n_rows, width), 0)
    shift = 1
    while shift < n_rows:
        if shift < SUBLANES:
            prev = jnp.where(row >= shift, pltpu.roll(t, shift, 0), 0.0)
        else:
            prev = jnp.concatenate([jnp.zeros((shift, width), t.dtype), t[:n_rows - shift]], axis=0)
        t = t + prev
        shift *= 2
    return t


def _hgrn_tile(hq, hf, hi, gate, lbraw_ref, norm_w_ref, state_ref, o_ref, layer, interleave):
    chunk = HGRN_CHUNK
    rows, width = hq.shape
    n_chunks = rows // chunk
    n_heads = width // HGRN_HEAD_DIM
    heads = [slice(h * HGRN_HEAD_DIM, (h + 1) * HGRN_HEAD_DIM) for h in range(n_heads)]
    chunks = [slice(c * chunk, (c + 1) * chunk) for c in range(n_chunks)]

    raw = lbraw_ref[...]
    e = jnp.exp(raw - jnp.max(raw, axis=0, keepdims=True))
    sm = e / jnp.sum(e, axis=0, keepdims=True)
    lb = jnp.sum(sm[0:layer + 1, :], axis=0, keepdims=True)

    ti = lax.broadcasted_iota(jnp.int32, (chunk, chunk), 0)
    si = lax.broadcasted_iota(jnp.int32, (chunk, chunk), 1)
    causal = ti >= si

    hkey, cum = [], []
    for rws in chunks:
        f = lb + (1.0 - lb) * jax.nn.sigmoid(hf[rws, :])
        hkey.append(1.0 - f)
        cum.append(_cumsum_rows(jnp.log(f)))
    interleave[0]()

    q_dec, decay, att, upd = [], [], [], []
    for c, rws in enumerate(chunks):
        dec_c = jnp.exp(cum[c][chunk - 1:chunk, :])
        q_c = (hq[rws, :] * jnp.exp(cum[c])).astype(BF16)
        k_inv = hkey[c] * jnp.exp(-cum[c])
        k_end = (k_inv * dec_c).astype(BF16)
        k_inv = k_inv.astype(BF16)
        att.append([lax.dot_general(q_c[:, cs], k_inv[:, cs], (((1,), (1,)), ((), ())),
                                    preferred_element_type=F32) for cs in heads])
        upd.append([lax.dot_general(hi[rws, cs], k_end[:, cs], (((0,), (0,)), ((), ())),
                                    preferred_element_type=F32) for cs in heads])
        q_dec.append(q_c)
        decay.append(dec_c)
    interleave[1]()

    o_intra = [[jnp.dot(jnp.where(causal, att[c][h], 0.0).astype(BF16), hi[rws, cs],
                        preferred_element_type=F32) for h, cs in enumerate(heads)]
               for c, rws in enumerate(chunks)]
    interleave[2]()

    for h, cs in enumerate(heads):
        if h == n_heads // 2:
            interleave[3]()
        st = state_ref[h]
        for c, rws in enumerate(chunks):
            o = o_intra[c][h] + lax.dot_general(
                q_dec[c][:, cs], st.astype(BF16), (((1,), (1,)), ((), ())),
                preferred_element_type=F32)
            ms = jnp.mean(o * o, axis=-1, keepdims=True)
            y = (o * lax.rsqrt(ms + NORM_EPS)) * norm_w_ref[:, cs] * gate[rws, cs]
            o_ref[rws, cs] = y.astype(o_ref.dtype)
            st = decay[c][:, cs] * st + upd[c][h]
        state_ref[h] = st


def _in_proj_kernel(x_ref, pos_ref, freq_ref, spread_ref, cos_base_ref, sgn_up_ref, sgn_dn_ref,
                    nw_ref, lbraw_ref, hnw_ref, w_ref,
                    q_ref, k_ref, v_ref, ag_ref, yh_ref, state_ref, wb_ref,
                    *, width, layer, steps_per_seq):
    @pl.when(pl.program_id(0) == 0)
    def _():
        wb_ref[...] = w_ref[...].astype(BF16)

    @pl.when(pl.program_id(0) % steps_per_seq == 0)
    def _():
        state_ref[...] = jnp.zeros(state_ref.shape, F32)

    for t in range(PROJ_TILES):
        rows = pl.ds(t * PROJ_ROWS, PROJ_ROWS)
        _in_proj_tile(x_ref.at[rows], pos_ref[:, t * PROJ_ROWS:(t + 1) * PROJ_ROWS],
                      freq_ref, spread_ref, cos_base_ref, sgn_up_ref, sgn_dn_ref,
                      nw_ref, lbraw_ref, hnw_ref, wb_ref,
                      q_ref.at[rows], k_ref.at[rows], v_ref.at[rows], ag_ref.at[rows],
                      yh_ref.at[rows], state_ref, width, layer)


def _in_proj_tile(x_ref, pos, freq_ref, spread_ref, cos_base_ref, sgn_up_ref, sgn_dn_ref,
                  nw_ref, lbraw_ref, hnw_ref, w_ref,
                  q_ref, k_ref, v_ref, ag_ref, yh_ref, state_ref, width, layer):
    x = x_ref[...]
    ms = jnp.mean(x * x, axis=-1, keepdims=True)
    hn = ((x * lax.rsqrt(ms + NORM_EPS)) * nw_ref[...]).astype(BF16)

    ang = freq_ref[...] * pos.astype(F32)
    parts = []
    for t in (jnp.cos(ang), jnp.sin(ang)):
        hi = t.astype(BF16).astype(F32)
        parts += [hi, t - hi]
    tabs = lax.dot_general(jnp.concatenate(parts, axis=0).astype(BF16), spread_ref[...],
                           (((0,), (0,)), ((), ())), preferred_element_type=F32)
    cos = tabs[:, :LANES] + cos_base_ref[...]
    sin = tabs[:, LANES:]
    sgn_up = sgn_up_ref[...]
    sgn_dn = sgn_dn_ref[...]
    half = ROPE_DIMS // 2

    def rope(t):
        cols = []
        for j in range(width // LANES):
            tj = t[:, j * LANES:(j + 1) * LANES]
            up = pltpu.roll(tj, LANES - half, 1)
            dn = pltpu.roll(tj, half, 1)
            cols.append(tj * cos + sin * (up * sgn_up + dn * sgn_dn))
        return jnp.concatenate(cols, axis=1)

    def proj(g):
        return jnp.dot(hn, w_ref[:, g * width:(g + 1) * width], preferred_element_type=F32)

    scale = ATTN_HEAD_DIM ** -0.5 * LOG2_E

    def emit_q():
        q_ref[...] = (rope(proj(0)) * scale).astype(BF16)

    def emit_k():
        k_ref[...] = rope(proj(1)).astype(BF16)

    def emit_v():
        v_ref[...] = proj(2).astype(BF16)

    def emit_ag():
        ag_ref[...] = _silu(proj(3)).astype(BF16)

    hf = proj(5)
    hq = _silu(proj(4))
    hi = proj(6).astype(BF16)
    gate = _silu(proj(7))
    _hgrn_tile(hq, hf, hi, gate, lbraw_ref, hnw_ref, state_ref, yh_ref, layer,
               (emit_q, emit_k, emit_v, emit_ag))


def _rope_tables():
    half = ROPE_DIMS // 2
    inv_freq = ROPE_THETA ** (-np.arange(half, dtype=np.float32) * (2.0 / ROPE_DIMS))
    d = np.arange(LANES) % ATTN_HEAD_DIM
    rotary = d < ROPE_DIMS
    lane_uses = (np.arange(half)[:, None] == (d % half)[None, :]) & rotary[None, :]
    spread = np.zeros((4 * half, 2 * LANES), np.float32)
    for t in range(4):
        table = t // 2
        spread[t * half:(t + 1) * half, table * LANES:(table + 1) * LANES] = lane_uses
    cos_base = np.where(rotary, 0.0, 1.0).astype(np.float32)
    sgn_up = np.where(d < half, -1.0, 0.0).astype(np.float32)
    sgn_dn = np.where((d >= half) & rotary, 1.0, 0.0).astype(np.float32)
    return (inv_freq.astype(np.float32)[:, None], jnp.asarray(spread, BF16), cos_base[None],
            sgn_up[None], sgn_dn[None])


def _in_proj(x2, positions, norm_w, w, lb_raw, hgrn_norm_w, width, layer, seq):
    n, d_model = x2.shape
    rows = PROJ_ROWS * PROJ_TILES
    pos3 = positions.reshape(n // rows, 1, rows)
    consts = _rope_tables()
    row_spec = lambda c: pl.BlockSpec((rows, c), lambda i: (i, 0))
    full_spec = lambda a: pl.BlockSpec(a.shape, lambda i: (0,) * a.ndim)
    pos_spec = pl.BlockSpec((None, 1, rows), lambda i: (i, 0, 0))
    out_sds = jax.ShapeDtypeStruct((n, width), BF16)
    n_out = 5
    small = [*consts, norm_w, lb_raw, hgrn_norm_w]
    return pl.pallas_call(
        functools.partial(_in_proj_kernel, width=width, layer=layer, steps_per_seq=seq // rows),
        out_shape=[out_sds] * n_out,
        grid=(n // rows,),
        in_specs=[row_spec(d_model), pos_spec] + [full_spec(a) for a in small] + [
            pl.BlockSpec(w.shape, lambda i: (0, 0), pipeline_mode=pl.Buffered(1))],
        out_specs=[row_spec(width)] * n_out,
        scratch_shapes=[pltpu.VMEM((width // HGRN_HEAD_DIM, HGRN_HEAD_DIM, HGRN_HEAD_DIM), F32),
                        pltpu.VMEM(w.shape, BF16)],
        compiler_params=pltpu.CompilerParams(
            dimension_semantics=("arbitrary",), vmem_limit_bytes=VMEM_LIMIT),
        name="in_proj",
    )(x2, pos3, *small, w)


def _attn_kernel(q_ref, k_ref, v_ref, gate_ref, bias_ref, bias0_ref, seg_ref, nw_ref, o_ref,
                 qf, kf, vf, q4, k4, v4, o_pat, m_pat, l_pat, *, seq):
    blk = ATTN_BLOCK
    dint = ATTN_DEINTERLEAVE
    len4 = seq // dint
    pad4 = blk * max(d for _, d in DILATED_PATTERNS) // dint
    seg4 = pad4 + len4

    qf[...] = q_ref[...].astype(F32)
    for src, dst, dst4 in ((k_ref, kf, k4), (v_ref, vf, v4)):
        dst[0:blk, :] = jnp.zeros((blk, LANES), F32)
        dst[blk:blk + seq, :] = src[...].astype(F32)
        for r in range(dint):
            dst4[r * seg4:r * seg4 + pad4, :] = jnp.zeros((pad4, LANES), F32)
            dst4[r * seg4 + pad4:(r + 1) * seg4, :] = dst[pl.ds(blk + r, len4, stride=dint), :]
    for r in range(dint):
        q4[r * len4:(r + 1) * len4, :] = qf[pl.ds(r, len4, stride=dint), :]

    lane = lax.broadcasted_iota(jnp.int32, (1, LANES), 1)
    head_a = lane < ATTN_HEAD_DIM
    head_b = jnp.logical_not(head_a)

    def rows_of(start, size, stride):
        if stride == 1:
            return pl.ds(pl.multiple_of(start, blk), size)
        return pl.ds(start, size, stride=stride)

    for pat, (window, dil) in enumerate(DILATED_PATTERNS):
        assert window // dil == blk and (dil == 1 or dil % dint == 0)
        n_blk = seq // dil // blk
        n_run = min(ATTN_RUN, n_blk)
        n_res = min(ATTN_CHAINS // n_run, dil)
        runs_per_res = n_blk // n_run
        stride = 1 if dil == 1 else dil // dint
        q_src, k_src, v_src = (qf, kf, vf) if dil == 1 else (q4, k4, v4)

        def body(i, carry, pat=pat, dil=dil, n_run=n_run, n_res=n_res,
                 runs_per_res=runs_per_res, stride=stride, q_src=q_src, k_src=k_src, v_src=v_src):
            for j in range(n_res):
                r = (i // runs_per_res) * n_res + j
                n0 = (i % runs_per_res) * n_run
                if dil == 1:
                    q_start = blk * n0
                    k_start = blk + blk * (n0 - 1)
                else:
                    seg, off = r % dint, r // dint
                    q_start = seg * len4 + off + stride * blk * n0
                    k_start = seg * seg4 + pad4 + off + stride * blk * (n0 - 1)
                k_rows = rows_of(k_start, (n_run + 1) * blk, stride)
                kb = k_src[k_rows, :]
                k_a = jnp.where(head_a, kb, 0.0).astype(BF16)
                k_b = jnp.where(head_b, kb, 0.0).astype(BF16)
                vb = v_src[k_rows, :]
                v_a = jnp.where(head_a, vb, 1.0).astype(BF16)
                v_b = jnp.where(head_b, vb, 1.0).astype(BF16)
                qb = q_src[rows_of(q_start, n_run * blk, stride), :].astype(BF16)
                for u in range(n_run):
                    bias = bias_ref[...]
                    if u == 0:
                        bias = jnp.where(n0 == 0, bias0_ref[...], bias)
                    q_u = qb[u * blk:(u + 1) * blk]
                    keys = slice(u * blk, (u + 2) * blk)

                    def one_head(k_h, v_h):
                        s = lax.dot_general(q_u, k_h[keys], (((1,), (1,)), ((), ())),
                                            preferred_element_type=F32) + bias
                        m = jnp.max(s, axis=-1, keepdims=True)
                        p = jnp.exp2(s - m).astype(BF16)
                        return jnp.dot(p, v_h[keys], preferred_element_type=F32), m

                    ol_a, m_a = one_head(k_a, v_a)
                    ol_b, m_b = one_head(k_b, v_b)
                    rows = rows_of(q_start + u * stride * blk, blk, stride)
                    o_pat[pat, rows, :] = jnp.where(head_a, ol_a, ol_b)
                    m_pat[pat, rows, :] = jnp.where(head_a, m_a, m_b)
                    l_pat[pat, rows, :] = jnp.where(head_a, ol_b, ol_a)
            return carry

        lax.fori_loop(0, dil * n_blk // (n_run * n_res), body, 0)

    blocks_per_seg = len4 // blk

    def merge(i, carry):
        seg, c = i // blocks_per_seg, i % blocks_per_seg
        rows4 = pl.ds(pl.multiple_of(seg * len4 + c * blk, blk), blk)
        rows1 = pl.ds(seg + dint * blk * c, blk, stride=dint)
        rows = [rows1 if dil == 1 else rows4 for _, dil in DILATED_PATTERNS]
        ms = [m_pat[p, rw, :] for p, rw in enumerate(rows)]
        m_top = functools.reduce(jnp.maximum, ms)
        ws = [jnp.exp2(m - m_top) for m in ms]
        num = sum(w * o_pat[p, rw, :] for p, (w, rw) in enumerate(zip(ws, rows)))
        den = sum(w * pltpu.roll(l_pat[p, rw, :], ATTN_HEAD_DIM, 1)
                  for p, (w, rw) in enumerate(zip(ws, rows)))
        ssq = jnp.dot((num * num).astype(BF16), seg_ref[...], preferred_element_type=F32)
        out = num * lax.rsqrt(ssq * (1.0 / ATTN_HEAD_DIM) + NORM_EPS * (den * den))
        qf[rows1, :] = out
        return carry

    assert DILATED_PATTERNS[0][1] == 1
    lax.fori_loop(0, seq // blk, merge, 0, unroll=8)
    o_ref[...] = ((qf[...] * nw_ref[...]) * gate_ref[...].astype(F32)).astype(BF16)


def _band_bias():
    qi = np.arange(ATTN_BLOCK)[:, None]
    kj = np.arange(2 * ATTN_BLOCK)[None, :]
    dist = ATTN_BLOCK + qi - kj
    band = (dist >= 0) & (dist <= ATTN_BLOCK)
    first = band & (kj >= ATTN_BLOCK)
    to_bias = lambda mk: np.where(mk, 0.0, -np.inf).astype(np.float32)
    return to_bias(band), to_bias(first)


def _attention(q, k, v, gate, norm_w):
    b, seq, width = q.shape
    bias, bias0 = _band_bias()
    n_pat = len(DILATED_PATTERNS)
    pad4 = ATTN_BLOCK * max(d for _, d in DILATED_PATTERNS) // ATTN_DEINTERLEAVE
    rows4 = seq + ATTN_DEINTERLEAVE * pad4
    head_of_lane = np.arange(LANES) // ATTN_HEAD_DIM
    seg = jnp.asarray((head_of_lane[:, None] == head_of_lane[None, :]).astype(np.float32), BF16)
    col_spec = pl.BlockSpec((None, seq, LANES), lambda i, j: (i, 0, j))
    bias_spec = pl.BlockSpec(bias.shape, lambda i, j: (0, 0))
    seg_spec = pl.BlockSpec(seg.shape, lambda i, j: (0, 0))
    nw_spec = pl.BlockSpec((1, LANES), lambda i, j: (0, j))
    return pl.pallas_call(
        functools.partial(_attn_kernel, seq=seq),
        out_shape=jax.ShapeDtypeStruct((b, seq, width), BF16),
        grid=(b, width // LANES),
        in_specs=[col_spec, col_spec, col_spec, col_spec, bias_spec, bias_spec, seg_spec, nw_spec],
        out_specs=col_spec,
        scratch_shapes=[pltpu.VMEM((seq, LANES), F32),
                        pltpu.VMEM((ATTN_BLOCK + seq, LANES), F32),
                        pltpu.VMEM((ATTN_BLOCK + seq, LANES), F32),
                        pltpu.VMEM((seq, LANES), F32),
                        pltpu.VMEM((rows4, LANES), F32),
                        pltpu.VMEM((rows4, LANES), F32),
                        pltpu.VMEM((n_pat, seq, LANES), F32),
                        pltpu.VMEM((n_pat, seq, LANES), F32),
                        pltpu.VMEM((n_pat, seq, LANES), F32)],
        compiler_params=pltpu.CompilerParams(
            dimension_semantics=("parallel", "parallel"), vmem_limit_bytes=VMEM_LIMIT),
        name="dilated_attn",
    )(q, k, v, gate, bias, bias0, seg, norm_w)


def _out_proj_kernel(x_ref, ya_ref, yh_ref, fnw_ref, w_ref, out_ref, *, attn_width):
    mixed = jnp.dot(ya_ref[...], w_ref[0:attn_width, :].astype(BF16),
                    preferred_element_type=F32)
    mixed = mixed + jnp.dot(yh_ref[...], w_ref[attn_width:, :].astype(BF16),
                            preferred_element_type=F32)
    x = x_ref[...] + mixed
    ms = jnp.mean(x * x, axis=-1, keepdims=True)
    out_ref[...] = (x * lax.rsqrt(ms + NORM_EPS)) * fnw_ref[...]


def _out_proj(x2, ya, yh, fnw, w):
    n, d_model = x2.shape
    attn_width = ya.shape[-1]
    rows = OUT_ROWS
    row_spec = lambda c: pl.BlockSpec((rows, c), lambda i: (i, 0))
    full_spec = lambda a: pl.BlockSpec(a.shape, lambda i: (0,) * a.ndim)
    return pl.pallas_call(
        functools.partial(_out_proj_kernel, attn_width=attn_width),
        out_shape=jax.ShapeDtypeStruct((n, d_model), F32),
        grid=(n // rows,),
        in_specs=[row_spec(d_model), row_spec(attn_width), row_spec(yh.shape[-1]),
                  full_spec(fnw), full_spec(w)],
        out_specs=row_spec(d_model),
        compiler_params=pltpu.CompilerParams(
            dimension_semantics=("parallel",), vmem_limit_bytes=VMEM_LIMIT),
        name="out_proj",
    )(x2, ya, yh, fnw, w)


def kernel(x, positions, w_in, w_out, mix_norm_w, attn_out_norm_w, hgrn_out_norm_w,
           hgrn_lb_raw, final_norm_w):
    b, seq, d_model = x.shape
    depth = w_in.shape[0]
    attn_width = attn_out_norm_w.shape[-1]
    hgrn_width = hgrn_out_norm_w.shape[-1]
    assert depth == 1 and attn_width == hgrn_width and w_in.shape[-1] == 8 * attn_width
    assert seq % (max(d for _, d in DILATED_PATTERNS) * ATTN_BLOCK) == 0
    n = b * seq
    layer = 0
    x2 = x.reshape(n, d_model)
    q, k, v, ag, yh = _in_proj(
        x2, positions, mix_norm_w[layer][None], w_in[layer], hgrn_lb_raw,
        hgrn_out_norm_w[layer][None], attn_width, layer, seq)
    to3 = lambda t: t.reshape(b, seq, t.shape[-1])
    ya = _attention(to3(q), to3(k), to3(v), to3(ag), attn_out_norm_w[layer][None])
    out = _out_proj(x2, ya.reshape(n, attn_width), yh, final_norm_w[None],
                    w_out[layer])
    return out.reshape(b, seq, d_model)
```

```python
import functools

import numpy as np
import jax
import jax.numpy as jnp
from jax import lax
from jax.experimental import pallas as pl
from jax.experimental.pallas import tpu as pltpu

F32 = jnp.float32
BF16 = jnp.bfloat16

LANES = 128
SUBLANES = 8
ATTN_HEAD_DIM = 64
HGRN_HEAD_DIM = 128
DILATED_PATTERNS = ((128, 1), (512, 4), (2048, 16))
ATTN_BLOCK = 128
ATTN_DEINTERLEAVE = 4
ROPE_THETA = 500000.0
ROPE_DIMS = ATTN_HEAD_DIM // 4
HGRN_CHUNK = 64
NORM_EPS = 1e-6
LOG2_E = 1.4426950408889634
VMEM_LIMIT = 56 * 1024 * 1024

PROJ_ROWS = 512
PROJ_TILES = 2
PROJ_STAGES = 6
PROJ_SKEW = 2
OUT_ROWS = 2048
ATTN_RUN = 32
ATTN_CHAINS = 32


def _silu(t):
    return t * jax.nn.sigmoid(t)


def _cumsum_rows(t):
    n_rows, width = t.shape
    row = lax.broadcasted_iota(jnp.int32, (n_rows, width), 0)
    shift = 1
    while shift < n_rows:
        if shift < SUBLANES:
            prev = jnp.where(row >= shift, pltpu.roll(t, shift, 0), 0.0)
        else:
            prev = jnp.concatenate([jnp.zeros((shift, width), t.dtype), t[:n_rows - shift]], axis=0)
        t = t + prev
        shift *= 2
    return t


def _hgrn_tile(hq, hf, hi, gate, lbraw_ref, norm_w_ref, state_ref, o_ref, layer, interleave):
    chunk = HGRN_CHUNK
    rows, width = hq.shape
    n_chunks = rows // chunk
    n_heads = width // HGRN_HEAD_DIM
    heads = [slice(h * HGRN_HEAD_DIM, (h + 1) * HGRN_HEAD_DIM) for h in range(n_heads)]
    chunks = [slice(c * chunk, (c + 1) * chunk) for c in range(n_chunks)]

    raw = lbraw_ref[...]
    e = jnp.exp(raw - jnp.max(raw, axis=0, keepdims=True))
    sm = e / jnp.sum(e, axis=0, keepdims=True)
    lb = jnp.sum(sm[0:layer + 1, :], axis=0, keepdims=True)

    ti = lax.broadcasted_iota(jnp.int32, (chunk, chunk), 0)
    si = lax.broadcasted_iota(jnp.int32, (chunk, chunk), 1)
    causal = ti >= si

    hkey, cum = [], []
    for rws in chunks:
        f = lb + (1.0 - lb) * jax.nn.sigmoid(hf[rws, :])
        hkey.append(1.0 - f)
        cum.append(_cumsum_rows(jnp.log(f)))
    interleave[0]()
    yield

    q_dec, decay, att, upd = [], [], [], []
    for c, rws in enumerate(chunks):
        dec_c = jnp.exp(cum[c][chunk - 1:chunk, :])
        q_c = (hq[rws, :] * jnp.exp(cum[c])).astype(BF16)
        k_inv = hkey[c] * jnp.exp(-cum[c])
        k_end = (k_inv * dec_c).astype(BF16)
        k_inv = k_inv.astype(BF16)
        att.append([lax.dot_general(q_c[:, cs], k_inv[:, cs], (((1,), (1,)), ((), ())),
                                    preferred_element_type=F32) for cs in heads])
        upd.append([lax.dot_general(hi[rws, cs], k_end[:, cs], (((0,), (0,)), ((), ())),
                                    preferred_element_type=F32) for cs in heads])
        q_dec.append(q_c)
        decay.append(dec_c)
    interleave[1]()
    yield

    o_intra = [[jnp.dot(jnp.where(causal, att[c][h], 0.0).astype(BF16), hi[rws, cs],
                        preferred_element_type=F32) for h, cs in enumerate(heads)]
               for c, rws in enumerate(chunks)]
    interleave[2]()
    yield

    for h, cs in enumerate(heads):
        if h == n_heads // 2:
            interleave[3]()
            yield
        st = state_ref[h]
        for c, rws in enumerate(chunks):
            o = o_intra[c][h] + lax.dot_general(
                q_dec[c][:, cs], st.astype(BF16), (((1,), (1,)), ((), ())),
                preferred_element_type=F32)
            ms = jnp.mean(o * o, axis=-1, keepdims=True)
            y = (o * lax.rsqrt(ms + NORM_EPS)) * norm_w_ref[:, cs] * gate[rws, cs]
            o_ref[rws, cs] = y.astype(o_ref.dtype)
            st = decay[c][:, cs] * st + upd[c][h]
        state_ref[h] = st


def _in_proj_kernel(x_ref, pos_ref, freq_ref, spread_ref, cos_base_ref, sgn_up_ref, sgn_dn_ref,
                    nw_ref, lbraw_ref, hnw_ref, w_ref,
                    q_ref, k_ref, v_ref, ag_ref, yh_ref, state_ref,
                    *, width, layer, steps_per_seq):
    @pl.when(pl.program_id(0) % steps_per_seq == 0)
    def _():
        state_ref[...] = jnp.zeros(state_ref.shape, F32)

    tiles = []
    for t in range(PROJ_TILES):
        rows = pl.ds(t * PROJ_ROWS, PROJ_ROWS)
        tiles.append(_in_proj_tile(
            x_ref.at[rows], pos_ref[:, t * PROJ_ROWS:(t + 1) * PROJ_ROWS],
            freq_ref, spread_ref, cos_base_ref, sgn_up_ref, sgn_dn_ref,
            nw_ref, lbraw_ref, hnw_ref, w_ref,
            q_ref.at[rows], k_ref.at[rows], v_ref.at[rows], ag_ref.at[rows],
            yh_ref.at[rows], state_ref, width, layer))
    for step in range(PROJ_STAGES + PROJ_SKEW * (PROJ_TILES - 1)):
        for t, tile in enumerate(tiles):
            if 0 <= step - PROJ_SKEW * t < PROJ_STAGES:
                next(tile, None)


def _in_proj_tile(x_ref, pos, freq_ref, spread_ref, cos_base_ref, sgn_up_ref, sgn_dn_ref,
                  nw_ref, lbraw_ref, hnw_ref, w_ref,
                  q_ref, k_ref, v_ref, ag_ref, yh_ref, state_ref, width, layer):
    x = x_ref[...]
    ms = jnp.mean(x * x, axis=-1, keepdims=True)
    hn = ((x * lax.rsqrt(ms + NORM_EPS)) * nw_ref[...]).astype(BF16)

    ang = freq_ref[...] * pos.astype(F32)
    parts = []
    for t in (jnp.cos(ang), jnp.sin(ang)):
        hi = t.astype(BF16).astype(F32)
        parts += [hi, t - hi]
    tabs = lax.dot_general(jnp.concatenate(parts, axis=0).astype(BF16), spread_ref[...],
                           (((0,), (0,)), ((), ())), preferred_element_type=F32)
    cos = tabs[:, :LANES] + cos_base_ref[...]
    sin = tabs[:, LANES:]
    sgn_up = sgn_up_ref[...]
    sgn_dn = sgn_dn_ref[...]
    half = ROPE_DIMS // 2

    def rope(t):
        cols = []
        for j in range(width // LANES):
            tj = t[:, j * LANES:(j + 1) * LANES]
            up = pltpu.roll(tj, LANES - half, 1)
            dn = pltpu.roll(tj, half, 1)
            cols.append(tj * cos + sin * (up * sgn_up + dn * sgn_dn))
        return jnp.concatenate(cols, axis=1)

    def proj(g):
        return jnp.dot(hn, w_ref[:, g * width:(g + 1) * width], preferred_element_type=F32)

    scale = ATTN_HEAD_DIM ** -0.5 * LOG2_E

    def emit_q():
        q_ref[...] = (rope(proj(0)) * scale).astype(BF16)

    def emit_k():
        k_ref[...] = rope(proj(1)).astype(BF16)

    def emit_v():
        v_ref[...] = proj(2).astype(BF16)

    def emit_ag():
        ag_ref[...] = _silu(proj(3)).astype(BF16)

    hf = proj(5)
    hq = _silu(proj(4))
    hi = proj(6).astype(BF16)
    gate = _silu(proj(7))
    yield
    yield from _hgrn_tile(hq, hf, hi, gate, lbraw_ref, hnw_ref, state_ref, yh_ref, layer,
                          (emit_q, emit_k, emit_v, emit_ag))


def _rope_tables():
    half = ROPE_DIMS // 2
    inv_freq = ROPE_THETA ** (-np.arange(half, dtype=np.float32) * (2.0 / ROPE_DIMS))
    d = np.arange(LANES) % ATTN_HEAD_DIM
    rotary = d < ROPE_DIMS
    lane_uses = (np.arange(half)[:, None] == (d % half)[None, :]) & rotary[None, :]
    spread = np.zeros((4 * half, 2 * LANES), np.float32)
    for t in range(4):
        table = t // 2
        spread[t * half:(t + 1) * half, table * LANES:(table + 1) * LANES] = lane_uses
    cos_base = np.where(rotary, 0.0, 1.0).astype(np.float32)
    sgn_up = np.where(d < half, -1.0, 0.0).astype(np.float32)
    sgn_dn = np.where((d >= half) & rotary, 1.0, 0.0).astype(np.float32)
    return (inv_freq.astype(np.float32)[:, None], jnp.asarray(spread, BF16), cos_base[None],
            sgn_up[None], sgn_dn[None])


def _in_proj(x2, positions, norm_w, w, lb_raw, hgrn_norm_w, width, layer, seq):
    n, d_model = x2.shape
    rows = PROJ_ROWS * PROJ_TILES
    pos3 = positions.reshape(n // rows, 1, rows)
    consts = _rope_tables()
    row_spec = lambda c: pl.BlockSpec((rows, c), lambda i: (i, 0))
    full_spec = lambda a: pl.BlockSpec(a.shape, lambda i: (0,) * a.ndim)
    pos_spec = pl.BlockSpec((None, 1, rows), lambda i: (i, 0, 0))
    out_sds = jax.ShapeDtypeStruct((n, width), BF16)
    n_out = 5
    small = [*consts, norm_w, lb_raw, hgrn_norm_w]
    return pl.pallas_call(
        functools.partial(_in_proj_kernel, width=width, layer=layer, steps_per_seq=seq // rows),
        out_shape=[out_sds] * n_out,
        grid=(n // rows,),
        in_specs=[row_spec(d_model), pos_spec] + [full_spec(a) for a in small] + [
            pl.BlockSpec(w.shape, lambda i: (0, 0), pipeline_mode=pl.Buffered(1))],
        out_specs=[row_spec(width)] * n_out,
        scratch_shapes=[pltpu.VMEM((width // HGRN_HEAD_DIM, HGRN_HEAD_DIM, HGRN_HEAD_DIM), F32)],
        compiler_params=pltpu.CompilerParams(
            dimension_semantics=("arbitrary",), vmem_limit_bytes=VMEM_LIMIT),
        name="in_proj",
    )(x2, pos3, *small, w)


def _attn_kernel(q_ref, k_ref, v_ref, gate_ref, bias_ref, bias0_ref, seg_ref, nw_ref, o_ref,
                 qf, kf, vf, q4, k4, v4, o_pat, m_pat, l_pat, *, seq):
    blk = ATTN_BLOCK
    dint = ATTN_DEINTERLEAVE
    len4 = seq // dint
    pad4 = blk * max(d for _, d in DILATED_PATTERNS) // dint
    seg4 = pad4 + len4

    qf[...] = q_ref[...].astype(F32)
    for src, dst, dst4 in ((k_ref, kf, k4), (v_ref, vf, v4)):
        dst[0:blk, :] = jnp.zeros((blk, LANES), F32)
        dst[blk:blk + seq, :] = src[...].astype(F32)
        for r in range(dint):
            dst4[r * seg4:r * seg4 + pad4, :] = jnp.zeros((pad4, LANES), F32)
            dst4[r * seg4 + pad4:(r + 1) * seg4, :] = dst[pl.ds(blk + r, len4, stride=dint), :]
    for r in range(dint):
        q4[r * len4:(r + 1) * len4, :] = qf[pl.ds(r, len4, stride=dint), :]

    lane = lax.broadcasted_iota(jnp.int32, (1, LANES), 1)
    head_a = lane < ATTN_HEAD_DIM
    head_b = jnp.logical_not(head_a)

    def rows_of(start, size, stride):
        if stride == 1:
            return pl.ds(pl.multiple_of(start, blk), size)
        return pl.ds(start, size, stride=stride)

    for pat, (window, dil) in enumerate(DILATED_PATTERNS):
        assert window // dil == blk and (dil == 1 or dil % dint == 0)
        n_blk = seq // dil // blk
        n_run = min(ATTN_RUN, n_blk)
        n_res = min(ATTN_CHAINS // n_run, dil)
        runs_per_res = n_blk // n_run
        stride = 1 if dil == 1 else dil // dint
        q_src, k_src, v_src = (qf, kf, vf) if dil == 1 else (q4, k4, v4)

        def body(i, carry, pat=pat, dil=dil, n_run=n_run, n_res=n_res,
                 runs_per_res=runs_per_res, stride=stride, q_src=q_src, k_src=k_src, v_src=v_src):
            for j in range(n_res):
                r = (i // runs_per_res) * n_res + j
                n0 = (i % runs_per_res) * n_run
                if dil == 1:
                    q_start = blk * n0
                    k_start = blk + blk * (n0 - 1)
                else:
                    seg, off = r % dint, r // dint
                    q_start = seg * len4 + off + stride * blk * n0
                    k_start = seg * seg4 + pad4 + off + stride * blk * (n0 - 1)
                k_rows = rows_of(k_start, (n_run + 1) * blk, stride)
                kb = k_src[k_rows, :]
                k_a = jnp.where(head_a, kb, 0.0).astype(BF16)
                k_b = jnp.where(head_b, kb, 0.0).astype(BF16)
                vb = v_src[k_rows, :]
                v_a = jnp.where(head_a, vb, 1.0).astype(BF16)
                v_b = jnp.where(head_b, vb, 1.0).astype(BF16)
                qb = q_src[rows_of(q_start, n_run * blk, stride), :].astype(BF16)
                for u in range(n_run):
                    bias = bias_ref[...]
                    if u == 0:
                        bias = jnp.where(n0 == 0, bias0_ref[...], bias)
                    q_u = qb[u * blk:(u + 1) * blk]
                    keys = slice(u * blk, (u + 2) * blk)

                    def one_head(k_h, v_h):
                        s = lax.dot_general(q_u, k_h[keys], (((1,), (1,)), ((), ())),
                                            preferred_element_type=F32) + bias
                        m = jnp.max(s, axis=-1, keepdims=True)
                        p = jnp.exp2(s - m).astype(BF16)
                        return jnp.dot(p, v_h[keys], preferred_element_type=F32), m

                    ol_a, m_a = one_head(k_a, v_a)
                    ol_b, m_b = one_head(k_b, v_b)
                    rows = rows_of(q_start + u * stride * blk, blk, stride)
                    o_pat[pat, rows, :] = jnp.where(head_a, ol_a, ol_b)
                    m_pat[pat, rows, :] = jnp.where(head_a, m_a, m_b)
                    l_pat[pat, rows, :] = jnp.where(head_a, ol_b, ol_a)
            return carry

        lax.fori_loop(0, dil * n_blk // (n_run * n_res), body, 0)

    blocks_per_seg = len4 // blk

    def merge(i, carry):
        seg, c = i // blocks_per_seg, i % blocks_per_seg
        rows4 = pl.ds(pl.multiple_of(seg * len4 + c * blk, blk), blk)
        rows1 = pl.ds(seg + dint * blk * c, blk, stride=dint)
        rows = [rows1 if dil == 1 else rows4 for _, dil in DILATED_PATTERNS]
        ms = [m_pat[p, rw, :] for p, rw in enumerate(rows)]
        m_top = functools.reduce(jnp.maximum, ms)
        ws = [jnp.exp2(m - m_top) for m in ms]
        num = sum(w * o_pat[p, rw, :] for p, (w, rw) in enumerate(zip(ws, rows)))
        den = sum(w * pltpu.roll(l_pat[p, rw, :], ATTN_HEAD_DIM, 1)
                  for p, (w, rw) in enumerate(zip(ws, rows)))
        ssq = jnp.dot((num * num).astype(BF16), seg_ref[...], preferred_element_type=F32)
        out = num * lax.rsqrt(ssq * (1.0 / ATTN_HEAD_DIM) + NORM_EPS * (den * den))
        qf[rows1, :] = out
        return carry

    assert DILATED_PATTERNS[0][1] == 1
    lax.fori_loop(0, seq // blk, merge, 0, unroll=8)
    o_ref[...] = ((qf[...] * nw_ref[...]) * gate_ref[...].astype(F32)).astype(BF16)


def _band_bias():
    qi = np.arange(ATTN_BLOCK)[:, None]
    kj = np.arange(2 * ATTN_BLOCK)[None, :]
    dist = ATTN_BLOCK + qi - kj
    band = (dist >= 0) & (dist <= ATTN_BLOCK)
    first = band & (kj >= ATTN_BLOCK)
    to_bias = lambda mk: np.where(mk, 0.0, -np.inf).astype(np.float32)
    return to_bias(band), to_bias(first)


def _attention(q, k, v, gate, norm_w):
    b, seq, width = q.shape
    bias, bias0 = _band_bias()
    n_pat = len(DILATED_PATTERNS)
    pad4 = ATTN_BLOCK * max(d for _, d in DILATED_PATTERNS) // ATTN_DEINTERLEAVE
    rows4 = seq + ATTN_DEINTERLEAVE * pad4
    head_of_lane = np.arange(LANES) // ATTN_HEAD_DIM
    seg = jnp.asarray((head_of_lane[:, None] == head_of_lane[None, :]).astype(np.float32), BF16)
    col_spec = pl.BlockSpec((None, seq, LANES), lambda i, j: (i, 0, j))
    bias_spec = pl.BlockSpec(bias.shape, lambda i, j: (0, 0))
    seg_spec = pl.BlockSpec(seg.shape, lambda i, j: (0, 0))
    nw_spec = pl.BlockSpec((1, LANES), lambda i, j: (0, j))
    return pl.pallas_call(
        functools.partial(_attn_kernel, seq=seq),
        out_shape=jax.ShapeDtypeStruct((b, seq, width), BF16),
        grid=(b, width // LANES),
        in_specs=[col_spec, col_spec, col_spec, col_spec, bias_spec, bias_spec, seg_spec, nw_spec],
        out_specs=col_spec,
        scratch_shapes=[pltpu.VMEM((seq, LANES), F32),
                        pltpu.VMEM((ATTN_BLOCK + seq, LANES), F32),
                        pltpu.VMEM((ATTN_BLOCK + seq, LANES), F32),
                        pltpu.VMEM((seq, LANES), F32),
                        pltpu.VMEM((rows4, LANES), F32),
                        pltpu.VMEM((rows4, LANES), F32),
                        pltpu.VMEM((n_pat, seq, LANES), F32),
                        pltpu.VMEM((n_pat, seq, LANES), F32),
                        pltpu.VMEM((n_pat, seq, LANES), F32)],
        compiler_params=pltpu.CompilerParams(
            dimension_semantics=("parallel", "parallel"), vmem_limit_bytes=VMEM_LIMIT),
        name="dilated_attn",
    )(q, k, v, gate, bias, bias0, seg, norm_w)


def _out_proj_kernel(x_ref, ya_ref, yh_ref, fnw_ref, w_ref, out_ref, *, attn_width):
    mixed = jnp.dot(ya_ref[...], w_ref[0:attn_width, :].astype(BF16),
                    preferred_element_type=F32)
    mixed = mixed + jnp.dot(yh_ref[...], w_ref[attn_width:, :].astype(BF16),
                            preferred_element_type=F32)
    x = x_ref[...] + mixed
    ms = jnp.mean(x * x, axis=-1, keepdims=True)
    out_ref[...] = (x * lax.rsqrt(ms + NORM_EPS)) * fnw_ref[...]


def _out_proj(x2, ya, yh, fnw, w):
    n, d_model = x2.shape
    attn_width = ya.shape[-1]
    rows = OUT_ROWS
    row_spec = lambda c: pl.BlockSpec((rows, c), lambda i: (i, 0))
    full_spec = lambda a: pl.BlockSpec(a.shape, lambda i: (0,) * a.ndim)
    return pl.pallas_call(
        functools.partial(_out_proj_kernel, attn_width=attn_width),
        out_shape=jax.ShapeDtypeStruct((n, d_model), F32),
        grid=(n // rows,),
        in_specs=[row_spec(d_model), row_spec(attn_width), row_spec(yh.shape[-1]),
                  full_spec(fnw), full_spec(w)],
        out_specs=row_spec(d_model),
        compiler_params=pltpu.CompilerParams(
            dimension_semantics=("parallel",), vmem_limit_bytes=VMEM_LIMIT),
        name="out_proj",
    )(x2, ya, yh, fnw, w)


def kernel(x, positions, w_in, w_out, mix_norm_w, attn_out_norm_w, hgrn_out_norm_w,
           hgrn_lb_raw, final_norm_w):
    b, seq, d_model = x.shape
    depth = w_in.shape[0]
    attn_width = attn_out_norm_w.shape[-1]
    hgrn_width = hgrn_out_norm_w.shape[-1]
    assert depth == 1 and attn_width == hgrn_width and w_in.shape[-1] == 8 * attn_width
    assert seq % (max(d for _, d in DILATED_PATTERNS) * ATTN_BLOCK) == 0
    n = b * seq
    layer = 0
    x2 = x.reshape(n, d_model)
    q, k, v, ag, yh = _in_proj(
        x2, positions, mix_norm_w[layer][None], w_in[layer].astype(BF16), hgrn_lb_raw,
        hgrn_out_norm_w[layer][None], attn_width, layer, seq)
    to3 = lambda t: t.reshape(b, seq, t.shape[-1])
    ya = _attention(to3(q), to3(k), to3(v), to3(ag), attn_out_norm_w[layer][None])
    out = _out_proj(x2, ya.reshape(n, attn_width), yh, final_norm_w[None],
                    w_out[layer])
    return out.reshape(b, seq, d_model)
```

```python
import functools

import numpy as np
import jax
import jax.numpy as jnp
from jax import lax
from jax.experimental import pallas as pl
from jax.experimental.pallas import tpu as pltpu

F32 = jnp.float32
BF16 = jnp.bfloat16

LANES = 128
SUBLANES = 8
ATTN_HEAD_DIM = 64
HGRN_HEAD_DIM = 128
DILATED_PATTERNS = ((128, 1), (512, 4), (2048, 16))
ATTN_BLOCK = 128
ATTN_DEINTERLEAVE = 4
ROPE_THETA = 500000.0
ROPE_DIMS = ATTN_HEAD_DIM // 4
HGRN_CHUNK = 64
NORM_EPS = 1e-6
LOG2_E = 1.4426950408889634
VMEM_LIMIT = 56 * 1024 * 1024

PROJ_ROWS = 512
PROJ_TILES = 2
PROJ_STAGES = 6
PROJ_SKEW = 1
OUT_ROWS = 2048
ATTN_RUN = 32
ATTN_CHAINS = 32


def _silu(t):
    return t * jax.nn.sigmoid(t)


def _cumsum_rows(t):
    n_rows, width = t.shape
    row = lax.broadcasted_iota(jnp.int32, (n_rows, width), 0)
    shift = 1
    while shift < n_rows:
        if shift < SUBLANES:
            prev = jnp.where(row >= shift, pltpu.roll(t, shift, 0), 0.0)
        else:
            prev = jnp.concatenate([jnp.zeros((shift, width), t.dtype), t[:n_rows - shift]], axis=0)
        t = t + prev
        shift *= 2
    return t


def _hgrn_tile(hq, hf, hi, gate, lbraw_ref, norm_w_ref, state_ref, o_ref, layer, interleave):
    chunk = HGRN_CHUNK
    rows, width = hq.shape
    n_chunks = rows // chunk
    n_heads = width // HGRN_HEAD_DIM
    heads = [slice(h * HGRN_HEAD_DIM, (h + 1) * HGRN_HEAD_DIM) for h in range(n_heads)]
    chunks = [slice(c * chunk, (c + 1) * chunk) for c in range(n_chunks)]

    raw = lbraw_ref[...]
    e = jnp.exp(raw - jnp.max(raw, axis=0, keepdims=True))
    sm = e / jnp.sum(e, axis=0, keepdims=True)
    lb = jnp.sum(sm[0:layer + 1, :], axis=0, keepdims=True)

    ti = lax.broadcasted_iota(jnp.int32, (chunk, chunk), 0)
    si = lax.broadcasted_iota(jnp.int32, (chunk, chunk), 1)
    causal = ti >= si

    hkey, cum = [], []
    for rws in chunks:
        f = lb + (1.0 - lb) * jax.nn.sigmoid(hf[rws, :])
        hkey.append(1.0 - f)
        cum.append(_cumsum_rows(jnp.log(f)))
    interleave[0]()
    yield

    q_dec, decay, att, upd = [], [], [], []
    for c, rws in enumerate(chunks):
        dec_c = jnp.exp(cum[c][chunk - 1:chunk, :])
        q_c = (hq[rws, :] * jnp.exp(cum[c])).astype(BF16)
        k_inv = hkey[c] * jnp.exp(-cum[c])
        k_end = (k_inv * dec_c).astype(BF16)
        k_inv = k_inv.astype(BF16)
        att.append([lax.dot_general(q_c[:, cs], k_inv[:, cs], (((1,), (1,)), ((), ())),
                                    preferred_element_type=F32) for cs in heads])
        upd.append([lax.dot_general(hi[rws, cs], k_end[:, cs], (((0,), (0,)), ((), ())),
                                    preferred_element_type=F32) for cs in heads])
        q_dec.append(q_c)
        decay.append(dec_c)
    interleave[1]()
    yield

    o_intra = [[jnp.dot(jnp.where(causal, att[c][h], 0.0).astype(BF16), hi[rws, cs],
                        preferred_element_type=F32) for h, cs in enumerate(heads)]
               for c, rws in enumerate(chunks)]
    interleave[2]()
    yield

    for h, cs in enumerate(heads):
        if h == n_heads // 2:
            interleave[3]()
            yield
        st = state_ref[h]
        for c, rws in enumerate(chunks):
            o = o_intra[c][h] + lax.dot_general(
                q_dec[c][:, cs], st.astype(BF16), (((1,), (1,)), ((), ())),
                preferred_element_type=F32)
            ms = jnp.mean(o * o, axis=-1, keepdims=True)
            y = (o * lax.rsqrt(ms + NORM_EPS)) * norm_w_ref[:, cs] * gate[rws, cs]
            o_ref[rws, cs] = y.astype(o_ref.dtype)
            st = decay[c][:, cs] * st + upd[c][h]
        state_ref[h] = st


def _in_proj_kernel(x_ref, pos_ref, freq_ref, spread_ref, cos_base_ref, sgn_up_ref, sgn_dn_ref,
                    nw_ref, lbraw_ref, hnw_ref, w_ref,
                    q_ref, k_ref, v_ref, ag_ref, yh_ref, state_ref,
                    *, width, layer, steps_per_seq):
    @pl.when(pl.program_id(0) % steps_per_seq == 0)
    def _():
        state_ref[...] = jnp.zeros(state_ref.shape, F32)

    tiles = []
    for t in range(PROJ_TILES):
        rows = pl.ds(t * PROJ_ROWS, PROJ_ROWS)
        tiles.append(_in_proj_tile(
            x_ref.at[rows], pos_ref[:, t * PROJ_ROWS:(t + 1) * PROJ_ROWS],
            freq_ref, spread_ref, cos_base_ref, sgn_up_ref, sgn_dn_ref,
            nw_ref, lbraw_ref, hnw_ref, w_ref,
            q_ref.at[rows], k_ref.at[rows], v_ref.at[rows], ag_ref.at[rows],
            yh_ref.at[rows], state_ref, width, layer))
    for step in range(PROJ_STAGES + PROJ_SKEW * (PROJ_TILES - 1)):
        for t, tile in enumerate(tiles):
            if 0 <= step - PROJ_SKEW * t < PROJ_STAGES:
                next(tile, None)
    done = object()
    assert all(next(tile, done) is done for tile in tiles)


def _in_proj_tile(x_ref, pos, freq_ref, spread_ref, cos_base_ref, sgn_up_ref, sgn_dn_ref,
                  nw_ref, lbraw_ref, hnw_ref, w_ref,
                  q_ref, k_ref, v_ref, ag_ref, yh_ref, state_ref, width, layer):
    x = x_ref[...]
    ms = jnp.mean(x * x, axis=-1, keepdims=True)
    hn = ((x * lax.rsqrt(ms + NORM_EPS)) * nw_ref[...]).astype(BF16)

    ang = freq_ref[...] * pos.astype(F32)
    parts = []
    for t in (jnp.cos(ang), jnp.sin(ang)):
        hi = t.astype(BF16).astype(F32)
        parts += [hi, t - hi]
    tabs = lax.dot_general(jnp.concatenate(parts, axis=0).astype(BF16), spread_ref[...],
                           (((0,), (0,)), ((), ())), preferred_element_type=F32)
    cos = tabs[:, :LANES] + cos_base_ref[...]
    sin = tabs[:, LANES:]
    sgn_up = sgn_up_ref[...]
    sgn_dn = sgn_dn_ref[...]
    half = ROPE_DIMS // 2

    def rope(t):
        cols = []
        for j in range(width // LANES):
            tj = t[:, j * LANES:(j + 1) * LANES]
            up = pltpu.roll(tj, LANES - half, 1)
            dn = pltpu.roll(tj, half, 1)
            cols.append(tj * cos + sin * (up * sgn_up + dn * sgn_dn))
        return jnp.concatenate(cols, axis=1)

    def proj(g):
        return jnp.dot(hn, w_ref[:, g * width:(g + 1) * width], preferred_element_type=F32)

    scale = ATTN_HEAD_DIM ** -0.5 * LOG2_E

    def emit_q():
        q_ref[...] = (rope(proj(0)) * scale).astype(BF16)

    def emit_k():
        k_ref[...] = rope(proj(1)).astype(BF16)

    def emit_v():
        v_ref[...] = proj(2).astype(BF16)

    def emit_ag():
        ag_ref[...] = _silu(proj(3)).astype(BF16)

    hf = proj(5)
    hq = _silu(proj(4))
    hi = proj(6).astype(BF16)
    gate = _silu(proj(7))
    yield
    yield from _hgrn_tile(hq, hf, hi, gate, lbraw_ref, hnw_ref, state_ref, yh_ref, layer,
                          (emit_q, emit_k, emit_v, emit_ag))


def _rope_tables():
    half = ROPE_DIMS // 2
    inv_freq = ROPE_THETA ** (-np.arange(half, dtype=np.float32) * (2.0 / ROPE_DIMS))
    d = np.arange(LANES) % ATTN_HEAD_DIM
    rotary = d < ROPE_DIMS
    lane_uses = (np.arange(half)[:, None] == (d % half)[None, :]) & rotary[None, :]
    spread = np.zeros((4 * half, 2 * LANES), np.float32)
    for t in range(4):
        table = t // 2
        spread[t * half:(t + 1) * half, table * LANES:(table + 1) * LANES] = lane_uses
    cos_base = np.where(rotary, 0.0, 1.0).astype(np.float32)
    sgn_up = np.where(d < half, -1.0, 0.0).astype(np.float32)
    sgn_dn = np.where((d >= half) & rotary, 1.0, 0.0).astype(np.float32)
    return (inv_freq.astype(np.float32)[:, None], jnp.asarray(spread, BF16), cos_base[None],
            sgn_up[None], sgn_dn[None])


def _in_proj(x2, positions, norm_w, w, lb_raw, hgrn_norm_w, width, layer, seq):
    n, d_model = x2.shape
    rows = PROJ_ROWS * PROJ_TILES
    pos3 = positions.reshape(n // rows, 1, rows)
    consts = _rope_tables()
    row_spec = lambda c: pl.BlockSpec((rows, c), lambda i: (i, 0))
    full_spec = lambda a: pl.BlockSpec(a.shape, lambda i: (0,) * a.ndim)
    pos_spec = pl.BlockSpec((None, 1, rows), lambda i: (i, 0, 0))
    out_sds = jax.ShapeDtypeStruct((n, width), BF16)
    n_out = 5
    small = [*consts, norm_w, lb_raw, hgrn_norm_w]
    return pl.pallas_call(
        functools.partial(_in_proj_kernel, width=width, layer=layer, steps_per_seq=seq // rows),
        out_shape=[out_sds] * n_out,
        grid=(n // rows,),
        in_specs=[row_spec(d_model), pos_spec] + [full_spec(a) for a in small] + [
            pl.BlockSpec(w.shape, lambda i: (0, 0), pipeline_mode=pl.Buffered(1))],
        out_specs=[row_spec(width)] * n_out,
        scratch_shapes=[pltpu.VMEM((width // HGRN_HEAD_DIM, HGRN_HEAD_DIM, HGRN_HEAD_DIM), F32)],
        compiler_params=pltpu.CompilerParams(
            dimension_semantics=("arbitrary",), vmem_limit_bytes=VMEM_LIMIT),
        name="in_proj",
    )(x2, pos3, *small, w)


def _attn_kernel(q_ref, k_ref, v_ref, gate_ref, bias_ref, bias0_ref, seg_ref, nw_ref, o_ref,
                 qf, kf, vf, q4, k4, v4, o_pat, m_pat, l_pat, *, seq):
    blk = ATTN_BLOCK
    dint = ATTN_DEINTERLEAVE
    len4 = seq // dint
    pad4 = blk * max(d for _, d in DILATED_PATTERNS) // dint
    seg4 = pad4 + len4

    qf[...] = q_ref[...].astype(F32)
    for src, dst, dst4 in ((k_ref, kf, k4), (v_ref, vf, v4)):
        dst[0:blk, :] = jnp.zeros((blk, LANES), F32)
        dst[blk:blk + seq, :] = src[...].astype(F32)
        for r in range(dint):
            dst4[r * seg4:r * seg4 + pad4, :] = jnp.zeros((pad4, LANES), F32)
            dst4[r * seg4 + pad4:(r + 1) * seg4, :] = dst[pl.ds(blk + r, len4, stride=dint), :]
    for r in range(dint):
        q4[r * len4:(r + 1) * len4, :] = qf[pl.ds(r, len4, stride=dint), :]

    lane = lax.broadcasted_iota(jnp.int32, (1, LANES), 1)
    head_a = lane < ATTN_HEAD_DIM
    head_b = jnp.logical_not(head_a)

    def rows_of(start, size, stride):
        if stride == 1:
            return pl.ds(pl.multiple_of(start, blk), size)
        return pl.ds(start, size, stride=stride)

    for pat, (window, dil) in enumerate(DILATED_PATTERNS):
        assert window // dil == blk and (dil == 1 or dil % dint == 0)
        n_blk = seq // dil // blk
        n_run = min(ATTN_RUN, n_blk)
        n_res = min(ATTN_CHAINS // n_run, dil)
        runs_per_res = n_blk // n_run
        stride = 1 if dil == 1 else dil // dint
        q_src, k_src, v_src = (qf, kf, vf) if dil == 1 else (q4, k4, v4)

        def body(i, carry, pat=pat, dil=dil, n_run=n_run, n_res=n_res,
                 runs_per_res=runs_per_res, stride=stride, q_src=q_src, k_src=k_src, v_src=v_src):
            for j in range(n_res):
                r = (i // runs_per_res) * n_res + j
                n0 = (i % runs_per_res) * n_run
                if dil == 1:
                    q_start = blk * n0
                    k_start = blk + blk * (n0 - 1)
                else:
                    seg, off = r % dint, r // dint
                    q_start = seg * len4 + off + stride * blk * n0
                    k_start = seg * seg4 + pad4 + off + stride * blk * (n0 - 1)
                k_rows = rows_of(k_start, (n_run + 1) * blk, stride)
                kb = k_src[k_rows, :]
                k_a = jnp.where(head_a, kb, 0.0).astype(BF16)
                k_b = jnp.where(head_b, kb, 0.0).astype(BF16)
                vb = v_src[k_rows, :]
                v_a = jnp.where(head_a, vb, 1.0).astype(BF16)
                v_b = jnp.where(head_b, vb, 1.0).astype(BF16)
                qb = q_src[rows_of(q_start, n_run * blk, stride), :].astype(BF16)
                for u in range(n_run):
                    bias = bias_ref[...]
                    if u == 0:
                        bias = jnp.where(n0 == 0, bias0_ref[...], bias)
                    q_u = qb[u * blk:(u + 1) * blk]
                    keys = slice(u * blk, (u + 2) * blk)

                    def one_head(k_h, v_h):
                        s = lax.dot_general(q_u, k_h[keys], (((1,), (1,)), ((), ())),
                                            preferred_element_type=F32) + bias
                        m = jnp.max(s, axis=-1, keepdims=True)
                        p = jnp.exp2(s - m).astype(BF16)
                        return jnp.dot(p, v_h[keys], preferred_element_type=F32), m

                    ol_a, m_a = one_head(k_a, v_a)
                    ol_b, m_b = one_head(k_b, v_b)
                    rows = rows_of(q_start + u * stride * blk, blk, stride)
                    o_pat[pat, rows, :] = jnp.where(head_a, ol_a, ol_b)
                    m_pat[pat, rows, :] = jnp.where(head_a, m_a, m_b)
                    l_pat[pat, rows, :] = jnp.where(head_a, ol_b, ol_a)
            return carry

        lax.fori_loop(0, dil * n_blk // (n_run * n_res), body, 0)

    blocks_per_seg = len4 // blk

    def merge(i, carry):
        seg, c = i // blocks_per_seg, i % blocks_per_seg
        rows4 = pl.ds(pl.multiple_of(seg * len4 + c * blk, blk), blk)
        rows1 = pl.ds(seg + dint * blk * c, blk, stride=dint)
        rows = [rows1 if dil == 1 else rows4 for _, dil in DILATED_PATTERNS]
        ms = [m_pat[p, rw, :] for p, rw in enumerate(rows)]
        m_top = functools.reduce(jnp.maximum, ms)
        ws = [jnp.exp2(m - m_top) for m in ms]
        num = sum(w * o_pat[p, rw, :] for p, (w, rw) in enumerate(zip(ws, rows)))
        den = sum(w * pltpu.roll(l_pat[p, rw, :], ATTN_HEAD_DIM, 1)
                  for p, (w, rw) in enumerate(zip(ws, rows)))
        ssq = jnp.dot((num * num).astype(BF16), seg_ref[...], preferred_element_type=F32)
        out = num * lax.rsqrt(ssq * (1.0 / ATTN_HEAD_DIM) + NORM_EPS * (den * den))
        qf[rows1, :] = out
        return carry

    assert DILATED_PATTERNS[0][1] == 1
    lax.fori_loop(0, seq // blk, merge, 0, unroll=16)
    o_ref[...] = ((qf[...] * nw_ref[...]) * gate_ref[...].astype(F32)).astype(BF16)


def _band_bias():
    qi = np.arange(ATTN_BLOCK)[:, None]
    kj = np.arange(2 * ATTN_BLOCK)[None, :]
    dist = ATTN_BLOCK + qi - kj
    band = (dist >= 0) & (dist <= ATTN_BLOCK)
    first = band & (kj >= ATTN_BLOCK)
    to_bias = lambda mk: np.where(mk, 0.0, -np.inf).astype(np.float32)
    return to_bias(band), to_bias(first)


def _attention(q, k, v, gate, norm_w):
    b, seq, width = q.shape
    bias, bias0 = _band_bias()
    n_pat = len(DILATED_PATTERNS)
    pad4 = ATTN_BLOCK * max(d for _, d in DILATED_PATTERNS) // ATTN_DEINTERLEAVE
    rows4 = seq + ATTN_DEINTERLEAVE * pad4
    head_of_lane = np.arange(LANES) // ATTN_HEAD_DIM
    seg = jnp.asarray((head_of_lane[:, None] == head_of_lane[None, :]).astype(np.float32), BF16)
    col_spec = pl.BlockSpec((None, seq, LANES), lambda i, j: (i, 0, j))
    bias_spec = pl.BlockSpec(bias.shape, lambda i, j: (0, 0))
    seg_spec = pl.BlockSpec(seg.shape, lambda i, j: (0, 0))
    nw_spec = pl.BlockSpec((1, LANES), lambda i, j: (0, j))
    return pl.pallas_call(
        functools.partial(_attn_kernel, seq=seq),
        out_shape=jax.ShapeDtypeStruct((b, seq, width), BF16),
        grid=(b, width // LANES),
        in_specs=[col_spec, col_spec, col_spec, col_spec, bias_spec, bias_spec, seg_spec, nw_spec],
        out_specs=col_spec,
        scratch_shapes=[pltpu.VMEM((seq, LANES), F32),
                        pltpu.VMEM((ATTN_BLOCK + seq, LANES), F32),
                        pltpu.VMEM((ATTN_BLOCK + seq, LANES), F32),
                        pltpu.VMEM((seq, LANES), F32),
                        pltpu.VMEM((rows4, LANES), F32),
                        pltpu.VMEM((rows4, LANES), F32),
                        pltpu.VMEM((n_pat, seq, LANES), F32),
                        pltpu.VMEM((n_pat, seq, LANES), F32),
                        pltpu.VMEM((n_pat, seq, LANES), F32)],
        compiler_params=pltpu.CompilerParams(
            dimension_semantics=("parallel", "parallel"), vmem_limit_bytes=VMEM_LIMIT),
        name="dilated_attn",
    )(q, k, v, gate, bias, bias0, seg, norm_w)


def _out_proj_kernel(x_ref, ya_ref, yh_ref, fnw_ref, w_ref, out_ref, *, attn_width):
    mixed = jnp.dot(ya_ref[...], w_ref[0:attn_width, :].astype(BF16),
                    preferred_element_type=F32)
    mixed = mixed + jnp.dot(yh_ref[...], w_ref[attn_width:, :].astype(BF16),
                            preferred_element_type=F32)
    x = x_ref[...] + mixed
    ms = jnp.mean(x * x, axis=-1, keepdims=True)
    out_ref[...] = (x * lax.rsqrt(ms + NORM_EPS)) * fnw_ref[...]


def _out_proj(x2, ya, yh, fnw, w):
    n, d_model = x2.shape
    attn_width = ya.shape[-1]
    rows = OUT_ROWS
    row_spec = lambda c: pl.BlockSpec((rows, c), lambda i: (i, 0))
    full_spec = lambda a: pl.BlockSpec(a.shape, lambda i: (0,) * a.ndim)
    return pl.pallas_call(
        functools.partial(_out_proj_kernel, attn_width=attn_width),
        out_shape=jax.ShapeDtypeStruct((n, d_model), F32),
        grid=(n // rows,),
        in_specs=[row_spec(d_model), row_spec(attn_width), row_spec(yh.shape[-1]),
                  full_spec(fnw), full_spec(w)],
        out_specs=row_spec(d_model),
        compiler_params=pltpu.CompilerParams(
            dimension_semantics=("parallel",), vmem_limit_bytes=VMEM_LIMIT),
        name="out_proj",
    )(x2, ya, yh, fnw, w)


def kernel(x, positions, w_in, w_out, mix_norm_w, attn_out_norm_w, hgrn_out_norm_w,
           hgrn_lb_raw, final_norm_w):
    b, seq, d_model = x.shape
    depth = w_in.shape[0]
    attn_width = attn_out_norm_w.shape[-1]
    hgrn_width = hgrn_out_norm_w.shape[-1]
    assert depth == 1 and attn_width == hgrn_width and w_in.shape[-1] == 8 * attn_width
    assert seq % (max(d for _, d in DILATED_PATTERNS) * ATTN_BLOCK) == 0
    n = b * seq
    layer = 0
    x2 = x.reshape(n, d_model)
    q, k, v, ag, yh = _in_proj(
        x2, positions, mix_norm_w[layer][None], w_in[layer].astype(BF16), hgrn_lb_raw,
        hgrn_out_norm_w[layer][None], attn_width, layer, seq)
    to3 = lambda t: t.reshape(b, seq, t.shape[-1])
    ya = _attention(to3(q), to3(k), to3(v), to3(ag), attn_out_norm_w[layer][None])
    out = _out_proj(x2, ya.reshape(n, attn_width), yh, final_norm_w[None],
                    w_out[layer])
    return out.reshape(b, seq, d_model)
```

```python
import functools

import numpy as np
import jax
import jax.numpy as jnp
from jax import lax
from jax.experimental import pallas as pl
from jax.experimental.pallas import tpu as pltpu

F32 = jnp.float32
BF16 = jnp.bfloat16

LANES = 128
SUBLANES = 8
ATTN_HEAD_DIM = 64
HGRN_HEAD_DIM = 128
DILATED_PATTERNS = ((128, 1), (512, 4), (2048, 16))
ATTN_BLOCK = 128
ATTN_DEINTERLEAVE = 4
ROPE_THETA = 500000.0
ROPE_DIMS = ATTN_HEAD_DIM // 4
HGRN_CHUNK = 64
NORM_EPS = 1e-6
LOG2_E = 1.4426950408889634
VMEM_LIMIT = 56 * 1024 * 1024

PROJ_ROWS = 512
PROJ_TILES = 2
PROJ_STAGES = 6
PROJ_SKEW = 2
OUT_ROWS = 2048
ATTN_RUN = 32
ATTN_CHAINS = 32


def _silu(t):
    return t * jax.nn.sigmoid(t)


def _cumsum_rows(t):
    n_rows, width = t.shape
    row = lax.broadcasted_iota(jnp.int32, (n_rows, width), 0)
    shift = 1
    while shift < n_rows:
        if shift < SUBLANES:
            prev = jnp.where(row >= shift, pltpu.roll(t, shift, 0), 0.0)
        else:
            prev = jnp.concatenate([jnp.zeros((shift, width), t.dtype), t[:n_rows - shift]], axis=0)
        t = t + prev
        shift *= 2
    return t


def _hgrn_tile(hq, hf, hi, gate, lbraw_ref, norm_w_ref, state_ref, o_ref, layer, interleave):
    chunk = HGRN_CHUNK
    rows, width = hq.shape
    n_chunks = rows // chunk
    n_heads = width // HGRN_HEAD_DIM
    heads = [slice(h * HGRN_HEAD_DIM, (h + 1) * HGRN_HEAD_DIM) for h in range(n_heads)]
    chunks = [slice(c * chunk, (c + 1) * chunk) for c in range(n_chunks)]

    raw = lbraw_ref[...]
    e = jnp.exp(raw - jnp.max(raw, axis=0, keepdims=True))
    sm = e / jnp.sum(e, axis=0, keepdims=True)
    lb = jnp.sum(sm[0:layer + 1, :], axis=0, keepdims=True)

    ti = lax.broadcasted_iota(jnp.int32, (chunk, chunk), 0)
    si = lax.broadcasted_iota(jnp.int32, (chunk, chunk), 1)
    causal = ti >= si

    hkey, cum = [], []
    for rws in chunks:
        f = lb + (1.0 - lb) * jax.nn.sigmoid(hf[rws, :])
        hkey.append(1.0 - f)
        cum.append(_cumsum_rows(jnp.log(f)))
    for fill in interleave[0]:
        fill()
    yield

    q_dec, decay, att, upd = [], [], [], []
    for c, rws in enumerate(chunks):
        dec_c = jnp.exp(cum[c][chunk - 1:chunk, :])
        q_c = (hq[rws, :] * jnp.exp(cum[c])).astype(BF16)
        k_inv = hkey[c] * jnp.exp(-cum[c])
        k_end = (k_inv * dec_c).astype(BF16)
        k_inv = k_inv.astype(BF16)
        att.append([lax.dot_general(q_c[:, cs], k_inv[:, cs], (((1,), (1,)), ((), ())),
                                    preferred_element_type=F32) for cs in heads])
        upd.append([lax.dot_general(hi[rws, cs], k_end[:, cs], (((0,), (0,)), ((), ())),
                                    preferred_element_type=F32) for cs in heads])
        q_dec.append(q_c)
        decay.append(dec_c)
    for fill in interleave[1]:
        fill()
    yield

    o_intra = [[jnp.dot(jnp.where(causal, att[c][h], 0.0).astype(BF16), hi[rws, cs],
                        preferred_element_type=F32) for h, cs in enumerate(heads)]
               for c, rws in enumerate(chunks)]
    for fill in interleave[2]:
        fill()
    yield

    for h, cs in enumerate(heads):
        if h > 0:
            for fill in interleave[2 + h]:
                fill()
        if h == n_heads // 2:
            yield
        st = state_ref[h]
        for c, rws in enumerate(chunks):
            o = o_intra[c][h] + lax.dot_general(
                q_dec[c][:, cs], st.astype(BF16), (((1,), (1,)), ((), ())),
                preferred_element_type=F32)
            ms = jnp.mean(o * o, axis=-1, keepdims=True)
            y = (o * lax.rsqrt(ms + NORM_EPS)) * norm_w_ref[:, cs] * gate[rws, cs]
            o_ref[rws, cs] = y.astype(o_ref.dtype)
            st = decay[c][:, cs] * st + upd[c][h]
        state_ref[h] = st


def _in_proj_kernel(x_ref, pos_ref, freq_ref, spread_ref, cos_base_ref, sgn_up_ref, sgn_dn_ref,
                    nw_ref, lbraw_ref, hnw_ref, w_ref,
                    q_ref, k_ref, v_ref, ag_ref, yh_ref, state_ref,
                    *, width, layer, steps_per_seq):
    @pl.when(pl.program_id(0) % steps_per_seq == 0)
    def _():
        state_ref[...] = jnp.zeros(state_ref.shape, F32)

    tiles = []
    for t in range(PROJ_TILES):
        rows = pl.ds(t * PROJ_ROWS, PROJ_ROWS)
        tiles.append(_in_proj_tile(
            x_ref.at[rows], pos_ref[:, t * PROJ_ROWS:(t + 1) * PROJ_ROWS],
            freq_ref, spread_ref, cos_base_ref, sgn_up_ref, sgn_dn_ref,
            nw_ref, lbraw_ref, hnw_ref, w_ref,
            q_ref.at[rows], k_ref.at[rows], v_ref.at[rows], ag_ref.at[rows],
            yh_ref.at[rows], state_ref, width, layer, last=t == PROJ_TILES - 1))
    for step in range(PROJ_STAGES + PROJ_SKEW * (PROJ_TILES - 1)):
        for t, tile in enumerate(tiles):
            if 0 <= step - PROJ_SKEW * t < PROJ_STAGES:
                next(tile, None)
    done = object()
    assert all(next(tile, done) is done for tile in tiles)


def _in_proj_tile(x_ref, pos, freq_ref, spread_ref, cos_base_ref, sgn_up_ref, sgn_dn_ref,
                  nw_ref, lbraw_ref, hnw_ref, w_ref,
                  q_ref, k_ref, v_ref, ag_ref, yh_ref, state_ref, width, layer, last):
    x = x_ref[...]
    ms = jnp.mean(x * x, axis=-1, keepdims=True)
    hn = ((x * lax.rsqrt(ms + NORM_EPS)) * nw_ref[...]).astype(BF16)

    ang = freq_ref[...] * pos.astype(F32)
    parts = []
    for t in (jnp.cos(ang), jnp.sin(ang)):
        hi = t.astype(BF16).astype(F32)
        parts += [hi, t - hi]
    tabs = lax.dot_general(jnp.concatenate(parts, axis=0).astype(BF16), spread_ref[...],
                           (((0,), (0,)), ((), ())), preferred_element_type=F32)
    cos = tabs[:, :LANES] + cos_base_ref[...]
    sin = tabs[:, LANES:]
    sgn_up = sgn_up_ref[...]
    sgn_dn = sgn_dn_ref[...]
    half = ROPE_DIMS // 2

    def rope(t):
        cols = []
        for j in range(t.shape[1] // LANES):
            tj = t[:, j * LANES:(j + 1) * LANES]
            up = pltpu.roll(tj, LANES - half, 1)
            dn = pltpu.roll(tj, half, 1)
            cols.append(tj * cos + sin * (up * sgn_up + dn * sgn_dn))
        return jnp.concatenate(cols, axis=1)

    def proj(g, part=None):
        lo, size = g * width, width
        if part is not None:
            lo, size = lo + part * (width // 2), width // 2
        return jnp.dot(hn, w_ref[:, lo:lo + size], preferred_element_type=F32)

    scale = ATTN_HEAD_DIM ** -0.5 * LOG2_E

    def emit(ref, g, post):
        def half(part):
            def fill():
                cols = slice(part * (width // 2), (part + 1) * (width // 2))
                ref[:, cols] = post(proj(g, part)).astype(BF16)
            return fill
        return half(0), half(1)

    q_lo, q_hi = emit(q_ref, 0, lambda t: rope(t) * scale)
    k_lo, k_hi = emit(k_ref, 1, rope)
    v_lo, v_hi = emit(v_ref, 2, lambda t: t)
    g_lo, g_hi = emit(ag_ref, 3, _silu)
    if last:
        fills = [[q_lo, q_hi], [k_lo, k_hi], [v_lo], [v_hi], [g_lo], [g_hi]]
    else:
        fills = [[q_lo, q_hi], [k_lo, k_hi], [v_lo, v_hi], [], [g_lo, g_hi], []]

    hf = proj(5)
    hq = _silu(proj(4))
    hi = proj(6).astype(BF16)
    gate = _silu(proj(7))
    yield
    yield from _hgrn_tile(hq, hf, hi, gate, lbraw_ref, hnw_ref, state_ref, yh_ref, layer, fills)


def _rope_tables():
    half = ROPE_DIMS // 2
    inv_freq = ROPE_THETA ** (-np.arange(half, dtype=np.float32) * (2.0 / ROPE_DIMS))
    d = np.arange(LANES) % ATTN_HEAD_DIM
    rotary = d < ROPE_DIMS
    lane_uses = (np.arange(half)[:, None] == (d % half)[None, :]) & rotary[None, :]
    spread = np.zeros((4 * half, 2 * LANES), np.float32)
    for t in range(4):
        table = t // 2
        spread[t * half:(t + 1) * half, table * LANES:(table + 1) * LANES] = lane_uses
    cos_base = np.where(rotary, 0.0, 1.0).astype(np.float32)
    sgn_up = np.where(d < half, -1.0, 0.0).astype(np.float32)
    sgn_dn = np.where((d >= half) & rotary, 1.0, 0.0).astype(np.float32)
    return (inv_freq.astype(np.float32)[:, None], jnp.asarray(spread, BF16), cos_base[None],
            sgn_up[None], sgn_dn[None])


def _in_proj(x2, positions, norm_w, w, lb_raw, hgrn_norm_w, width, layer, seq):
    n, d_model = x2.shape
    rows = PROJ_ROWS * PROJ_TILES
    pos3 = positions.reshape(n // rows, 1, rows)
    consts = _rope_tables()
    row_spec = lambda c: pl.BlockSpec((rows, c), lambda i: (i, 0))
    full_spec = lambda a: pl.BlockSpec(a.shape, lambda i: (0,) * a.ndim)
    pos_spec = pl.BlockSpec((None, 1, rows), lambda i: (i, 0, 0))
    out_sds = jax.ShapeDtypeStruct((n, width), BF16)
    n_out = 5
    small = [*consts, norm_w, lb_raw, hgrn_norm_w]
    return pl.pallas_call(
        functools.partial(_in_proj_kernel, width=width, layer=layer, steps_per_seq=seq // rows),
        out_shape=[out_sds] * n_out,
        grid=(n // rows,),
        in_specs=[row_spec(d_model), pos_spec] + [full_spec(a) for a in small] + [
            pl.BlockSpec(w.shape, lambda i: (0, 0), pipeline_mode=pl.Buffered(1))],
        out_specs=[row_spec(width)] * n_out,
        scratch_shapes=[pltpu.VMEM((width // HGRN_HEAD_DIM, HGRN_HEAD_DIM, HGRN_HEAD_DIM), F32)],
        compiler_params=pltpu.CompilerParams(
            dimension_semantics=("arbitrary",), vmem_limit_bytes=VMEM_LIMIT),
        name="in_proj",
    )(x2, pos3, *small, w)


def _attn_kernel(q_ref, k_ref, v_ref, gate_ref, bias_ref, bias0_ref, seg_ref, nw_ref, o_ref,
                 qf, kf, vf, q4, k4, v4, o_pat, m_pat, l_pat, *, seq):
    blk = ATTN_BLOCK
    dint = ATTN_DEINTERLEAVE
    len4 = seq // dint
    pad4 = blk * max(d for _, d in DILATED_PATTERNS) // dint
    seg4 = pad4 + len4

    qf[...] = q_ref[...].astype(F32)
    for src, dst, dst4 in ((k_ref, kf, k4), (v_ref, vf, v4)):
        dst[0:blk, :] = jnp.zeros((blk, LANES), F32)
        dst[blk:blk + seq, :] = src[...].astype(F32)
        for r in range(dint):
            dst4[r * seg4:r * seg4 + pad4, :] = jnp.zeros((pad4, LANES), F32)
            dst4[r * seg4 + pad4:(r + 1) * seg4, :] = dst[pl.ds(blk + r, len4, stride=dint), :]
    for r in range(dint):
        q4[r * len4:(r + 1) * len4, :] = qf[pl.ds(r, len4, stride=dint), :]

    lane = lax.broadcasted_iota(jnp.int32, (1, LANES), 1)
    head_a = lane < ATTN_HEAD_DIM
    head_b = jnp.logical_not(head_a)

    def rows_of(start, size, stride):
        if stride == 1:
            return pl.ds(pl.multiple_of(start, blk), size)
        return pl.ds(start, size, stride=stride)

    for pat, (window, dil) in enumerate(DILATED_PATTERNS):
        assert window // dil == blk and (dil == 1 or dil % dint == 0)
        n_blk = seq // dil // blk
        n_run = min(ATTN_RUN, n_blk)
        n_res = min(ATTN_CHAINS // n_run, dil)
        runs_per_res = n_blk // n_run
        stride = 1 if dil == 1 else dil // dint
        q_src, k_src, v_src = (qf, kf, vf) if dil == 1 else (q4, k4, v4)

        def body(i, carry, pat=pat, dil=dil, n_run=n_run, n_res=n_res,
                 runs_per_res=runs_per_res, stride=stride, q_src=q_src, k_src=k_src, v_src=v_src):
            for j in range(n_res):
                r = (i // runs_per_res) * n_res + j
                n0 = (i % runs_per_res) * n_run
                if dil == 1:
                    q_start = blk * n0
                    k_start = blk + blk * (n0 - 1)
                else:
                    seg, off = r % dint, r // dint
                    q_start = seg * len4 + off + stride * blk * n0
                    k_start = seg * seg4 + pad4 + off + stride * blk * (n0 - 1)
                k_rows = rows_of(k_start, (n_run + 1) * blk, stride)
                kb = k_src[k_rows, :]
                k_a = jnp.where(head_a, kb, 0.0).astype(BF16)
                k_b = jnp.where(head_b, kb, 0.0).astype(BF16)
                vb = v_src[k_rows, :]
                v_a = jnp.where(head_a, vb, 1.0).astype(BF16)
                v_b = jnp.where(head_b, vb, 1.0).astype(BF16)
                qb = q_src[rows_of(q_start, n_run * blk, stride), :].astype(BF16)
                for u in range(n_run):
                    bias = bias_ref[...]
                    if u == 0:
                        bias = jnp.where(n0 == 0, bias0_ref[...], bias)
                    q_u = qb[u * blk:(u + 1) * blk]
                    keys = slice(u * blk, (u + 2) * blk)

                    def one_head(k_h, v_h):
                        s = lax.dot_general(q_u, k_h[keys], (((1,), (1,)), ((), ())),
                                            preferred_element_type=F32) + bias
                        m = jnp.max(s, axis=-1, keepdims=True)
                        p = jnp.exp2(s - m).astype(BF16)
                        return jnp.dot(p, v_h[keys], preferred_element_type=F32), m

                    ol_a, m_a = one_head(k_a, v_a)
                    ol_b, m_b = one_head(k_b, v_b)
                    rows = rows_of(q_start + u * stride * blk, blk, stride)
                    o_pat[pat, rows, :] = jnp.where(head_a, ol_a, ol_b)
                    m_pat[pat, rows, :] = jnp.where(head_a, m_a, m_b)
                    l_pat[pat, rows, :] = jnp.where(head_a, ol_b, ol_a)
            return carry

        lax.fori_loop(0, dil * n_blk // (n_run * n_res), body, 0)

    blocks_per_seg = len4 // blk

    def merge(i, carry):
        seg, c = i // blocks_per_seg, i % blocks_per_seg
        rows4 = pl.ds(pl.multiple_of(seg * len4 + c * blk, blk), blk)
        rows1 = pl.ds(seg + dint * blk * c, blk, stride=dint)
        rows = [rows1 if dil == 1 else rows4 for _, dil in DILATED_PATTERNS]
        ms = [m_pat[p, rw, :] for p, rw in enumerate(rows)]
        m_top = functools.reduce(jnp.maximum, ms)
        ws = [jnp.exp2(m - m_top) for m in ms]
        num = sum(w * o_pat[p, rw, :] for p, (w, rw) in enumerate(zip(ws, rows)))
        den = sum(w * pltpu.roll(l_pat[p, rw, :], ATTN_HEAD_DIM, 1)
                  for p, (w, rw) in enumerate(zip(ws, rows)))
        ssq = jnp.dot((num * num).astype(BF16), seg_ref[...], preferred_element_type=F32)
        out = num * lax.rsqrt(ssq * (1.0 / ATTN_HEAD_DIM) + NORM_EPS * (den * den))
        qf[rows1, :] = out
        return carry

    assert DILATED_PATTERNS[0][1] == 1
    lax.fori_loop(0, seq // blk, merge, 0, unroll=16)
    o_ref[...] = ((qf[...] * nw_ref[...]) * gate_ref[...].astype(F32)).astype(BF16)


def _band_bias():
    qi = np.arange(ATTN_BLOCK)[:, None]
    kj = np.arange(2 * ATTN_BLOCK)[None, :]
    dist = ATTN_BLOCK + qi - kj
    band = (dist >= 0) & (dist <= ATTN_BLOCK)
    first = band & (kj >= ATTN_BLOCK)
    to_bias = lambda mk: np.where(mk, 0.0, -np.inf).astype(np.float32)
    return to_bias(band), to_bias(first)


def _attention(q, k, v, gate, norm_w):
    b, seq, width = q.shape
    bias, bias0 = _band_bias()
    n_pat = len(DILATED_PATTERNS)
    pad4 = ATTN_BLOCK * max(d for _, d in DILATED_PATTERNS) // ATTN_DEINTERLEAVE
    rows4 = seq + ATTN_DEINTERLEAVE * pad4
    head_of_lane = np.arange(LANES) // ATTN_HEAD_DIM
    seg = jnp.asarray((head_of_lane[:, None] == head_of_lane[None, :]).astype(np.float32), BF16)
    col_spec = pl.BlockSpec((None, seq, LANES), lambda i, j: (i, 0, j))
    bias_spec = pl.BlockSpec(bias.shape, lambda i, j: (0, 0))
    seg_spec = pl.BlockSpec(seg.shape, lambda i, j: (0, 0))
    nw_spec = pl.BlockSpec((1, LANES), lambda i, j: (0, j))
    return pl.pallas_call(
        functools.partial(_attn_kernel, seq=seq),
        out_shape=jax.ShapeDtypeStruct((b, seq, width), BF16),
        grid=(b, width // LANES),
        in_specs=[col_spec, col_spec, col_spec, col_spec, bias_spec, bias_spec, seg_spec, nw_spec],
        out_specs=col_spec,
        scratch_shapes=[pltpu.VMEM((seq, LANES), F32),
                        pltpu.VMEM((ATTN_BLOCK + seq, LANES), F32),
                        pltpu.VMEM((ATTN_BLOCK + seq, LANES), F32),
                        pltpu.VMEM((seq, LANES), F32),
                        pltpu.VMEM((rows4, LANES), F32),
                        pltpu.VMEM((rows4, LANES), F32),
                        pltpu.VMEM((n_pat, seq, LANES), F32),
                        pltpu.VMEM((n_pat, seq, LANES), F32),
                        pltpu.VMEM((n_pat, seq, LANES), F32)],
        compiler_params=pltpu.CompilerParams(
            dimension_semantics=("parallel", "parallel"), vmem_limit_bytes=VMEM_LIMIT),
        name="dilated_attn",
    )(q, k, v, gate, bias, bias0, seg, norm_w)


def _out_proj_kernel(x_ref, ya_ref, yh_ref, fnw_ref, w_ref, out_ref, *, attn_width):
    mixed = jnp.dot(ya_ref[...], w_ref[0:attn_width, :].astype(BF16),
                    preferred_element_type=F32)
    mixed = mixed + jnp.dot(yh_ref[...], w_ref[attn_width:, :].astype(BF16),
                            preferred_element_type=F32)
    x = x_ref[...] + mixed
    ms = jnp.mean(x * x, axis=-1, keepdims=True)
    out_ref[...] = (x * lax.rsqrt(ms + NORM_EPS)) * fnw_ref[...]


def _out_proj(x2, ya, yh, fnw, w):
    n, d_model = x2.shape
    attn_width = ya.shape[-1]
    rows = OUT_ROWS
    row_spec = lambda c: pl.BlockSpec((rows, c), lambda i: (i, 0))
    full_spec = lambda a: pl.BlockSpec(a.shape, lambda i: (0,) * a.ndim)
    return pl.pallas_call(
        functools.partial(_out_proj_kernel, attn_width=attn_width),
        out_shape=jax.ShapeDtypeStruct((n, d_model), F32),
        grid=(n // rows,),
        in_specs=[row_spec(d_model), row_spec(attn_width), row_spec(yh.shape[-1]),
                  full_spec(fnw), full_spec(w)],
        out_specs=row_spec(d_model),
        compiler_params=pltpu.CompilerParams(
            dimension_semantics=("parallel",), vmem_limit_bytes=VMEM_LIMIT),
        name="out_proj",
    )(x2, ya, yh, fnw, w)


def kernel(x, positions, w_in, w_out, mix_norm_w, attn_out_norm_w, hgrn_out_norm_w,
           hgrn_lb_raw, final_norm_w):
    b, seq, d_model = x.shape
    depth = w_in.shape[0]
    attn_width = attn_out_norm_w.shape[-1]
    hgrn_width = hgrn_out_norm_w.shape[-1]
    assert depth == 1 and attn_width == hgrn_width and w_in.shape[-1] == 8 * attn_width
    assert seq % (max(d for _, d in DILATED_PATTERNS) * ATTN_BLOCK) == 0
    n = b * seq
    layer = 0
    x2 = x.reshape(n, d_model)
    q, k, v, ag, yh = _in_proj(
        x2, positions, mix_norm_w[layer][None], w_in[layer].astype(BF16), hgrn_lb_raw,
        hgrn_out_norm_w[layer][None], attn_width, layer, seq)
    to3 = lambda t: t.reshape(b, seq, t.shape[-1])
    ya = _attention(to3(q), to3(k), to3(v), to3(ag), attn_out_norm_w[layer][None])
    out = _out_proj(x2, ya.reshape(n, attn_width), yh, final_norm_w[None],
                    w_out[layer])
    return out.reshape(b, seq, d_model)
```

```python
import functools

import numpy as np
import jax
import jax.numpy as jnp
from jax import lax
from jax.experimental import pallas as pl
from jax.experimental.pallas import tpu as pltpu

F32 = jnp.float32
BF16 = jnp.bfloat16

LANES = 128
SUBLANES = 8
ATTN_HEAD_DIM = 64
HGRN_HEAD_DIM = 128
DILATED_PATTERNS = ((128, 1), (512, 4), (2048, 16))
ATTN_BLOCK = 128
ATTN_DEINTERLEAVE = 4
ROPE_THETA = 500000.0
ROPE_DIMS = ATTN_HEAD_DIM // 4
HGRN_CHUNK = 64
NORM_EPS = 1e-6
LOG2_E = 1.4426950408889634
VMEM_LIMIT = 56 * 1024 * 1024

PROJ_ROWS = 256
PROJ_TILES = 4
PROJ_STAGES = 6
PROJ_SKEW = 2
OUT_ROWS = 2048
ATTN_RUN = 32
ATTN_CHAINS = 32


def _silu(t):
    return t * jax.nn.sigmoid(t)


def _cumsum_rows(t):
    n_rows, width = t.shape
    row = lax.broadcasted_iota(jnp.int32, (n_rows, width), 0)
    shift = 1
    while shift < n_rows:
        if shift < SUBLANES:
            prev = jnp.where(row >= shift, pltpu.roll(t, shift, 0), 0.0)
        else:
            prev = jnp.concatenate([jnp.zeros((shift, width), t.dtype), t[:n_rows - shift]], axis=0)
        t = t + prev
        shift *= 2
    return t


def _hgrn_tile(hq, hf, hi, gate, lbraw_ref, norm_w_ref, state_ref, o_ref, layer, interleave):
    chunk = HGRN_CHUNK
    rows, width = hq.shape
    n_chunks = rows // chunk
    n_heads = width // HGRN_HEAD_DIM
    heads = [slice(h * HGRN_HEAD_DIM, (h + 1) * HGRN_HEAD_DIM) for h in range(n_heads)]
    chunks = [slice(c * chunk, (c + 1) * chunk) for c in range(n_chunks)]

    raw = lbraw_ref[...]
    e = jnp.exp(raw - jnp.max(raw, axis=0, keepdims=True))
    sm = e / jnp.sum(e, axis=0, keepdims=True)
    lb = jnp.sum(sm[0:layer + 1, :], axis=0, keepdims=True)

    ti = lax.broadcasted_iota(jnp.int32, (chunk, chunk), 0)
    si = lax.broadcasted_iota(jnp.int32, (chunk, chunk), 1)
    causal = ti >= si

    hkey, cum = [], []
    for rws in chunks:
        f = lb + (1.0 - lb) * jax.nn.sigmoid(hf[rws, :])
        hkey.append(1.0 - f)
        cum.append(_cumsum_rows(jnp.log(f)))
    for fill in interleave[0]:
        fill()
    yield

    q_dec, decay, att, upd = [], [], [], []
    for c, rws in enumerate(chunks):
        dec_c = jnp.exp(cum[c][chunk - 1:chunk, :])
        q_c = (hq[rws, :] * jnp.exp(cum[c])).astype(BF16)
        k_inv = hkey[c] * jnp.exp(-cum[c])
        k_end = (k_inv * dec_c).astype(BF16)
        k_inv = k_inv.astype(BF16)
        att.append([lax.dot_general(q_c[:, cs], k_inv[:, cs], (((1,), (1,)), ((), ())),
                                    preferred_element_type=F32) for cs in heads])
        upd.append([lax.dot_general(hi[rws, cs], k_end[:, cs], (((0,), (0,)), ((), ())),
                                    preferred_element_type=F32) for cs in heads])
        q_dec.append(q_c)
        decay.append(dec_c)
    for fill in interleave[1]:
        fill()
    yield

    o_intra = [[jnp.dot(jnp.where(causal, att[c][h], 0.0).astype(BF16), hi[rws, cs],
                        preferred_element_type=F32) for h, cs in enumerate(heads)]
               for c, rws in enumerate(chunks)]
    for fill in interleave[2]:
        fill()
    yield

    for h, cs in enumerate(heads):
        if h > 0:
            for fill in interleave[2 + h]:
                fill()
        if h == n_heads // 2:
            yield
        st = state_ref[h]
        for c, rws in enumerate(chunks):
            o = o_intra[c][h] + lax.dot_general(
                q_dec[c][:, cs], st.astype(BF16), (((1,), (1,)), ((), ())),
                preferred_element_type=F32)
            ms = jnp.mean(o * o, axis=-1, keepdims=True)
            y = (o * lax.rsqrt(ms + NORM_EPS)) * norm_w_ref[:, cs] * gate[rws, cs]
            o_ref[rws, cs] = y.astype(o_ref.dtype)
            st = decay[c][:, cs] * st + upd[c][h]
        state_ref[h] = st


def _in_proj_kernel(x_ref, pos_ref, freq_ref, spread_ref, cos_base_ref, sgn_up_ref, sgn_dn_ref,
                    nw_ref, lbraw_ref, hnw_ref, w_ref,
                    q_ref, k_ref, v_ref, ag_ref, yh_ref, state_ref,
                    *, width, layer, steps_per_seq):
    @pl.when(pl.program_id(0) % steps_per_seq == 0)
    def _():
        state_ref[...] = jnp.zeros(state_ref.shape, F32)

    tiles = []
    for t in range(PROJ_TILES):
        rows = pl.ds(t * PROJ_ROWS, PROJ_ROWS)
        tiles.append(_in_proj_tile(
            x_ref.at[rows], pos_ref[:, t * PROJ_ROWS:(t + 1) * PROJ_ROWS],
            freq_ref, spread_ref, cos_base_ref, sgn_up_ref, sgn_dn_ref,
            nw_ref, lbraw_ref, hnw_ref, w_ref,
            q_ref.at[rows], k_ref.at[rows], v_ref.at[rows], ag_ref.at[rows],
            yh_ref.at[rows], state_ref, width, layer, last=t == PROJ_TILES - 1))
    for step in range(PROJ_STAGES + PROJ_SKEW * (PROJ_TILES - 1)):
        for t, tile in enumerate(tiles):
            if 0 <= step - PROJ_SKEW * t < PROJ_STAGES:
                next(tile, None)
    done = object()
    assert all(next(tile, done) is done for tile in tiles)


def _in_proj_tile(x_ref, pos, freq_ref, spread_ref, cos_base_ref, sgn_up_ref, sgn_dn_ref,
                  nw_ref, lbraw_ref, hnw_ref, w_ref,
                  q_ref, k_ref, v_ref, ag_ref, yh_ref, state_ref, width, layer, last):
    x = x_ref[...]
    ms = jnp.mean(x * x, axis=-1, keepdims=True)
    hn = ((x * lax.rsqrt(ms + NORM_EPS)) * nw_ref[...]).astype(BF16)

    ang = freq_ref[...] * pos.astype(F32)
    parts = []
    for t in (jnp.cos(ang), jnp.sin(ang)):
        hi = t.astype(BF16).astype(F32)
        parts += [hi, t - hi]
    tabs = lax.dot_general(jnp.concatenate(parts, axis=0).astype(BF16), spread_ref[...],
                           (((0,), (0,)), ((), ())), preferred_element_type=F32)
    cos = tabs[:, :LANES] + cos_base_ref[...]
    sin = tabs[:, LANES:]
    sgn_up = sgn_up_ref[...]
    sgn_dn = sgn_dn_ref[...]
    half = ROPE_DIMS // 2

    def rope(t):
        cols = []
        for j in range(t.shape[1] // LANES):
            tj = t[:, j * LANES:(j + 1) * LANES]
            up = pltpu.roll(tj, LANES - half, 1)
            dn = pltpu.roll(tj, half, 1)
            cols.append(tj * cos + sin * (up * sgn_up + dn * sgn_dn))
        return jnp.concatenate(cols, axis=1)

    def proj(g, part=None):
        lo, size = g * width, width
        if part is not None:
            lo, size = lo + part * (width // 2), width // 2
        return jnp.dot(hn, w_ref[:, lo:lo + size], preferred_element_type=F32)

    scale = ATTN_HEAD_DIM ** -0.5 * LOG2_E

    def emit(ref, g, post):
        def half(part):
            def fill():
                cols = slice(part * (width // 2), (part + 1) * (width // 2))
                ref[:, cols] = post(proj(g, part)).astype(BF16)
            return fill
        return half(0), half(1)

    q_lo, q_hi = emit(q_ref, 0, lambda t: rope(t) * scale)
    k_lo, k_hi = emit(k_ref, 1, rope)
    v_lo, v_hi = emit(v_ref, 2, lambda t: t)
    g_lo, g_hi = emit(ag_ref, 3, _silu)
    if last:
        fills = [[q_lo, q_hi], [k_lo, k_hi], [v_lo], [v_hi], [g_lo], [g_hi]]
    else:
        fills = [[q_lo, q_hi], [k_lo, k_hi], [v_lo, v_hi], [], [g_lo, g_hi], []]

    hf = proj(5)
    hq = _silu(proj(4))
    hi = proj(6).astype(BF16)
    gate = _silu(proj(7))
    yield
    yield from _hgrn_tile(hq, hf, hi, gate, lbraw_ref, hnw_ref, state_ref, yh_ref, layer, fills)


def _rope_tables():
    half = ROPE_DIMS // 2
    inv_freq = ROPE_THETA ** (-np.arange(half, dtype=np.float32) * (2.0 / ROPE_DIMS))
    d = np.arange(LANES) % ATTN_HEAD_DIM
    rotary = d < ROPE_DIMS
    lane_uses = (np.arange(half)[:, None] == (d % half)[None, :]) & rotary[None, :]
    spread = np.zeros((4 * half, 2 * LANES), np.float32)
    for t in range(4):
        table = t // 2
        spread[t * half:(t + 1) * half, table * LANES:(table + 1) * LANES] = lane_uses
    cos_base = np.where(rotary, 0.0, 1.0).astype(np.float32)
    sgn_up = np.where(d < half, -1.0, 0.0).astype(np.float32)
    sgn_dn = np.where((d >= half) & rotary, 1.0, 0.0).astype(np.float32)
    return (inv_freq.astype(np.float32)[:, None], jnp.asarray(spread, BF16), cos_base[None],
            sgn_up[None], sgn_dn[None])


def _in_proj(x2, positions, norm_w, w, lb_raw, hgrn_norm_w, width, layer, seq):
    n, d_model = x2.shape
    rows = PROJ_ROWS * PROJ_TILES
    pos3 = positions.reshape(n // rows, 1, rows)
    consts = _rope_tables()
    row_spec = lambda c: pl.BlockSpec((rows, c), lambda i: (i, 0))
    full_spec = lambda a: pl.BlockSpec(a.shape, lambda i: (0,) * a.ndim)
    pos_spec = pl.BlockSpec((None, 1, rows), lambda i: (i, 0, 0))
    out_sds = jax.ShapeDtypeStruct((n, width), BF16)
    n_out = 5
    small = [*consts, norm_w, lb_raw, hgrn_norm_w]
    return pl.pallas_call(
        functools.partial(_in_proj_kernel, width=width, layer=layer, steps_per_seq=seq // rows),
        out_shape=[out_sds] * n_out,
        grid=(n // rows,),
        in_specs=[row_spec(d_model), pos_spec] + [full_spec(a) for a in small] + [
            pl.BlockSpec(w.shape, lambda i: (0, 0), pipeline_mode=pl.Buffered(1))],
        out_specs=[row_spec(width)] * n_out,
        scratch_shapes=[pltpu.VMEM((width // HGRN_HEAD_DIM, HGRN_HEAD_DIM, HGRN_HEAD_DIM), F32)],
        compiler_params=pltpu.CompilerParams(
            dimension_semantics=("arbitrary",), vmem_limit_bytes=VMEM_LIMIT),
        name="in_proj",
    )(x2, pos3, *small, w)


def _attn_kernel(q_ref, k_ref, v_ref, gate_ref, bias_ref, bias0_ref, seg_ref, nw_ref, o_ref,
                 qf, kf, vf, q4, k4, v4, o_pat, m_pat, l_pat, *, seq):
    blk = ATTN_BLOCK
    dint = ATTN_DEINTERLEAVE
    len4 = seq // dint
    pad4 = blk * max(d for _, d in DILATED_PATTERNS) // dint
    seg4 = pad4 + len4

    qf[...] = q_ref[...].astype(F32)
    for src, dst, dst4 in ((k_ref, kf, k4), (v_ref, vf, v4)):
        dst[0:blk, :] = jnp.zeros((blk, LANES), F32)
        dst[blk:blk + seq, :] = src[...].astype(F32)
        for r in range(dint):
            dst4[r * seg4:r * seg4 + pad4, :] = jnp.zeros((pad4, LANES), F32)
            dst4[r * seg4 + pad4:(r + 1) * seg4, :] = dst[pl.ds(blk + r, len4, stride=dint), :]
    for r in range(dint):
        q4[r * len4:(r + 1) * len4, :] = qf[pl.ds(r, len4, stride=dint), :]

    lane = lax.broadcasted_iota(jnp.int32, (1, LANES), 1)
    head_a = lane < ATTN_HEAD_DIM
    head_b = jnp.logical_not(head_a)

    def rows_of(start, size, stride):
        if stride == 1:
            return pl.ds(pl.multiple_of(start, blk), size)
        return pl.ds(start, size, stride=stride)

    for pat, (window, dil) in enumerate(DILATED_PATTERNS):
        assert window // dil == blk and (dil == 1 or dil % dint == 0)
        n_blk = seq // dil // blk
        n_run = min(ATTN_RUN, n_blk)
        n_res = min(ATTN_CHAINS // n_run, dil)
        runs_per_res = n_blk // n_run
        stride = 1 if dil == 1 else dil // dint
        q_src, k_src, v_src = (qf, kf, vf) if dil == 1 else (q4, k4, v4)

        def body(i, carry, pat=pat, dil=dil, n_run=n_run, n_res=n_res,
                 runs_per_res=runs_per_res, stride=stride, q_src=q_src, k_src=k_src, v_src=v_src):
            for j in range(n_res):
                r = (i // runs_per_res) * n_res + j
                n0 = (i % runs_per_res) * n_run
                if dil == 1:
                    q_start = blk * n0
                    k_start = blk + blk * (n0 - 1)
                else:
                    seg, off = r % dint, r // dint
                    q_start = seg * len4 + off + stride * blk * n0
                    k_start = seg * seg4 + pad4 + off + stride * blk * (n0 - 1)
                k_rows = rows_of(k_start, (n_run + 1) * blk, stride)
                kb = k_src[k_rows, :]
                k_a = jnp.where(head_a, kb, 0.0).astype(BF16)
                k_b = jnp.where(head_b, kb, 0.0).astype(BF16)
                vb = v_src[k_rows, :]
                v_a = jnp.where(head_a, vb, 1.0).astype(BF16)
                v_b = jnp.where(head_b, vb, 1.0).astype(BF16)
                qb = q_src[rows_of(q_start, n_run * blk, stride), :].astype(BF16)
                for u in range(n_run):
                    bias = bias_ref[...]
                    if u == 0:
                        bias = jnp.where(n0 == 0, bias0_ref[...], bias)
                    q_u = qb[u * blk:(u + 1) * blk]
                    keys = slice(u * blk, (u + 2) * blk)

                    def one_head(k_h, v_h):
                        s = lax.dot_general(q_u, k_h[keys], (((1,), (1,)), ((), ())),
                                            preferred_element_type=F32) + bias
                        m = jnp.max(s, axis=-1, keepdims=True)
                        p = jnp.exp2(s - m).astype(BF16)
                        return jnp.dot(p, v_h[keys], preferred_element_type=F32), m

                    ol_a, m_a = one_head(k_a, v_a)
                    ol_b, m_b = one_head(k_b, v_b)
                    rows = rows_of(q_start + u * stride * blk, blk, stride)
                    o_pat[pat, rows, :] = jnp.where(head_a, ol_a, ol_b)
                    m_pat[pat, rows, :] = jnp.where(head_a, m_a, m_b)
                    l_pat[pat, rows, :] = jnp.where(head_a, ol_b, ol_a)
            return carry

        lax.fori_loop(0, dil * n_blk // (n_run * n_res), body, 0)

    blocks_per_seg = len4 // blk

    def merge(i, carry):
        seg, c = i // blocks_per_seg, i % blocks_per_seg
        rows4 = pl.ds(pl.multiple_of(seg * len4 + c * blk, blk), blk)
        rows1 = pl.ds(seg + dint * blk * c, blk, stride=dint)
        rows = [rows1 if dil == 1 else rows4 for _, dil in DILATED_PATTERNS]
        ms = [m_pat[p, rw, :] for p, rw in enumerate(rows)]
        m_top = functools.reduce(jnp.maximum, ms)
        ws = [jnp.exp2(m - m_top) for m in ms]
        num = sum(w * o_pat[p, rw, :] for p, (w, rw) in enumerate(zip(ws, rows)))
        den = sum(w * pltpu.roll(l_pat[p, rw, :], ATTN_HEAD_DIM, 1)
                  for p, (w, rw) in enumerate(zip(ws, rows)))
        ssq = jnp.dot((num * num).astype(BF16), seg_ref[...], preferred_element_type=F32)
        out = num * lax.rsqrt(ssq * (1.0 / ATTN_HEAD_DIM) + NORM_EPS * (den * den))
        qf[rows1, :] = out
        return carry

    assert DILATED_PATTERNS[0][1] == 1
    lax.fori_loop(0, seq // blk, merge, 0, unroll=16)
    o_ref[...] = ((qf[...] * nw_ref[...]) * gate_ref[...].astype(F32)).astype(BF16)


def _band_bias():
    qi = np.arange(ATTN_BLOCK)[:, None]
    kj = np.arange(2 * ATTN_BLOCK)[None, :]
    dist = ATTN_BLOCK + qi - kj
    band = (dist >= 0) & (dist <= ATTN_BLOCK)
    first = band & (kj >= ATTN_BLOCK)
    to_bias = lambda mk: np.where(mk, 0.0, -np.inf).astype(np.float32)
    return to_bias(band), to_bias(first)


def _attention(q, k, v, gate, norm_w):
    b, seq, width = q.shape
    bias, bias0 = _band_bias()
    n_pat = len(DILATED_PATTERNS)
    pad4 = ATTN_BLOCK * max(d for _, d in DILATED_PATTERNS) // ATTN_DEINTERLEAVE
    rows4 = seq + ATTN_DEINTERLEAVE * pad4
    head_of_lane = np.arange(LANES) // ATTN_HEAD_DIM
    seg = jnp.asarray((head_of_lane[:, None] == head_of_lane[None, :]).astype(np.float32), BF16)
    col_spec = pl.BlockSpec((None, seq, LANES), lambda i, j: (i, 0, j))
    bias_spec = pl.BlockSpec(bias.shape, lambda i, j: (0, 0))
    seg_spec = pl.BlockSpec(seg.shape, lambda i, j: (0, 0))
    nw_spec = pl.BlockSpec((1, LANES), lambda i, j: (0, j))
    return pl.pallas_call(
        functools.partial(_attn_kernel, seq=seq),
        out_shape=jax.ShapeDtypeStruct((b, seq, width), BF16),
        grid=(b, width // LANES),
        in_specs=[col_spec, col_spec, col_spec, col_spec, bias_spec, bias_spec, seg_spec, nw_spec],
        out_specs=col_spec,
        scratch_shapes=[pltpu.VMEM((seq, LANES), F32),
                        pltpu.VMEM((ATTN_BLOCK + seq, LANES), F32),
                        pltpu.VMEM((ATTN_BLOCK + seq, LANES), F32),
                        pltpu.VMEM((seq, LANES), F32),
                        pltpu.VMEM((rows4, LANES), F32),
                        pltpu.VMEM((rows4, LANES), F32),
                        pltpu.VMEM((n_pat, seq, LANES), F32),
                        pltpu.VMEM((n_pat, seq, LANES), F32),
                        pltpu.VMEM((n_pat, seq, LANES), F32)],
        compiler_params=pltpu.CompilerParams(
            dimension_semantics=("parallel", "parallel"), vmem_limit_bytes=VMEM_LIMIT),
        name="dilated_attn",
    )(q, k, v, gate, bias, bias0, seg, norm_w)


def _out_proj_kernel(x_ref, ya_ref, yh_ref, fnw_ref, w_ref, out_ref, *, attn_width):
    mixed = jnp.dot(ya_ref[...], w_ref[0:attn_width, :].astype(BF16),
                    preferred_element_type=F32)
    mixed = mixed + jnp.dot(yh_ref[...], w_ref[attn_width:, :].astype(BF16),
                            preferred_element_type=F32)
    x = x_ref[...] + mixed
    ms = jnp.mean(x * x, axis=-1, keepdims=True)
    out_ref[...] = (x * lax.rsqrt(ms + NORM_EPS)) * fnw_ref[...]


def _out_proj(x2, ya, yh, fnw, w):
    n, d_model = x2.shape
    attn_width = ya.shape[-1]
    rows = OUT_ROWS
    row_spec = lambda c: pl.BlockSpec((rows, c), lambda i: (i, 0))
    full_spec = lambda a: pl.BlockSpec(a.shape, lambda i: (0,) * a.ndim)
    return pl.pallas_call(
        functools.partial(_out_proj_kernel, attn_width=attn_width),
        out_shape=jax.ShapeDtypeStruct((n, d_model), F32),
        grid=(n // rows,),
        in_specs=[row_spec(d_model), row_spec(attn_width), row_spec(yh.shape[-1]),
                  full_spec(fnw), full_spec(w)],
        out_specs=row_spec(d_model),
        compiler_params=pltpu.CompilerParams(
            dimension_semantics=("parallel",), vmem_limit_bytes=VMEM_LIMIT),
        name="out_proj",
    )(x2, ya, yh, fnw, w)


def kernel(x, positions, w_in, w_out, mix_norm_w, attn_out_norm_w, hgrn_out_norm_w,
           hgrn_lb_raw, final_norm_w):
    b, seq, d_model = x.shape
    depth = w_in.shape[0]
    attn_width = attn_out_norm_w.shape[-1]
    hgrn_width = hgrn_out_norm_w.shape[-1]
    assert depth == 1 and attn_width == hgrn_width and w_in.shape[-1] == 8 * attn_width
    assert seq % (max(d for _, d in DILATED_PATTERNS) * ATTN_BLOCK) == 0
    n = b * seq
    layer = 0
    x2 = x.reshape(n, d_model)
    q, k, v, ag, yh = _in_proj(
        x2, positions, mix_norm_w[layer][None], w_in[layer].astype(BF16), hgrn_lb_raw,
        hgrn_out_norm_w[layer][None], attn_width, layer, seq)
    to3 = lambda t: t.reshape(b, seq, t.shape[-1])
    ya = _attention(to3(q), to3(k), to3(v), to3(ag), attn_out_norm_w[layer][None])
    out = _out_proj(x2, ya.reshape(n, attn_width), yh, final_norm_w[None],
                    w_out[layer])
    return out.reshape(b, seq, d_model)
```

```python
import functools

import numpy as np
import jax
import jax.numpy as jnp
from jax import lax
from jax.experimental import pallas as pl
from jax.experimental.pallas import tpu as pltpu

F32 = jnp.float32
BF16 = jnp.bfloat16

LANES = 128
SUBLANES = 8
ATTN_HEAD_DIM = 64
HGRN_HEAD_DIM = 128
DILATED_PATTERNS = ((128, 1), (512, 4), (2048, 16))
ATTN_BLOCK = 128
ATTN_DEINTERLEAVE = 4
ROPE_THETA = 500000.0
ROPE_DIMS = ATTN_HEAD_DIM // 4
HGRN_CHUNK = 64
NORM_EPS = 1e-6
LOG2_E = 1.4426950408889634
VMEM_LIMIT = 56 * 1024 * 1024

PROJ_ROWS = 1024
PROJ_TILES = 1
PROJ_STAGES = 6
PROJ_SKEW = 2
OUT_ROWS = 2048
ATTN_RUN = 32
ATTN_CHAINS = 32


def _silu(t):
    return t * jax.nn.sigmoid(t)


def _cumsum_rows(t):
    n_rows, width = t.shape
    row = lax.broadcasted_iota(jnp.int32, (n_rows, width), 0)
    shift = 1
    while shift < n_rows:
        if shift < SUBLANES:
            prev = jnp.where(row >= shift, pltpu.roll(t, shift, 0), 0.0)
        else:
            prev = jnp.concatenate([jnp.zeros((shift, width), t.dtype), t[:n_rows - shift]], axis=0)
        t = t + prev
        shift *= 2
    return t


def _hgrn_tile(hq, hf, hi, gate, lbraw_ref, norm_w_ref, state_ref, o_ref, layer, interleave):
    chunk = HGRN_CHUNK
    rows, width = hq.shape
    n_chunks = rows // chunk
    n_heads = width // HGRN_HEAD_DIM
    heads = [slice(h * HGRN_HEAD_DIM, (h + 1) * HGRN_HEAD_DIM) for h in range(n_heads)]
    chunks = [slice(c * chunk, (c + 1) * chunk) for c in range(n_chunks)]

    raw = lbraw_ref[...]
    e = jnp.exp(raw - jnp.max(raw, axis=0, keepdims=True))
    sm = e / jnp.sum(e, axis=0, keepdims=True)
    lb = jnp.sum(sm[0:layer + 1, :], axis=0, keepdims=True)

    ti = lax.broadcasted_iota(jnp.int32, (chunk, chunk), 0)
    si = lax.broadcasted_iota(jnp.int32, (chunk, chunk), 1)
    causal = ti >= si

    hkey, cum = [], []
    for rws in chunks:
        f = lb + (1.0 - lb) * jax.nn.sigmoid(hf[rws, :])
        hkey.append(1.0 - f)
        cum.append(_cumsum_rows(jnp.log(f)))
    for fill in interleave[0]:
        fill()
    yield

    q_dec, decay, att, upd = [], [], [], []
    for c, rws in enumerate(chunks):
        dec_c = jnp.exp(cum[c][chunk - 1:chunk, :])
        q_c = (hq[rws, :] * jnp.exp(cum[c])).astype(BF16)
        k_inv = hkey[c] * jnp.exp(-cum[c])
        k_end = (k_inv * dec_c).astype(BF16)
        k_inv = k_inv.astype(BF16)
        att.append([lax.dot_general(q_c[:, cs], k_inv[:, cs], (((1,), (1,)), ((), ())),
                                    preferred_element_type=F32) for cs in heads])
        upd.append([lax.dot_general(hi[rws, cs], k_end[:, cs], (((0,), (0,)), ((), ())),
                                    preferred_element_type=F32) for cs in heads])
        q_dec.append(q_c)
        decay.append(dec_c)
    for fill in interleave[1]:
        fill()
    yield

    o_intra = [[jnp.dot(jnp.where(causal, att[c][h], 0.0).astype(BF16), hi[rws, cs],
                        preferred_element_type=F32) for h, cs in enumerate(heads)]
               for c, rws in enumerate(chunks)]
    for fill in interleave[2]:
        fill()
    yield

    for h, cs in enumerate(heads):
        if h > 0:
            for fill in interleave[2 + h]:
                fill()
        if h == n_heads // 2:
            yield
        st = state_ref[h]
        for c, rws in enumerate(chunks):
            o = o_intra[c][h] + lax.dot_general(
                q_dec[c][:, cs], st.astype(BF16), (((1,), (1,)), ((), ())),
                preferred_element_type=F32)
            ms = jnp.mean(o * o, axis=-1, keepdims=True)
            y = (o * lax.rsqrt(ms + NORM_EPS)) * norm_w_ref[:, cs] * gate[rws, cs]
            o_ref[rws, cs] = y.astype(o_ref.dtype)
            st = decay[c][:, cs] * st + upd[c][h]
        state_ref[h] = st


def _in_proj_kernel(x_ref, pos_ref, freq_ref, spread_ref, cos_base_ref, sgn_up_ref, sgn_dn_ref,
                    nw_ref, lbraw_ref, hnw_ref, w_ref,
                    q_ref, k_ref, v_ref, ag_ref, yh_ref, state_ref,
                    *, width, layer, steps_per_seq):
    @pl.when(pl.program_id(0) % steps_per_seq == 0)
    def _():
        state_ref[...] = jnp.zeros(state_ref.shape, F32)

    tiles = []
    for t in range(PROJ_TILES):
        rows = pl.ds(t * PROJ_ROWS, PROJ_ROWS)
        tiles.append(_in_proj_tile(
            x_ref.at[rows], pos_ref[:, t * PROJ_ROWS:(t + 1) * PROJ_ROWS],
            freq_ref, spread_ref, cos_base_ref, sgn_up_ref, sgn_dn_ref,
            nw_ref, lbraw_ref, hnw_ref, w_ref,
            q_ref.at[rows], k_ref.at[rows], v_ref.at[rows], ag_ref.at[rows],
            yh_ref.at[rows], state_ref, width, layer, last=t == PROJ_TILES - 1))
    for step in range(PROJ_STAGES + PROJ_SKEW * (PROJ_TILES - 1)):
        for t, tile in enumerate(tiles):
            if 0 <= step - PROJ_SKEW * t < PROJ_STAGES:
                next(tile, None)
    done = object()
    assert all(next(tile, done) is done for tile in tiles)


def _in_proj_tile(x_ref, pos, freq_ref, spread_ref, cos_base_ref, sgn_up_ref, sgn_dn_ref,
                  nw_ref, lbraw_ref, hnw_ref, w_ref,
                  q_ref, k_ref, v_ref, ag_ref, yh_ref, state_ref, width, layer, last):
    x = x_ref[...]
    ms = jnp.mean(x * x, axis=-1, keepdims=True)
    hn = ((x * lax.rsqrt(ms + NORM_EPS)) * nw_ref[...]).astype(BF16)

    ang = freq_ref[...] * pos.astype(F32)
    parts = []
    for t in (jnp.cos(ang), jnp.sin(ang)):
        hi = t.astype(BF16).astype(F32)
        parts += [hi, t - hi]
    tabs = lax.dot_general(jnp.concatenate(parts, axis=0).astype(BF16), spread_ref[...],
                           (((0,), (0,)), ((), ())), preferred_element_type=F32)
    cos = tabs[:, :LANES] + cos_base_ref[...]
    sin = tabs[:, LANES:]
    sgn_up = sgn_up_ref[...]
    sgn_dn = sgn_dn_ref[...]
    half = ROPE_DIMS // 2

    def rope(t):
        cols = []
        for j in range(t.shape[1] // LANES):
            tj = t[:, j * LANES:(j + 1) * LANES]
            up = pltpu.roll(tj, LANES - half, 1)
            dn = pltpu.roll(tj, half, 1)
            cols.append(tj * cos + sin * (up * sgn_up + dn * sgn_dn))
        return jnp.concatenate(cols, axis=1)

    def proj(g, part=None):
        lo, size = g * width, width
        if part is not None:
            lo, size = lo + part * (width // 2), width // 2
        return jnp.dot(hn, w_ref[:, lo:lo + size], preferred_element_type=F32)

    scale = ATTN_HEAD_DIM ** -0.5 * LOG2_E

    def emit(ref, g, post):
        def half(part):
            def fill():
                cols = slice(part * (width // 2), (part + 1) * (width // 2))
                ref[:, cols] = post(proj(g, part)).astype(BF16)
            return fill
        return half(0), half(1)

    q_lo, q_hi = emit(q_ref, 0, lambda t: rope(t) * scale)
    k_lo, k_hi = emit(k_ref, 1, rope)
    v_lo, v_hi = emit(v_ref, 2, lambda t: t)
    g_lo, g_hi = emit(ag_ref, 3, _silu)
    if last:
        fills = [[q_lo, q_hi], [k_lo, k_hi], [v_lo], [v_hi], [g_lo], [g_hi]]
    else:
        fills = [[q_lo, q_hi], [k_lo, k_hi], [v_lo, v_hi], [], [g_lo, g_hi], []]

    hf = proj(5)
    hq = _silu(proj(4))
    hi = proj(6).astype(BF16)
    gate = _silu(proj(7))
    yield
    yield from _hgrn_tile(hq, hf, hi, gate, lbraw_ref, hnw_ref, state_ref, yh_ref, layer, fills)


def _rope_tables():
    half = ROPE_DIMS // 2
    inv_freq = ROPE_THETA ** (-np.arange(half, dtype=np.float32) * (2.0 / ROPE_DIMS))
    d = np.arange(LANES) % ATTN_HEAD_DIM
    rotary = d < ROPE_DIMS
    lane_uses = (np.arange(half)[:, None] == (d % half)[None, :]) & rotary[None, :]
    spread = np.zeros((4 * half, 2 * LANES), np.float32)
    for t in range(4):
        table = t // 2
        spread[t * half:(t + 1) * half, table * LANES:(table + 1) * LANES] = lane_uses
    cos_base = np.where(rotary, 0.0, 1.0).astype(np.float32)
    sgn_up = np.where(d < half, -1.0, 0.0).astype(np.float32)
    sgn_dn = np.where((d >= half) & rotary, 1.0, 0.0).astype(np.float32)
    return (inv_freq.astype(np.float32)[:, None], jnp.asarray(spread, BF16), cos_base[None],
            sgn_up[None], sgn_dn[None])


def _in_proj(x2, positions, norm_w, w, lb_raw, hgrn_norm_w, width, layer, seq):
    n, d_model = x2.shape
    rows = PROJ_ROWS * PROJ_TILES
    pos3 = positions.reshape(n // rows, 1, rows)
    consts = _rope_tables()
    row_spec = lambda c: pl.BlockSpec((rows, c), lambda i: (i, 0))
    full_spec = lambda a: pl.BlockSpec(a.shape, lambda i: (0,) * a.ndim)
    pos_spec = pl.BlockSpec((None, 1, rows), lambda i: (i, 0, 0))
    out_sds = jax.ShapeDtypeStruct((n, width), BF16)
    n_out = 5
    small = [*consts, norm_w, lb_raw, hgrn_norm_w]
    return pl.pallas_call(
        functools.partial(_in_proj_kernel, width=width, layer=layer, steps_per_seq=seq // rows),
        out_shape=[out_sds] * n_out,
        grid=(n // rows,),
        in_specs=[row_spec(d_model), pos_spec] + [full_spec(a) for a in small] + [
            pl.BlockSpec(w.shape, lambda i: (0, 0), pipeline_mode=pl.Buffered(1))],
        out_specs=[row_spec(width)] * n_out,
        scratch_shapes=[pltpu.VMEM((width // HGRN_HEAD_DIM, HGRN_HEAD_DIM, HGRN_HEAD_DIM), F32)],
        compiler_params=pltpu.CompilerParams(
            dimension_semantics=("arbitrary",), vmem_limit_bytes=VMEM_LIMIT),
        name="in_proj",
    )(x2, pos3, *small, w)


def _attn_kernel(q_ref, k_ref, v_ref, gate_ref, bias_ref, bias0_ref, seg_ref, nw_ref, o_ref,
                 qf, kf, vf, q4, k4, v4, o_pat, m_pat, l_pat, *, seq):
    blk = ATTN_BLOCK
    dint = ATTN_DEINTERLEAVE
    len4 = seq // dint
    pad4 = blk * max(d for _, d in DILATED_PATTERNS) // dint
    seg4 = pad4 + len4

    qf[...] = q_ref[...].astype(F32)
    for src, dst, dst4 in ((k_ref, kf, k4), (v_ref, vf, v4)):
        dst[0:blk, :] = jnp.zeros((blk, LANES), F32)
        dst[blk:blk + seq, :] = src[...].astype(F32)
        for r in range(dint):
            dst4[r * seg4:r * seg4 + pad4, :] = jnp.zeros((pad4, LANES), F32)
            dst4[r * seg4 + pad4:(r + 1) * seg4, :] = dst[pl.ds(blk + r, len4, stride=dint), :]
    for r in range(dint):
        q4[r * len4:(r + 1) * len4, :] = qf[pl.ds(r, len4, stride=dint), :]

    lane = lax.broadcasted_iota(jnp.int32, (1, LANES), 1)
    head_a = lane < ATTN_HEAD_DIM
    head_b = jnp.logical_not(head_a)

    def rows_of(start, size, stride):
        if stride == 1:
            return pl.ds(pl.multiple_of(start, blk), size)
        return pl.ds(start, size, stride=stride)

    for pat, (window, dil) in enumerate(DILATED_PATTERNS):
        assert window // dil == blk and (dil == 1 or dil % dint == 0)
        n_blk = seq // dil // blk
        n_run = min(ATTN_RUN, n_blk)
        n_res = min(ATTN_CHAINS // n_run, dil)
        runs_per_res = n_blk // n_run
        stride = 1 if dil == 1 else dil // dint
        q_src, k_src, v_src = (qf, kf, vf) if dil == 1 else (q4, k4, v4)

        def body(i, carry, pat=pat, dil=dil, n_run=n_run, n_res=n_res,
                 runs_per_res=runs_per_res, stride=stride, q_src=q_src, k_src=k_src, v_src=v_src):
            for j in range(n_res):
                r = (i // runs_per_res) * n_res + j
                n0 = (i % runs_per_res) * n_run
                if dil == 1:
                    q_start = blk * n0
                    k_start = blk + blk * (n0 - 1)
                else:
                    seg, off = r % dint, r // dint
                    q_start = seg * len4 + off + stride * blk * n0
                    k_start = seg * seg4 + pad4 + off + stride * blk * (n0 - 1)
                k_rows = rows_of(k_start, (n_run + 1) * blk, stride)
                kb = k_src[k_rows, :]
                k_a = jnp.where(head_a, kb, 0.0).astype(BF16)
                k_b = jnp.where(head_b, kb, 0.0).astype(BF16)
                vb = v_src[k_rows, :]
                v_a = jnp.where(head_a, vb, 1.0).astype(BF16)
                v_b = jnp.where(head_b, vb, 1.0).astype(BF16)
                qb = q_src[rows_of(q_start, n_run * blk, stride), :].astype(BF16)
                for u in range(n_run):
                    bias = bias_ref[...]
                    if u == 0:
                        bias = jnp.where(n0 == 0, bias0_ref[...], bias)
                    q_u = qb[u * blk:(u + 1) * blk]
                    keys = slice(u * blk, (u + 2) * blk)

                    def one_head(k_h, v_h):
                        s = lax.dot_general(q_u, k_h[keys], (((1,), (1,)), ((), ())),
                                            preferred_element_type=F32) + bias
                        m = jnp.max(s, axis=-1, keepdims=True)
                        p = jnp.exp2(s - m).astype(BF16)
                        return jnp.dot(p, v_h[keys], preferred_element_type=F32), m

                    ol_a, m_a = one_head(k_a, v_a)
                    ol_b, m_b = one_head(k_b, v_b)
                    rows = rows_of(q_start + u * stride * blk, blk, stride)
                    o_pat[pat, rows, :] = jnp.where(head_a, ol_a, ol_b)
                    m_pat[pat, rows, :] = jnp.where(head_a, m_a, m_b)
                    l_pat[pat, rows, :] = jnp.where(head_a, ol_b, ol_a)
            return carry

        lax.fori_loop(0, dil * n_blk // (n_run * n_res), body, 0)

    blocks_per_seg = len4 // blk

    def merge(i, carry):
        seg, c = i // blocks_per_seg, i % blocks_per_seg
        rows4 = pl.ds(pl.multiple_of(seg * len4 + c * blk, blk), blk)
        rows1 = pl.ds(seg + dint * blk * c, blk, stride=dint)
        rows = [rows1 if dil == 1 else rows4 for _, dil in DILATED_PATTERNS]
        ms = [m_pat[p, rw, :] for p, rw in enumerate(rows)]
        m_top = functools.reduce(jnp.maximum, ms)
        ws = [jnp.exp2(m - m_top) for m in ms]
        num = sum(w * o_pat[p, rw, :] for p, (w, rw) in enumerate(zip(ws, rows)))
        den = sum(w * pltpu.roll(l_pat[p, rw, :], ATTN_HEAD_DIM, 1)
                  for p, (w, rw) in enumerate(zip(ws, rows)))
        ssq = jnp.dot((num * num).astype(BF16), seg_ref[...], preferred_element_type=F32)
        out = num * lax.rsqrt(ssq * (1.0 / ATTN_HEAD_DIM) + NORM_EPS * (den * den))
        qf[rows1, :] = out
        return carry

    assert DILATED_PATTERNS[0][1] == 1
    lax.fori_loop(0, seq // blk, merge, 0, unroll=16)
    o_ref[...] = ((qf[...] * nw_ref[...]) * gate_ref[...].astype(F32)).astype(BF16)


def _band_bias():
    qi = np.arange(ATTN_BLOCK)[:, None]
    kj = np.arange(2 * ATTN_BLOCK)[None, :]
    dist = ATTN_BLOCK + qi - kj
    band = (dist >= 0) & (dist <= ATTN_BLOCK)
    first = band & (kj >= ATTN_BLOCK)
    to_bias = lambda mk: np.where(mk, 0.0, -np.inf).astype(np.float32)
    return to_bias(band), to_bias(first)


def _attention(q, k, v, gate, norm_w):
    b, seq, width = q.shape
    bias, bias0 = _band_bias()
    n_pat = len(DILATED_PATTERNS)
    pad4 = ATTN_BLOCK * max(d for _, d in DILATED_PATTERNS) // ATTN_DEINTERLEAVE
    rows4 = seq + ATTN_DEINTERLEAVE * pad4
    head_of_lane = np.arange(LANES) // ATTN_HEAD_DIM
    seg = jnp.asarray((head_of_lane[:, None] == head_of_lane[None, :]).astype(np.float32), BF16)
    col_spec = pl.BlockSpec((None, seq, LANES), lambda i, j: (i, 0, j))
    bias_spec = pl.BlockSpec(bias.shape, lambda i, j: (0, 0))
    seg_spec = pl.BlockSpec(seg.shape, lambda i, j: (0, 0))
    nw_spec = pl.BlockSpec((1, LANES), lambda i, j: (0, j))
    return pl.pallas_call(
        functools.partial(_attn_kernel, seq=seq),
        out_shape=jax.ShapeDtypeStruct((b, seq, width), BF16),
        grid=(b, width // LANES),
        in_specs=[col_spec, col_spec, col_spec, col_spec, bias_spec, bias_spec, seg_spec, nw_spec],
        out_specs=col_spec,
        scratch_shapes=[pltpu.VMEM((seq, LANES), F32),
                        pltpu.VMEM((ATTN_BLOCK + seq, LANES), F32),
                        pltpu.VMEM((ATTN_BLOCK + seq, LANES), F32),
                        pltpu.VMEM((seq, LANES), F32),
                        pltpu.VMEM((rows4, LANES), F32),
                        pltpu.VMEM((rows4, LANES), F32),
                        pltpu.VMEM((n_pat, seq, LANES), F32),
                        pltpu.VMEM((n_pat, seq, LANES), F32),
                        pltpu.VMEM((n_pat, seq, LANES), F32)],
        compiler_params=pltpu.CompilerParams(
            dimension_semantics=("parallel", "parallel"), vmem_limit_bytes=VMEM_LIMIT),
        name="dilated_attn",
    )(q, k, v, gate, bias, bias0, seg, norm_w)


def _out_proj_kernel(x_ref, ya_ref, yh_ref, fnw_ref, w_ref, out_ref, *, attn_width):
    mixed = jnp.dot(ya_ref[...], w_ref[0:attn_width, :].astype(BF16),
                    preferred_element_type=F32)
    mixed = mixed + jnp.dot(yh_ref[...], w_ref[attn_width:, :].astype(BF16),
                            preferred_element_type=F32)
    x = x_ref[...] + mixed
    ms = jnp.mean(x * x, axis=-1, keepdims=True)
    out_ref[...] = (x * lax.rsqrt(ms + NORM_EPS)) * fnw_ref[...]


def _out_proj(x2, ya, yh, fnw, w):
    n, d_model = x2.shape
    attn_width = ya.shape[-1]
    rows = OUT_ROWS
    row_spec = lambda c: pl.BlockSpec((rows, c), lambda i: (i, 0))
    full_spec = lambda a: pl.BlockSpec(a.shape, lambda i: (0,) * a.ndim)
    return pl.pallas_call(
        functools.partial(_out_proj_kernel, attn_width=attn_width),
        out_shape=jax.ShapeDtypeStruct((n, d_model), F32),
        grid=(n // rows,),
        in_specs=[row_spec(d_model), row_spec(attn_width), row_spec(yh.shape[-1]),
                  full_spec(fnw), full_spec(w)],
        out_specs=row_spec(d_model),
        compiler_params=pltpu.CompilerParams(
            dimension_semantics=("parallel",), vmem_limit_bytes=VMEM_LIMIT),
        name="out_proj",
    )(x2, ya, yh, fnw, w)


def kernel(x, positions, w_in, w_out, mix_norm_w, attn_out_norm_w, hgrn_out_norm_w,
           hgrn_lb_raw, final_norm_w):
    b, seq, d_model = x.shape
    depth = w_in.shape[0]
    attn_width = attn_out_norm_w.shape[-1]
    hgrn_width = hgrn_out_norm_w.shape[-1]
    assert depth == 1 and attn_width == hgrn_width and w_in.shape[-1] == 8 * attn_width
    assert seq % (max(d for _, d in DILATED_PATTERNS) * ATTN_BLOCK) == 0
    n = b * seq
    layer = 0
    x2 = x.reshape(n, d_model)
    q, k, v, ag, yh = _in_proj(
        x2, positions, mix_norm_w[layer][None], w_in[layer].astype(BF16), hgrn_lb_raw,
        hgrn_out_norm_w[layer][None], attn_width, layer, seq)
    to3 = lambda t: t.reshape(b, seq, t.shape[-1])
    ya = _attention(to3(q), to3(k), to3(v), to3(ag), attn_out_norm_w[layer][None])
    out = _out_proj(x2, ya.reshape(n, attn_width), yh, final_norm_w[None],
                    w_out[layer])
    return out.reshape(b, seq, d_model)
```

```python
import functools

import numpy as np
import jax
import jax.numpy as jnp
from jax import lax
from jax.experimental import pallas as pl
from jax.experimental.pallas import tpu as pltpu

F32 = jnp.float32
BF16 = jnp.bfloat16

LANES = 128
SUBLANES = 8
ATTN_HEAD_DIM = 64
HGRN_HEAD_DIM = 128
DILATED_PATTERNS = ((128, 1), (512, 4), (2048, 16))
ATTN_BLOCK = 128
ATTN_DEINTERLEAVE = 4
ROPE_THETA = 500000.0
ROPE_DIMS = ATTN_HEAD_DIM // 4
HGRN_CHUNK = 64
NORM_EPS = 1e-6
LOG2_E = 1.4426950408889634
VMEM_LIMIT = 56 * 1024 * 1024

PROJ_ROWS = 1024
PROJ_TILES = 1
PROJ_STAGES = 6
PROJ_SKEW = 2
OUT_ROWS = 2048
ATTN_RUN = 32
ATTN_CHAINS = 32


def _silu(t):
    return t * jax.nn.sigmoid(t)


def _cumsum_rows(t):
    n_rows, width = t.shape
    row = lax.broadcasted_iota(jnp.int32, (n_rows, width), 0)
    shift = 1
    while shift < n_rows:
        if shift < SUBLANES:
            prev = jnp.where(row >= shift, pltpu.roll(t, shift, 0), 0.0)
        else:
            prev = jnp.concatenate([jnp.zeros((shift, width), t.dtype), t[:n_rows - shift]], axis=0)
        t = t + prev
        shift *= 2
    return t


def _hgrn_tile(hq, hf, hi, gate, lbraw_ref, norm_w_ref, state_ref, o_ref, layer, interleave):
    chunk = HGRN_CHUNK
    rows, width = hq.shape
    n_chunks = rows // chunk
    n_heads = width // HGRN_HEAD_DIM
    heads = [slice(h * HGRN_HEAD_DIM, (h + 1) * HGRN_HEAD_DIM) for h in range(n_heads)]
    chunks = [slice(c * chunk, (c + 1) * chunk) for c in range(n_chunks)]

    raw = lbraw_ref[...]
    e = jnp.exp(raw - jnp.max(raw, axis=0, keepdims=True))
    sm = e / jnp.sum(e, axis=0, keepdims=True)
    lb = jnp.sum(sm[0:layer + 1, :], axis=0, keepdims=True)

    ti = lax.broadcasted_iota(jnp.int32, (chunk, chunk), 0)
    si = lax.broadcasted_iota(jnp.int32, (chunk, chunk), 1)
    causal = ti >= si

    hkey, cum = [], []
    for rws in chunks:
        f = lb + (1.0 - lb) * jax.nn.sigmoid(hf[rws, :])
        hkey.append(1.0 - f)
        cum.append(_cumsum_rows(jnp.log(f)))
    for fill in interleave[0]:
        fill()
    yield

    q_dec, decay, att, upd = [], [], [], []
    for c, rws in enumerate(chunks):
        dec_c = jnp.exp(cum[c][chunk - 1:chunk, :])
        q_c = (hq[rws, :] * jnp.exp(cum[c])).astype(BF16)
        k_inv = hkey[c] * jnp.exp(-cum[c])
        k_end = (k_inv * dec_c).astype(BF16)
        k_inv = k_inv.astype(BF16)
        att.append([lax.dot_general(q_c[:, cs], k_inv[:, cs], (((1,), (1,)), ((), ())),
                                    preferred_element_type=F32) for cs in heads])
        upd.append([lax.dot_general(hi[rws, cs], k_end[:, cs], (((0,), (0,)), ((), ())),
                                    preferred_element_type=F32) for cs in heads])
        q_dec.append(q_c)
        decay.append(dec_c)
    for fill in interleave[1]:
        fill()
    yield

    o_intra = [[jnp.dot(jnp.where(causal, att[c][h], 0.0).astype(BF16), hi[rws, cs],
                        preferred_element_type=F32) for h, cs in enumerate(heads)]
               for c, rws in enumerate(chunks)]
    for fill in interleave[2]:
        fill()
    yield

    for h, cs in enumerate(heads):
        if h > 0:
            for fill in interleave[2 + h]:
                fill()
        if h == n_heads // 2:
            yield
        st = state_ref[h]
        for c, rws in enumerate(chunks):
            o = o_intra[c][h] + lax.dot_general(
                q_dec[c][:, cs], st.astype(BF16), (((1,), (1,)), ((), ())),
                preferred_element_type=F32)
            ms = jnp.mean(o * o, axis=-1, keepdims=True)
            y = (o * lax.rsqrt(ms + NORM_EPS)) * norm_w_ref[:, cs] * gate[rws, cs]
            o_ref[rws, cs] = y.astype(o_ref.dtype)
            st = decay[c][:, cs] * st + upd[c][h]
        state_ref[h] = st


def _in_proj_kernel(x_ref, pos_ref, freq_ref, spread_ref, cos_base_ref, sgn_up_ref, sgn_dn_ref,
                    nw_ref, lbraw_ref, hnw_ref, w_ref,
                    q_ref, k_ref, v_ref, ag_ref, yh_ref, state_ref,
                    *, width, layer, steps_per_seq):
    @pl.when(pl.program_id(0) % steps_per_seq == 0)
    def _():
        state_ref[...] = jnp.zeros(state_ref.shape, F32)

    tiles = []
    for t in range(PROJ_TILES):
        rows = pl.ds(t * PROJ_ROWS, PROJ_ROWS)
        tiles.append(_in_proj_tile(
            x_ref.at[rows], pos_ref[:, t * PROJ_ROWS:(t + 1) * PROJ_ROWS],
            freq_ref, spread_ref, cos_base_ref, sgn_up_ref, sgn_dn_ref,
            nw_ref, lbraw_ref, hnw_ref, w_ref,
            q_ref.at[rows], k_ref.at[rows], v_ref.at[rows], ag_ref.at[rows],
            yh_ref.at[rows], state_ref, width, layer, last=t == PROJ_TILES - 1))
    for step in range(PROJ_STAGES + PROJ_SKEW * (PROJ_TILES - 1)):
        for t, tile in enumerate(tiles):
            if 0 <= step - PROJ_SKEW * t < PROJ_STAGES:
                next(tile, None)
    done = object()
    assert all(next(tile, done) is done for tile in tiles)


def _in_proj_tile(x_ref, pos, freq_ref, spread_ref, cos_base_ref, sgn_up_ref, sgn_dn_ref,
                  nw_ref, lbraw_ref, hnw_ref, w_ref,
                  q_ref, k_ref, v_ref, ag_ref, yh_ref, state_ref, width, layer, last):
    x = x_ref[...]
    ms = jnp.mean(x * x, axis=-1, keepdims=True)
    hn = ((x * lax.rsqrt(ms + NORM_EPS)) * nw_ref[...]).astype(BF16)

    ang = freq_ref[...] * pos.astype(F32)
    parts = []
    for t in (jnp.cos(ang), jnp.sin(ang)):
        hi = t.astype(BF16).astype(F32)
        parts += [hi, t - hi]
    tabs = lax.dot_general(jnp.concatenate(parts, axis=0).astype(BF16), spread_ref[...],
                           (((0,), (0,)), ((), ())), preferred_element_type=F32)
    cos = tabs[:, :LANES] + cos_base_ref[...]
    sin = tabs[:, LANES:]
    sgn_up = sgn_up_ref[...]
    sgn_dn = sgn_dn_ref[...]
    half = ROPE_DIMS // 2

    def rope(t):
        cols = []
        for j in range(t.shape[1] // LANES):
            tj = t[:, j * LANES:(j + 1) * LANES]
            up = pltpu.roll(tj, LANES - half, 1)
            dn = pltpu.roll(tj, half, 1)
            cols.append(tj * cos + sin * (up * sgn_up + dn * sgn_dn))
        return jnp.concatenate(cols, axis=1)

    def proj(g, part=None):
        lo, size = g * width, width
        if part is not None:
            lo, size = lo + part * (width // 2), width // 2
        return jnp.dot(hn, w_ref[:, lo:lo + size], preferred_element_type=F32)

    scale = ATTN_HEAD_DIM ** -0.5 * LOG2_E

    def emit(ref, g, post):
        def half(part):
            def fill():
                cols = slice(part * (width // 2), (part + 1) * (width // 2))
                ref[:, cols] = post(proj(g, part)).astype(BF16)
            return fill
        return half(0), half(1)

    q_lo, q_hi = emit(q_ref, 0, lambda t: rope(t) * scale)
    k_lo, k_hi = emit(k_ref, 1, rope)
    v_lo, v_hi = emit(v_ref, 2, lambda t: t)
    g_lo, g_hi = emit(ag_ref, 3, _silu)
    if last:
        fills = [[q_lo, q_hi], [k_lo, k_hi], [v_lo], [v_hi], [g_lo], [g_hi]]
    else:
        fills = [[q_lo, q_hi], [k_lo, k_hi], [v_lo, v_hi], [], [g_lo, g_hi], []]

    hf = proj(5)
    hq = _silu(proj(4))
    hi = proj(6).astype(BF16)
    gate = _silu(proj(7))
    yield
    yield from _hgrn_tile(hq, hf, hi, gate, lbraw_ref, hnw_ref, state_ref, yh_ref, layer, fills)


def _rope_tables():
    half = ROPE_DIMS // 2
    inv_freq = ROPE_THETA ** (-np.arange(half, dtype=np.float32) * (2.0 / ROPE_DIMS))
    d = np.arange(LANES) % ATTN_HEAD_DIM
    rotary = d < ROPE_DIMS
    lane_uses = (np.arange(half)[:, None] == (d % half)[None, :]) & rotary[None, :]
    spread = np.zeros((4 * half, 2 * LANES), np.float32)
    for t in range(4):
        table = t // 2
        spread[t * half:(t + 1) * half, table * LANES:(table + 1) * LANES] = lane_uses
    cos_base = np.where(rotary, 0.0, 1.0).astype(np.float32)
    sgn_up = np.where(d < half, -1.0, 0.0).astype(np.float32)
    sgn_dn = np.where((d >= half) & rotary, 1.0, 0.0).astype(np.float32)
    return (inv_freq.astype(np.float32)[:, None], jnp.asarray(spread, BF16), cos_base[None],
            sgn_up[None], sgn_dn[None])


def _in_proj(x2, positions, norm_w, w, lb_raw, hgrn_norm_w, width, layer, seq):
    n, d_model = x2.shape
    rows = PROJ_ROWS * PROJ_TILES
    pos3 = positions.reshape(n // rows, 1, rows)
    consts = _rope_tables()
    row_spec = lambda c: pl.BlockSpec((rows, c), lambda i: (i, 0))
    full_spec = lambda a: pl.BlockSpec(a.shape, lambda i: (0,) * a.ndim)
    pos_spec = pl.BlockSpec((None, 1, rows), lambda i: (i, 0, 0))
    out_sds = jax.ShapeDtypeStruct((n, width), BF16)
    n_out = 5
    small = [*consts, norm_w, lb_raw, hgrn_norm_w]
    return pl.pallas_call(
        functools.partial(_in_proj_kernel, width=width, layer=layer, steps_per_seq=seq // rows),
        out_shape=[out_sds] * n_out,
        grid=(n // rows,),
        in_specs=[row_spec(d_model), pos_spec] + [full_spec(a) for a in small] + [
            pl.BlockSpec(w.shape, lambda i: (0, 0), pipeline_mode=pl.Buffered(1))],
        out_specs=[row_spec(width)] * n_out,
        scratch_shapes=[pltpu.VMEM((width // HGRN_HEAD_DIM, HGRN_HEAD_DIM, HGRN_HEAD_DIM), F32)],
        compiler_params=pltpu.CompilerParams(
            dimension_semantics=("arbitrary",), vmem_limit_bytes=VMEM_LIMIT),
        name="in_proj",
    )(x2, pos3, *small, w)


def _attn_kernel(q_ref, k_ref, v_ref, gate_ref, bias_ref, bias0_ref, seg_ref, nw_ref, o_ref,
                 qf, kf, vf, q4, k4, v4, o_pat, m_pat, l_pat, *, seq):
    blk = ATTN_BLOCK
    dint = ATTN_DEINTERLEAVE
    len4 = seq // dint
    pad4 = blk * max(d for _, d in DILATED_PATTERNS) // dint
    seg4 = pad4 + len4

    qf[...] = q_ref[...].astype(F32)
    for src, dst, dst4 in ((k_ref, kf, k4), (v_ref, vf, v4)):
        dst[0:blk, :] = jnp.zeros((blk, LANES), F32)
        dst[blk:blk + seq, :] = src[...].astype(F32)
        for r in range(dint):
            dst4[r * seg4:r * seg4 + pad4, :] = jnp.zeros((pad4, LANES), F32)
            dst4[r * seg4 + pad4:(r + 1) * seg4, :] = dst[pl.ds(blk + r, len4, stride=dint), :]
    for r in range(dint):
        q4[r * len4:(r + 1) * len4, :] = qf[pl.ds(r, len4, stride=dint), :]

    lane = lax.broadcasted_iota(jnp.int32, (1, LANES), 1)
    head_a = lane < ATTN_HEAD_DIM
    head_b = jnp.logical_not(head_a)

    def rows_of(start, size, stride):
        if stride == 1:
            return pl.ds(start if isinstance(start, int) else pl.multiple_of(start, blk), size)
        return pl.ds(start, size, stride=stride)

    for pat, (window, dil) in enumerate(DILATED_PATTERNS):
        assert window // dil == blk and (dil == 1 or dil % dint == 0)
        n_blk = seq // dil // blk
        n_run = min(ATTN_RUN, n_blk)
        n_res = min(ATTN_CHAINS // n_run, dil)
        runs_per_res = n_blk // n_run
        stride = 1 if dil == 1 else dil // dint
        q_src, k_src, v_src = (qf, kf, vf) if dil == 1 else (q4, k4, v4)

        def body(i, carry, pat=pat, dil=dil, n_run=n_run, n_res=n_res,
                 runs_per_res=runs_per_res, stride=stride, q_src=q_src, k_src=k_src, v_src=v_src):
            for j in range(n_res):
                r = (i // runs_per_res) * n_res + j
                n0 = (i % runs_per_res) * n_run
                if dil == 1:
                    q_start = blk * n0
                    k_start = blk + blk * (n0 - 1)
                else:
                    seg, off = r % dint, r // dint
                    q_start = seg * len4 + off + stride * blk * n0
                    k_start = seg * seg4 + pad4 + off + stride * blk * (n0 - 1)
                k_rows = rows_of(k_start, (n_run + 1) * blk, stride)
                kb = k_src[k_rows, :]
                k_a = jnp.where(head_a, kb, 0.0).astype(BF16)
                k_b = jnp.where(head_b, kb, 0.0).astype(BF16)
                vb = v_src[k_rows, :]
                v_a = jnp.where(head_a, vb, 1.0).astype(BF16)
                v_b = jnp.where(head_b, vb, 1.0).astype(BF16)
                qb = q_src[rows_of(q_start, n_run * blk, stride), :].astype(BF16)
                for u in range(n_run):
                    bias = bias_ref[...]
                    keys = slice(u * blk, (u + 2) * blk)
                    if u == 0 and isinstance(n0, int):
                        if n0 == 0:
                            keys = slice(blk, 2 * blk)
                            bias = bias0_ref[:, blk:2 * blk]
                    elif u == 0:
                        bias = jnp.where(n0 == 0, bias0_ref[...], bias)
                    q_u = qb[u * blk:(u + 1) * blk]

                    def one_head(k_h, v_h):
                        s = lax.dot_general(q_u, k_h[keys], (((1,), (1,)), ((), ())),
                                            preferred_element_type=F32) + bias
                        m = jnp.max(s, axis=-1, keepdims=True)
                        p = jnp.exp2(s - m).astype(BF16)
                        return jnp.dot(p, v_h[keys], preferred_element_type=F32), m

                    ol_a, m_a = one_head(k_a, v_a)
                    ol_b, m_b = one_head(k_b, v_b)
                    rows = rows_of(q_start + u * stride * blk, blk, stride)
                    o_pat[pat, rows, :] = jnp.where(head_a, ol_a, ol_b)
                    m_pat[pat, rows, :] = jnp.where(head_a, m_a, m_b)
                    l_pat[pat, rows, :] = jnp.where(head_a, ol_b, ol_a)
            return carry

        n_steps = dil * n_blk // (n_run * n_res)
        if n_steps == 1:
            body(0, 0)
        else:
            lax.fori_loop(0, n_steps, body, 0)

    blocks_per_seg = len4 // blk

    def merge(i, carry):
        seg, c = i // blocks_per_seg, i % blocks_per_seg
        rows4 = pl.ds(pl.multiple_of(seg * len4 + c * blk, blk), blk)
        rows1 = pl.ds(seg + dint * blk * c, blk, stride=dint)
        rows = [rows1 if dil == 1 else rows4 for _, dil in DILATED_PATTERNS]
        ms = [m_pat[p, rw, :] for p, rw in enumerate(rows)]
        m_top = functools.reduce(jnp.maximum, ms)
        ws = [jnp.exp2(m - m_top) for m in ms]
        num = sum(w * o_pat[p, rw, :] for p, (w, rw) in enumerate(zip(ws, rows)))
        den = sum(w * pltpu.roll(l_pat[p, rw, :], ATTN_HEAD_DIM, 1)
                  for p, (w, rw) in enumerate(zip(ws, rows)))
        ssq = jnp.dot((num * num).astype(BF16), seg_ref[...], preferred_element_type=F32)
        out = num * lax.rsqrt(ssq * (1.0 / ATTN_HEAD_DIM) + NORM_EPS * (den * den))
        qf[rows1, :] = out
        return carry

    assert DILATED_PATTERNS[0][1] == 1
    lax.fori_loop(0, seq // blk, merge, 0, unroll=16)
    o_ref[...] = ((qf[...] * nw_ref[...]) * gate_ref[...].astype(F32)).astype(BF16)


def _band_bias():
    qi = np.arange(ATTN_BLOCK)[:, None]
    kj = np.arange(2 * ATTN_BLOCK)[None, :]
    dist = ATTN_BLOCK + qi - kj
    band = (dist >= 0) & (dist <= ATTN_BLOCK)
    first = band & (kj >= ATTN_BLOCK)
    to_bias = lambda mk: np.where(mk, 0.0, -np.inf).astype(np.float32)
    return to_bias(band), to_bias(first)


def _attention(q, k, v, gate, norm_w):
    b, seq, width = q.shape
    bias, bias0 = _band_bias()
    n_pat = len(DILATED_PATTERNS)
    pad4 = ATTN_BLOCK * max(d for _, d in DILATED_PATTERNS) // ATTN_DEINTERLEAVE
    rows4 = seq + ATTN_DEINTERLEAVE * pad4
    head_of_lane = np.arange(LANES) // ATTN_HEAD_DIM
    seg = jnp.asarray((head_of_lane[:, None] == head_of_lane[None, :]).astype(np.float32), BF16)
    col_spec = pl.BlockSpec((None, seq, LANES), lambda i, j: (i, 0, j))
    bias_spec = pl.BlockSpec(bias.shape, lambda i, j: (0, 0))
    seg_spec = pl.BlockSpec(seg.shape, lambda i, j: (0, 0))
    nw_spec = pl.BlockSpec((1, LANES), lambda i, j: (0, j))
    return pl.pallas_call(
        functools.partial(_attn_kernel, seq=seq),
        out_shape=jax.ShapeDtypeStruct((b, seq, width), BF16),
        grid=(b, width // LANES),
        in_specs=[col_spec, col_spec, col_spec, col_spec, bias_spec, bias_spec, seg_spec, nw_spec],
        out_specs=col_spec,
        scratch_shapes=[pltpu.VMEM((seq, LANES), F32),
                        pltpu.VMEM((ATTN_BLOCK + seq, LANES), F32),
                        pltpu.VMEM((ATTN_BLOCK + seq, LANES), F32),
                        pltpu.VMEM((seq, LANES), F32),
                        pltpu.VMEM((rows4, LANES), F32),
                        pltpu.VMEM((rows4, LANES), F32),
                        pltpu.VMEM((n_pat, seq, LANES), F32),
                        pltpu.VMEM((n_pat, seq, LANES), F32),
                        pltpu.VMEM((n_pat, seq, LANES), F32)],
        compiler_params=pltpu.CompilerParams(
            dimension_semantics=("parallel", "parallel"), vmem_limit_bytes=VMEM_LIMIT),
        name="dilated_attn",
    )(q, k, v, gate, bias, bias0, seg, norm_w)


def _out_proj_kernel(x_ref, ya_ref, yh_ref, fnw_ref, w_ref, out_ref, *, attn_width):
    mixed = jnp.dot(ya_ref[...], w_ref[0:attn_width, :].astype(BF16),
                    preferred_element_type=F32)
    mixed = mixed + jnp.dot(yh_ref[...], w_ref[attn_width:, :].astype(BF16),
                            preferred_element_type=F32)
    x = x_ref[...] + mixed
    ms = jnp.mean(x * x, axis=-1, keepdims=True)
    out_ref[...] = (x * lax.rsqrt(ms + NORM_EPS)) * fnw_ref[...]


def _out_proj(x2, ya, yh, fnw, w):
    n, d_model = x2.shape
    attn_width = ya.shape[-1]
    rows = OUT_ROWS
    row_spec = lambda c: pl.BlockSpec((rows, c), lambda i: (i, 0))
    full_spec = lambda a: pl.BlockSpec(a.shape, lambda i: (0,) * a.ndim)
    return pl.pallas_call(
        functools.partial(_out_proj_kernel, attn_width=attn_width),
        out_shape=jax.ShapeDtypeStruct((n, d_model), F32),
        grid=(n // rows,),
        in_specs=[row_spec(d_model), row_spec(attn_width), row_spec(yh.shape[-1]),
                  full_spec(fnw), full_spec(w)],
        out_specs=row_spec(d_model),
        compiler_params=pltpu.CompilerParams(
            dimension_semantics=("parallel",), vmem_limit_bytes=VMEM_LIMIT),
        name="out_proj",
    )(x2, ya, yh, fnw, w)


def kernel(x, positions, w_in, w_out, mix_norm_w, attn_out_norm_w, hgrn_out_norm_w,
           hgrn_lb_raw, final_norm_w):
    b, seq, d_model = x.shape
    depth = w_in.shape[0]
    attn_width = attn_out_norm_w.shape[-1]
    hgrn_width = hgrn_out_norm_w.shape[-1]
    assert depth == 1 and attn_width == hgrn_width and w_in.shape[-1] == 8 * attn_width
    assert seq % (max(d for _, d in DILATED_PATTERNS) * ATTN_BLOCK) == 0
    n = b * seq
    layer = 0
    x2 = x.reshape(n, d_model)
    q, k, v, ag, yh = _in_proj(
        x2, positions, mix_norm_w[layer][None], w_in[layer].astype(BF16), hgrn_lb_raw,
        hgrn_out_norm_w[layer][None], attn_width, layer, seq)
    to3 = lambda t: t.reshape(b, seq, t.shape[-1])
    ya = _attention(to3(q), to3(k), to3(v), to3(ag), attn_out_norm_w[layer][None])
    out = _out_proj(x2, ya.reshape(n, attn_width), yh, final_norm_w[None],
                    w_out[layer])
    return out.reshape(b, seq, d_model)
```

```python
import functools

import numpy as np
import jax
import jax.numpy as jnp
from jax import lax
from jax.experimental import pallas as pl
from jax.experimental.pallas import tpu as pltpu

F32 = jnp.float32
BF16 = jnp.bfloat16

LANES = 128
SUBLANES = 8
ATTN_HEAD_DIM = 64
HGRN_HEAD_DIM = 128
DILATED_PATTERNS = ((128, 1), (512, 4), (2048, 16))
ATTN_BLOCK = 128
ATTN_DEINTERLEAVE = 4
ROPE_THETA = 500000.0
ROPE_DIMS = ATTN_HEAD_DIM // 4
HGRN_CHUNK = 64
NORM_EPS = 1e-6
LOG2_E = 1.4426950408889634
VMEM_LIMIT = 56 * 1024 * 1024

PROJ_ROWS = 1024
PROJ_TILES = 1
PROJ_STAGES = 6
PROJ_SKEW = 2
OUT_ROWS = 2048
ATTN_RUN = 32
ATTN_CHAINS = 32


def _silu(t):
    return t * jax.nn.sigmoid(t)


def _cumsum_rows(t):
    n_rows, width = t.shape
    row = lax.broadcasted_iota(jnp.int32, (n_rows, width), 0)
    shift = 1
    while shift < n_rows:
        if shift < SUBLANES:
            prev = jnp.where(row >= shift, pltpu.roll(t, shift, 0), 0.0)
        else:
            prev = jnp.concatenate([jnp.zeros((shift, width), t.dtype), t[:n_rows - shift]], axis=0)
        t = t + prev
        shift *= 2
    return t


def _hgrn_tile(hq, hf, hi, gate, lbraw_ref, norm_w_ref, state_ref, o_ref, layer, interleave):
    chunk = HGRN_CHUNK
    rows, width = hq.shape
    n_chunks = rows // chunk
    n_heads = width // HGRN_HEAD_DIM
    heads = [slice(h * HGRN_HEAD_DIM, (h + 1) * HGRN_HEAD_DIM) for h in range(n_heads)]
    chunks = [slice(c * chunk, (c + 1) * chunk) for c in range(n_chunks)]

    raw = lbraw_ref[...]
    e = jnp.exp(raw - jnp.max(raw, axis=0, keepdims=True))
    sm = e / jnp.sum(e, axis=0, keepdims=True)
    lb = jnp.sum(sm[0:layer + 1, :], axis=0, keepdims=True)

    ti = lax.broadcasted_iota(jnp.int32, (chunk, chunk), 0)
    si = lax.broadcasted_iota(jnp.int32, (chunk, chunk), 1)
    causal = ti >= si

    hkey, cum = [], []
    for rws in chunks:
        f = lb + (1.0 - lb) * jax.nn.sigmoid(hf[rws, :])
        hkey.append(1.0 - f)
        cum.append(_cumsum_rows(jnp.log(f)))
    for fill in interleave[0]:
        fill()
    yield

    q_dec, decay, att, upd = [], [], [], []
    for c, rws in enumerate(chunks):
        dec_c = jnp.exp(cum[c][chunk - 1:chunk, :])
        q_c = (hq[rws, :] * jnp.exp(cum[c])).astype(BF16)
        k_inv = hkey[c] * jnp.exp(-cum[c])
        k_end = (k_inv * dec_c).astype(BF16)
        k_inv = k_inv.astype(BF16)
        att.append([lax.dot_general(q_c[:, cs], k_inv[:, cs], (((1,), (1,)), ((), ())),
                                    preferred_element_type=F32) for cs in heads])
        upd.append([lax.dot_general(hi[rws, cs], k_end[:, cs], (((0,), (0,)), ((), ())),
                                    preferred_element_type=F32) for cs in heads])
        q_dec.append(q_c)
        decay.append(dec_c)
    for fill in interleave[1]:
        fill()
    yield

    o_intra = [[jnp.dot(jnp.where(causal, att[c][h], 0.0).astype(BF16), hi[rws, cs],
                        preferred_element_type=F32) for h, cs in enumerate(heads)]
               for c, rws in enumerate(chunks)]
    for fill in interleave[2]:
        fill()
    yield

    for h, cs in enumerate(heads):
        if h > 0:
            for fill in interleave[2 + h]:
                fill()
        if h == n_heads // 2:
            yield
        st = state_ref[h]
        for c, rws in enumerate(chunks):
            o = o_intra[c][h] + lax.dot_general(
                q_dec[c][:, cs], st.astype(BF16), (((1,), (1,)), ((), ())),
                preferred_element_type=F32)
            ms = jnp.mean(o * o, axis=-1, keepdims=True)
            y = (o * lax.rsqrt(ms + NORM_EPS)) * norm_w_ref[:, cs] * gate[rws, cs]
            o_ref[rws, cs] = y.astype(o_ref.dtype)
            st = decay[c][:, cs] * st + upd[c][h]
        state_ref[h] = st


def _in_proj_kernel(x_ref, pos_ref, freq_ref, spread_ref, cos_base_ref, sgn_up_ref, sgn_dn_ref,
                    nw_ref, lbraw_ref, hnw_ref, w_ref,
                    q_ref, k_ref, v_ref, ag_ref, yh_ref, state_ref,
                    *, width, layer, steps_per_seq):
    @pl.when(pl.program_id(0) % steps_per_seq == 0)
    def _():
        state_ref[...] = jnp.zeros(state_ref.shape, F32)

    tiles = []
    for t in range(PROJ_TILES):
        rows = pl.ds(t * PROJ_ROWS, PROJ_ROWS)
        tiles.append(_in_proj_tile(
            x_ref.at[rows], pos_ref[:, t * PROJ_ROWS:(t + 1) * PROJ_ROWS],
            freq_ref, spread_ref, cos_base_ref, sgn_up_ref, sgn_dn_ref,
            nw_ref, lbraw_ref, hnw_ref, w_ref,
            q_ref.at[rows], k_ref.at[rows], v_ref.at[rows], ag_ref.at[rows],
            yh_ref.at[rows], state_ref, width, layer, last=t == PROJ_TILES - 1))
    for step in range(PROJ_STAGES + PROJ_SKEW * (PROJ_TILES - 1)):
        for t, tile in enumerate(tiles):
            if 0 <= step - PROJ_SKEW * t < PROJ_STAGES:
                next(tile, None)
    done = object()
    assert all(next(tile, done) is done for tile in tiles)


def _in_proj_tile(x_ref, pos, freq_ref, spread_ref, cos_base_ref, sgn_up_ref, sgn_dn_ref,
                  nw_ref, lbraw_ref, hnw_ref, w_ref,
                  q_ref, k_ref, v_ref, ag_ref, yh_ref, state_ref, width, layer, last):
    x = x_ref[...]
    ms = jnp.mean(x * x, axis=-1, keepdims=True)
    hn = ((x * lax.rsqrt(ms + NORM_EPS)) * nw_ref[...]).astype(BF16)

    ang = freq_ref[...] * pos.astype(F32)
    parts = []
    for t in (jnp.cos(ang), jnp.sin(ang)):
        hi = t.astype(BF16).astype(F32)
        parts += [hi, t - hi]
    tabs = lax.dot_general(jnp.concatenate(parts, axis=0).astype(BF16), spread_ref[...],
                           (((0,), (0,)), ((), ())), preferred_element_type=F32)
    cos = tabs[:, :LANES] + cos_base_ref[...]
    sin = tabs[:, LANES:]
    sgn_up = sgn_up_ref[...]
    sgn_dn = sgn_dn_ref[...]
    half = ROPE_DIMS // 2

    def rope(t):
        cols = []
        for j in range(t.shape[1] // LANES):
            tj = t[:, j * LANES:(j + 1) * LANES]
            up = pltpu.roll(tj, LANES - half, 1)
            dn = pltpu.roll(tj, half, 1)
            cols.append(tj * cos + sin * (up * sgn_up + dn * sgn_dn))
        return jnp.concatenate(cols, axis=1)

    def proj(g, part=None):
        lo, size = g * width, width
        if part is not None:
            lo, size = lo + part * (width // 2), width // 2
        return jnp.dot(hn, w_ref[:, lo:lo + size], preferred_element_type=F32)

    scale = ATTN_HEAD_DIM ** -0.5 * LOG2_E

    def emit(ref, g, post):
        def half(part):
            def fill():
                cols = slice(part * (width // 2), (part + 1) * (width // 2))
                ref[:, cols] = post(proj(g, part)).astype(BF16)
            return fill
        return half(0), half(1)

    q_lo, q_hi = emit(q_ref, 0, lambda t: rope(t) * scale)
    k_lo, k_hi = emit(k_ref, 1, rope)
    v_lo, v_hi = emit(v_ref, 2, lambda t: t)
    g_lo, g_hi = emit(ag_ref, 3, _silu)
    if last:
        fills = [[q_lo, q_hi], [k_lo, k_hi], [v_lo], [v_hi], [g_lo], [g_hi]]
    else:
        fills = [[q_lo, q_hi], [k_lo, k_hi], [v_lo, v_hi], [], [g_lo, g_hi], []]

    hf = proj(5)
    hq = _silu(proj(4))
    hi = proj(6).astype(BF16)
    gate = _silu(proj(7))
    yield
    yield from _hgrn_tile(hq, hf, hi, gate, lbraw_ref, hnw_ref, state_ref, yh_ref, layer, fills)


def _rope_tables():
    half = ROPE_DIMS // 2
    inv_freq = ROPE_THETA ** (-np.arange(half, dtype=np.float32) * (2.0 / ROPE_DIMS))
    d = np.arange(LANES) % ATTN_HEAD_DIM
    rotary = d < ROPE_DIMS
    lane_uses = (np.arange(half)[:, None] == (d % half)[None, :]) & rotary[None, :]
    spread = np.zeros((4 * half, 2 * LANES), np.float32)
    for t in range(4):
        table = t // 2
        spread[t * half:(t + 1) * half, table * LANES:(table + 1) * LANES] = lane_uses
    cos_base = np.where(rotary, 0.0, 1.0).astype(np.float32)
    sgn_up = np.where(d < half, -1.0, 0.0).astype(np.float32)
    sgn_dn = np.where((d >= half) & rotary, 1.0, 0.0).astype(np.float32)
    return (inv_freq.astype(np.float32)[:, None], jnp.asarray(spread, BF16), cos_base[None],
            sgn_up[None], sgn_dn[None])


def _in_proj(x2, positions, norm_w, w, lb_raw, hgrn_norm_w, width, layer, seq):
    n, d_model = x2.shape
    rows = PROJ_ROWS * PROJ_TILES
    pos3 = positions.reshape(n // rows, 1, rows)
    consts = _rope_tables()
    row_spec = lambda c: pl.BlockSpec((rows, c), lambda i: (i, 0))
    full_spec = lambda a: pl.BlockSpec(a.shape, lambda i: (0,) * a.ndim)
    pos_spec = pl.BlockSpec((None, 1, rows), lambda i: (i, 0, 0))
    out_sds = jax.ShapeDtypeStruct((n, width), BF16)
    n_out = 5
    small = [*consts, norm_w, lb_raw, hgrn_norm_w]
    return pl.pallas_call(
        functools.partial(_in_proj_kernel, width=width, layer=layer, steps_per_seq=seq // rows),
        out_shape=[out_sds] * n_out,
        grid=(n // rows,),
        in_specs=[row_spec(d_model), pos_spec] + [full_spec(a) for a in small] + [
            pl.BlockSpec(w.shape, lambda i: (0, 0), pipeline_mode=pl.Buffered(1))],
        out_specs=[row_spec(width)] * n_out,
        scratch_shapes=[pltpu.VMEM((width // HGRN_HEAD_DIM, HGRN_HEAD_DIM, HGRN_HEAD_DIM), F32)],
        compiler_params=pltpu.CompilerParams(
            dimension_semantics=("arbitrary",), vmem_limit_bytes=VMEM_LIMIT),
        name="in_proj",
    )(x2, pos3, *small, w)


def _attn_kernel(q_ref, k_ref, v_ref, gate_ref, bias_ref, bias0_ref, seg_ref, nw_ref, o_ref,
                 qf, kf, vf, q4, k4, v4, o_pat, m_pat, l_pat, *, seq):
    blk = ATTN_BLOCK
    dint = ATTN_DEINTERLEAVE
    len4 = seq // dint
    pad4 = blk * max(d for _, d in DILATED_PATTERNS) // dint
    seg4 = pad4 + len4

    qf[...] = q_ref[...].astype(F32)
    for src, dst, dst4 in ((k_ref, kf, k4), (v_ref, vf, v4)):
        dst[0:blk, :] = jnp.zeros((blk, LANES), F32)
        dst[blk:blk + seq, :] = src[...].astype(F32)
        for r in range(dint):
            dst4[r * seg4:r * seg4 + pad4, :] = jnp.zeros((pad4, LANES), F32)
            dst4[r * seg4 + pad4:(r + 1) * seg4, :] = dst[pl.ds(blk + r, len4, stride=dint), :]
    for r in range(dint):
        q4[r * len4:(r + 1) * len4, :] = qf[pl.ds(r, len4, stride=dint), :]

    lane = lax.broadcasted_iota(jnp.int32, (1, LANES), 1)
    head_a = lane < ATTN_HEAD_DIM
    head_b = jnp.logical_not(head_a)

    def rows_of(start, size, stride):
        if stride == 1:
            return pl.ds(start if isinstance(start, int) else pl.multiple_of(start, blk), size)
        return pl.ds(start, size, stride=stride)

    for pat, (window, dil) in enumerate(DILATED_PATTERNS):
        assert window // dil == blk and (dil == 1 or dil % dint == 0)
        n_blk = seq // dil // blk
        n_run = min(ATTN_RUN, n_blk)
        n_res = min(ATTN_CHAINS // n_run, dil)
        runs_per_res = n_blk // n_run
        stride = 1 if dil == 1 else dil // dint
        q_src, k_src, v_src = (qf, kf, vf) if dil == 1 else (q4, k4, v4)

        def body(i, carry, pat=pat, dil=dil, n_run=n_run, n_res=n_res,
                 runs_per_res=runs_per_res, stride=stride, q_src=q_src, k_src=k_src, v_src=v_src):
            for j in range(n_res):
                r = (i // runs_per_res) * n_res + j
                n0 = (i % runs_per_res) * n_run
                if dil == 1:
                    q_start = blk * n0
                    k_start = blk + blk * (n0 - 1)
                else:
                    seg, off = r % dint, r // dint
                    q_start = seg * len4 + off + stride * blk * n0
                    k_start = seg * seg4 + pad4 + off + stride * blk * (n0 - 1)
                k_rows = rows_of(k_start, (n_run + 1) * blk, stride)
                kb = k_src[k_rows, :]
                k_a = jnp.where(head_a, kb, 0.0).astype(BF16)
                k_b = jnp.where(head_b, kb, 0.0).astype(BF16)
                vb = v_src[k_rows, :]
                v_a = jnp.where(head_a, vb, 1.0).astype(BF16)
                v_b = jnp.where(head_b, vb, 1.0).astype(BF16)
                qb = q_src[rows_of(q_start, n_run * blk, stride), :].astype(BF16)
                for u in range(n_run):
                    bias = bias_ref[...]
                    keys = slice(u * blk, (u + 2) * blk)
                    if u == 0 and runs_per_res == 1:
                        keys = slice(blk, 2 * blk)
                        bias = bias0_ref[:, blk:2 * blk]
                    elif u == 0:
                        bias = jnp.where(n0 == 0, bias0_ref[...], bias)
                    q_u = qb[u * blk:(u + 1) * blk]

                    def one_head(k_h, v_h):
                        s = lax.dot_general(q_u, k_h[keys], (((1,), (1,)), ((), ())),
                                            preferred_element_type=F32) + bias
                        m = jnp.max(s, axis=-1, keepdims=True)
                        p = jnp.exp2(s - m).astype(BF16)
                        return jnp.dot(p, v_h[keys], preferred_element_type=F32), m

                    ol_a, m_a = one_head(k_a, v_a)
                    ol_b, m_b = one_head(k_b, v_b)
                    rows = rows_of(q_start + u * stride * blk, blk, stride)
                    o_pat[pat, rows, :] = jnp.where(head_a, ol_a, ol_b)
                    m_pat[pat, rows, :] = jnp.where(head_a, m_a, m_b)
                    l_pat[pat, rows, :] = jnp.where(head_a, ol_b, ol_a)
            return carry

        lax.fori_loop(0, dil * n_blk // (n_run * n_res), body, 0)

    blocks_per_seg = len4 // blk

    def merge(i, carry):
        seg, c = i // blocks_per_seg, i % blocks_per_seg
        rows4 = pl.ds(pl.multiple_of(seg * len4 + c * blk, blk), blk)
        rows1 = pl.ds(seg + dint * blk * c, blk, stride=dint)
        rows = [rows1 if dil == 1 else rows4 for _, dil in DILATED_PATTERNS]
        ms = [m_pat[p, rw, :] for p, rw in enumerate(rows)]
        m_top = functools.reduce(jnp.maximum, ms)
        ws = [jnp.exp2(m - m_top) for m in ms]
        num = sum(w * o_pat[p, rw, :] for p, (w, rw) in enumerate(zip(ws, rows)))
        den = sum(w * pltpu.roll(l_pat[p, rw, :], ATTN_HEAD_DIM, 1)
                  for p, (w, rw) in enumerate(zip(ws, rows)))
        ssq = jnp.dot((num * num).astype(BF16), seg_ref[...], preferred_element_type=F32)
        out = num * lax.rsqrt(ssq * (1.0 / ATTN_HEAD_DIM) + NORM_EPS * (den * den))
        qf[rows1, :] = out
        return carry

    assert DILATED_PATTERNS[0][1] == 1
    lax.fori_loop(0, seq // blk, merge, 0, unroll=16)
    o_ref[...] = ((qf[...] * nw_ref[...]) * gate_ref[...].astype(F32)).astype(BF16)


def _band_bias():
    qi = np.arange(ATTN_BLOCK)[:, None]
    kj = np.arange(2 * ATTN_BLOCK)[None, :]
    dist = ATTN_BLOCK + qi - kj
    band = (dist >= 0) & (dist <= ATTN_BLOCK)
    first = band & (kj >= ATTN_BLOCK)
    to_bias = lambda mk: np.where(mk, 0.0, -np.inf).astype(np.float32)
    return to_bias(band), to_bias(first)


def _attention(q, k, v, gate, norm_w):
    b, seq, width = q.shape
    bias, bias0 = _band_bias()
    n_pat = len(DILATED_PATTERNS)
    pad4 = ATTN_BLOCK * max(d for _, d in DILATED_PATTERNS) // ATTN_DEINTERLEAVE
    rows4 = seq + ATTN_DEINTERLEAVE * pad4
    head_of_lane = np.arange(LANES) // ATTN_HEAD_DIM
    seg = jnp.asarray((head_of_lane[:, None] == head_of_lane[None, :]).astype(np.float32), BF16)
    col_spec = pl.BlockSpec((None, seq, LANES), lambda i, j: (i, 0, j))
    bias_spec = pl.BlockSpec(bias.shape, lambda i, j: (0, 0))
    seg_spec = pl.BlockSpec(seg.shape, lambda i, j: (0, 0))
    nw_spec = pl.BlockSpec((1, LANES), lambda i, j: (0, j))
    return pl.pallas_call(
        functools.partial(_attn_kernel, seq=seq),
        out_shape=jax.ShapeDtypeStruct((b, seq, width), BF16),
        grid=(b, width // LANES),
        in_specs=[col_spec, col_spec, col_spec, col_spec, bias_spec, bias_spec, seg_spec, nw_spec],
        out_specs=col_spec,
        scratch_shapes=[pltpu.VMEM((seq, LANES), F32),
                        pltpu.VMEM((ATTN_BLOCK + seq, LANES), F32),
                        pltpu.VMEM((ATTN_BLOCK + seq, LANES), F32),
                        pltpu.VMEM((seq, LANES), F32),
                        pltpu.VMEM((rows4, LANES), F32),
                        pltpu.VMEM((rows4, LANES), F32),
                        pltpu.VMEM((n_pat, seq, LANES), F32),
                        pltpu.VMEM((n_pat, seq, LANES), F32),
                        pltpu.VMEM((n_pat, seq, LANES), F32)],
        compiler_params=pltpu.CompilerParams(
            dimension_semantics=("parallel", "parallel"), vmem_limit_bytes=VMEM_LIMIT),
        name="dilated_attn",
    )(q, k, v, gate, bias, bias0, seg, norm_w)


def _out_proj_kernel(x_ref, ya_ref, yh_ref, fnw_ref, w_ref, out_ref, *, attn_width):
    mixed = jnp.dot(ya_ref[...], w_ref[0:attn_width, :].astype(BF16),
                    preferred_element_type=F32)
    mixed = mixed + jnp.dot(yh_ref[...], w_ref[attn_width:, :].astype(BF16),
                            preferred_element_type=F32)
    x = x_ref[...] + mixed
    ms = jnp.mean(x * x, axis=-1, keepdims=True)
    out_ref[...] = (x * lax.rsqrt(ms + NORM_EPS)) * fnw_ref[...]


def _out_proj(x2, ya, yh, fnw, w):
    n, d_model = x2.shape
    attn_width = ya.shape[-1]
    rows = OUT_ROWS
    row_spec = lambda c: pl.BlockSpec((rows, c), lambda i: (i, 0))
    full_spec = lambda a: pl.BlockSpec(a.shape, lambda i: (0,) * a.ndim)
    return pl.pallas_call(
        functools.partial(_out_proj_kernel, attn_width=attn_width),
        out_shape=jax.ShapeDtypeStruct((n, d_model), F32),
        grid=(n // rows,),
        in_specs=[row_spec(d_model), row_spec(attn_width), row_spec(yh.shape[-1]),
                  full_spec(fnw), full_spec(w)],
        out_specs=row_spec(d_model),
        compiler_params=pltpu.CompilerParams(
            dimension_semantics=("parallel",), vmem_limit_bytes=VMEM_LIMIT),
        name="out_proj",
    )(x2, ya, yh, fnw, w)


def kernel(x, positions, w_in, w_out, mix_norm_w, attn_out_norm_w, hgrn_out_norm_w,
           hgrn_lb_raw, final_norm_w):
    b, seq, d_model = x.shape
    depth = w_in.shape[0]
    attn_width = attn_out_norm_w.shape[-1]
    hgrn_width = hgrn_out_norm_w.shape[-1]
    assert depth == 1 and attn_width == hgrn_width and w_in.shape[-1] == 8 * attn_width
    assert seq % (max(d for _, d in DILATED_PATTERNS) * ATTN_BLOCK) == 0
    n = b * seq
    layer = 0
    x2 = x.reshape(n, d_model)
    q, k, v, ag, yh = _in_proj(
        x2, positions, mix_norm_w[layer][None], w_in[layer].astype(BF16), hgrn_lb_raw,
        hgrn_out_norm_w[layer][None], attn_width, layer, seq)
    to3 = lambda t: t.reshape(b, seq, t.shape[-1])
    ya = _attention(to3(q), to3(k), to3(v), to3(ag), attn_out_norm_w[layer][None])
    out = _out_proj(x2, ya.reshape(n, attn_width), yh, final_norm_w[None],
                    w_out[layer])
    return out.reshape(b, seq, d_model)
```

```python
import functools

import numpy as np
import jax
import jax.numpy as jnp
from jax import lax
from jax.experimental import pallas as pl
from jax.experimental.pallas import tpu as pltpu

F32 = jnp.float32
BF16 = jnp.bfloat16

LANES = 128
SUBLANES = 8
ATTN_HEAD_DIM = 64
HGRN_HEAD_DIM = 128
DILATED_PATTERNS = ((128, 1), (512, 4), (2048, 16))
ATTN_BLOCK = 128
ATTN_DEINTERLEAVE = 4
ROPE_THETA = 500000.0
ROPE_DIMS = ATTN_HEAD_DIM // 4
HGRN_CHUNK = 64
NORM_EPS = 1e-6
LOG2_E = 1.4426950408889634
VMEM_LIMIT = 56 * 1024 * 1024

PROJ_ROWS = 1024
PROJ_TILES = 1
PROJ_STAGES = 6
PROJ_SKEW = 2
OUT_ROWS = 2048
ATTN_RUN = 32
ATTN_CHAINS = 32


def _silu(t):
    return t * jax.nn.sigmoid(t)


def _cumsum_rows(t):
    n_rows, width = t.shape
    row = lax.broadcasted_iota(jnp.int32, (n_rows, width), 0)
    shift = 1
    while shift < n_rows:
        if shift < SUBLANES:
            prev = jnp.where(row >= shift, pltpu.roll(t, shift, 0), 0.0)
        else:
            prev = jnp.concatenate([jnp.zeros((shift, width), t.dtype), t[:n_rows - shift]], axis=0)
        t = t + prev
        shift *= 2
    return t


def _hgrn_tile(hq, hf, hi, gate, lbraw_ref, norm_w_ref, state_ref, o_ref, layer, interleave):
    chunk = HGRN_CHUNK
    rows, width = hq.shape
    n_chunks = rows // chunk
    n_heads = width // HGRN_HEAD_DIM
    heads = [slice(h * HGRN_HEAD_DIM, (h + 1) * HGRN_HEAD_DIM) for h in range(n_heads)]
    chunks = [slice(c * chunk, (c + 1) * chunk) for c in range(n_chunks)]

    raw = lbraw_ref[...]
    e = jnp.exp(raw - jnp.max(raw, axis=0, keepdims=True))
    sm = e / jnp.sum(e, axis=0, keepdims=True)
    lb = jnp.sum(sm[0:layer + 1, :], axis=0, keepdims=True)

    ti = lax.broadcasted_iota(jnp.int32, (chunk, chunk), 0)
    si = lax.broadcasted_iota(jnp.int32, (chunk, chunk), 1)
    causal = ti >= si

    hkey, cum = [], []
    for rws in chunks:
        f = lb + (1.0 - lb) * jax.nn.sigmoid(hf[rws, :])
        hkey.append(1.0 - f)
        cum.append(_cumsum_rows(jnp.log(f)))
    for fill in interleave[0]:
        fill()
    yield

    q_dec, decay, att, upd = [], [], [], []
    for c, rws in enumerate(chunks):
        dec_c = jnp.exp(cum[c][chunk - 1:chunk, :])
        q_c = (hq[rws, :] * jnp.exp(cum[c])).astype(BF16)
        k_inv = hkey[c] * jnp.exp(-cum[c])
        k_end = (k_inv * dec_c).astype(BF16)
        k_inv = k_inv.astype(BF16)
        att.append([lax.dot_general(q_c[:, cs], k_inv[:, cs], (((1,), (1,)), ((), ())),
                                    preferred_element_type=F32) for cs in heads])
        upd.append([lax.dot_general(hi[rws, cs], k_end[:, cs], (((0,), (0,)), ((), ())),
                                    preferred_element_type=F32) for cs in heads])
        q_dec.append(q_c)
        decay.append(dec_c)
    for fill in interleave[1]:
        fill()
    yield

    o_intra = [[jnp.dot(jnp.where(causal, att[c][h], 0.0).astype(BF16), hi[rws, cs],
                        preferred_element_type=F32) for h, cs in enumerate(heads)]
               for c, rws in enumerate(chunks)]
    for fill in interleave[2]:
        fill()
    yield

    for h, cs in enumerate(heads):
        if h > 0:
            for fill in interleave[2 + h]:
                fill()
        if h == n_heads // 2:
            yield
        st = state_ref[h]
        for c, rws in enumerate(chunks):
            o = o_intra[c][h] + lax.dot_general(
                q_dec[c][:, cs], st.astype(BF16), (((1,), (1,)), ((), ())),
                preferred_element_type=F32)
            ms = jnp.mean(o * o, axis=-1, keepdims=True)
            y = (o * lax.rsqrt(ms + NORM_EPS)) * norm_w_ref[:, cs] * gate[rws, cs]
            o_ref[rws, cs] = y.astype(o_ref.dtype)
            st = decay[c][:, cs] * st + upd[c][h]
        state_ref[h] = st


def _in_proj_kernel(x_ref, pos_ref, freq_ref, spread_ref, cos_base_ref, sgn_up_ref, sgn_dn_ref,
                    nw_ref, lbraw_ref, hnw_ref, w_ref,
                    q_ref, k_ref, v_ref, ag_ref, yh_ref, state_ref, wb_ref,
                    *, width, layer, steps_per_seq):
    @pl.when(pl.program_id(0) == 0)
    def _():
        wb_ref[...] = w_ref[...].astype(BF16)

    @pl.when(pl.program_id(0) % steps_per_seq == 0)
    def _():
        state_ref[...] = jnp.zeros(state_ref.shape, F32)

    tiles = []
    for t in range(PROJ_TILES):
        rows = pl.ds(t * PROJ_ROWS, PROJ_ROWS)
        tiles.append(_in_proj_tile(
            x_ref.at[rows], pos_ref[:, t * PROJ_ROWS:(t + 1) * PROJ_ROWS],
            freq_ref, spread_ref, cos_base_ref, sgn_up_ref, sgn_dn_ref,
            nw_ref, lbraw_ref, hnw_ref, wb_ref,
            q_ref.at[rows], k_ref.at[rows], v_ref.at[rows], ag_ref.at[rows],
            yh_ref.at[rows], state_ref, width, layer, last=t == PROJ_TILES - 1))
    for step in range(PROJ_STAGES + PROJ_SKEW * (PROJ_TILES - 1)):
        for t, tile in enumerate(tiles):
            if 0 <= step - PROJ_SKEW * t < PROJ_STAGES:
                next(tile, None)
    done = object()
    assert all(next(tile, done) is done for tile in tiles)


def _in_proj_tile(x_ref, pos, freq_ref, spread_ref, cos_base_ref, sgn_up_ref, sgn_dn_ref,
                  nw_ref, lbraw_ref, hnw_ref, w_ref,
                  q_ref, k_ref, v_ref, ag_ref, yh_ref, state_ref, width, layer, last):
    x = x_ref[...]
    ms = jnp.mean(x * x, axis=-1, keepdims=True)
    hn = ((x * lax.rsqrt(ms + NORM_EPS)) * nw_ref[...]).astype(BF16)

    ang = freq_ref[...] * pos.astype(F32)
    parts = []
    for t in (jnp.cos(ang), jnp.sin(ang)):
        hi = t.astype(BF16).astype(F32)
        parts += [hi, t - hi]
    tabs = lax.dot_general(jnp.concatenate(parts, axis=0).astype(BF16), spread_ref[...],
                           (((0,), (0,)), ((), ())), preferred_element_type=F32)
    cos = tabs[:, :LANES] + cos_base_ref[...]
    sin = tabs[:, LANES:]
    sgn_up = sgn_up_ref[...]
    sgn_dn = sgn_dn_ref[...]
    half = ROPE_DIMS // 2

    def rope(t):
        cols = []
        for j in range(t.shape[1] // LANES):
            tj = t[:, j * LANES:(j + 1) * LANES]
            up = pltpu.roll(tj, LANES - half, 1)
            dn = pltpu.roll(tj, half, 1)
            cols.append(tj * cos + sin * (up * sgn_up + dn * sgn_dn))
        return jnp.concatenate(cols, axis=1)

    def proj(g, part=None):
        lo, size = g * width, width
        if part is not None:
            lo, size = lo + part * (width // 2), width // 2
        return jnp.dot(hn, w_ref[:, lo:lo + size], preferred_element_type=F32)

    scale = ATTN_HEAD_DIM ** -0.5 * LOG2_E

    def emit(ref, g, post):
        def half(part):
            def fill():
                cols = slice(part * (width // 2), (part + 1) * (width // 2))
                ref[:, cols] = post(proj(g, part)).astype(BF16)
            return fill
        return half(0), half(1)

    q_lo, q_hi = emit(q_ref, 0, lambda t: rope(t) * scale)
    k_lo, k_hi = emit(k_ref, 1, rope)
    v_lo, v_hi = emit(v_ref, 2, lambda t: t)
    g_lo, g_hi = emit(ag_ref, 3, _silu)
    if last:
        fills = [[q_lo, q_hi], [k_lo, k_hi], [v_lo], [v_hi], [g_lo], [g_hi]]
    else:
        fills = [[q_lo, q_hi], [k_lo, k_hi], [v_lo, v_hi], [], [g_lo, g_hi], []]

    hf = proj(5)
    hq = _silu(proj(4))
    hi = proj(6).astype(BF16)
    gate = _silu(proj(7))
    yield
    yield from _hgrn_tile(hq, hf, hi, gate, lbraw_ref, hnw_ref, state_ref, yh_ref, layer, fills)


def _rope_tables():
    half = ROPE_DIMS // 2
    inv_freq = ROPE_THETA ** (-np.arange(half, dtype=np.float32) * (2.0 / ROPE_DIMS))
    d = np.arange(LANES) % ATTN_HEAD_DIM
    rotary = d < ROPE_DIMS
    lane_uses = (np.arange(half)[:, None] == (d % half)[None, :]) & rotary[None, :]
    spread = np.zeros((4 * half, 2 * LANES), np.float32)
    for t in range(4):
        table = t // 2
        spread[t * half:(t + 1) * half, table * LANES:(table + 1) * LANES] = lane_uses
    cos_base = np.where(rotary, 0.0, 1.0).astype(np.float32)
    sgn_up = np.where(d < half, -1.0, 0.0).astype(np.float32)
    sgn_dn = np.where((d >= half) & rotary, 1.0, 0.0).astype(np.float32)
    return (inv_freq.astype(np.float32)[:, None], jnp.asarray(spread, BF16), cos_base[None],
            sgn_up[None], sgn_dn[None])


def _in_proj(x2, positions, norm_w, w, lb_raw, hgrn_norm_w, width, layer, seq):
    n, d_model = x2.shape
    rows = PROJ_ROWS * PROJ_TILES
    pos3 = positions.reshape(n // rows, 1, rows)
    consts = _rope_tables()
    row_spec = lambda c: pl.BlockSpec((rows, c), lambda i: (i, 0))
    full_spec = lambda a: pl.BlockSpec(a.shape, lambda i: (0,) * a.ndim)
    pos_spec = pl.BlockSpec((None, 1, rows), lambda i: (i, 0, 0))
    out_sds = jax.ShapeDtypeStruct((n, width), BF16)
    n_out = 5
    small = [*consts, norm_w, lb_raw, hgrn_norm_w]
    return pl.pallas_call(
        functools.partial(_in_proj_kernel, width=width, layer=layer, steps_per_seq=seq // rows),
        out_shape=[out_sds] * n_out,
        grid=(n // rows,),
        in_specs=[row_spec(d_model), pos_spec] + [full_spec(a) for a in small] + [
            pl.BlockSpec(w.shape, lambda i: (0, 0), pipeline_mode=pl.Buffered(1))],
        out_specs=[row_spec(width)] * n_out,
        scratch_shapes=[pltpu.VMEM((width // HGRN_HEAD_DIM, HGRN_HEAD_DIM, HGRN_HEAD_DIM), F32),
                        pltpu.VMEM(w.shape, BF16)],
        compiler_params=pltpu.CompilerParams(
            dimension_semantics=("arbitrary",), vmem_limit_bytes=VMEM_LIMIT),
        name="in_proj",
    )(x2, pos3, *small, w)


def _attn_kernel(q_ref, k_ref, v_ref, gate_ref, bias_ref, bias0_ref, seg_ref, nw_ref, o_ref,
                 qf, kf, vf, q4, k4, v4, o_pat, m_pat, l_pat, *, seq):
    blk = ATTN_BLOCK
    dint = ATTN_DEINTERLEAVE
    len4 = seq // dint
    pad4 = blk * max(d for _, d in DILATED_PATTERNS) // dint
    seg4 = pad4 + len4

    qf[...] = q_ref[...].astype(F32)
    for src, dst, dst4 in ((k_ref, kf, k4), (v_ref, vf, v4)):
        dst[0:blk, :] = jnp.zeros((blk, LANES), F32)
        dst[blk:blk + seq, :] = src[...].astype(F32)
        for r in range(dint):
            dst4[r * seg4:r * seg4 + pad4, :] = jnp.zeros((pad4, LANES), F32)
            dst4[r * seg4 + pad4:(r + 1) * seg4, :] = dst[pl.ds(blk + r, len4, stride=dint), :]
    for r in range(dint):
        q4[r * len4:(r + 1) * len4, :] = qf[pl.ds(r, len4, stride=dint), :]

    lane = lax.broadcasted_iota(jnp.int32, (1, LANES), 1)
    head_a = lane < ATTN_HEAD_DIM
    head_b = jnp.logical_not(head_a)

    def rows_of(start, size, stride):
        if stride == 1:
            return pl.ds(pl.multiple_of(start, blk), size)
        return pl.ds(start, size, stride=stride)

    for pat, (window, dil) in enumerate(DILATED_PATTERNS):
        assert window // dil == blk and (dil == 1 or dil % dint == 0)
        n_blk = seq // dil // blk
        n_run = min(ATTN_RUN, n_blk)
        n_res = min(ATTN_CHAINS // n_run, dil)
        runs_per_res = n_blk // n_run
        stride = 1 if dil == 1 else dil // dint
        q_src, k_src, v_src = (qf, kf, vf) if dil == 1 else (q4, k4, v4)

        def body(i, carry, pat=pat, dil=dil, n_run=n_run, n_res=n_res,
                 runs_per_res=runs_per_res, stride=stride, q_src=q_src, k_src=k_src, v_src=v_src):
            for j in range(n_res):
                r = (i // runs_per_res) * n_res + j
                n0 = (i % runs_per_res) * n_run
                if dil == 1:
                    q_start = blk * n0
                    k_start = blk + blk * (n0 - 1)
                else:
                    seg, off = r % dint, r // dint
                    q_start = seg * len4 + off + stride * blk * n0
                    k_start = seg * seg4 + pad4 + off + stride * blk * (n0 - 1)
                k_rows = rows_of(k_start, (n_run + 1) * blk, stride)
                kb = k_src[k_rows, :]
                k_a = jnp.where(head_a, kb, 0.0).astype(BF16)
                k_b = jnp.where(head_b, kb, 0.0).astype(BF16)
                vb = v_src[k_rows, :]
                v_a = jnp.where(head_a, vb, 1.0).astype(BF16)
                v_b = jnp.where(head_b, vb, 1.0).astype(BF16)
                qb = q_src[rows_of(q_start, n_run * blk, stride), :].astype(BF16)
                for u in range(n_run):
                    bias = bias_ref[...]
                    if u == 0:
                        bias = jnp.where(n0 == 0, bias0_ref[...], bias)
                    q_u = qb[u * blk:(u + 1) * blk]
                    keys = slice(u * blk, (u + 2) * blk)

                    def one_head(k_h, v_h):
                        s = lax.dot_general(q_u, k_h[keys], (((1,), (1,)), ((), ())),
                                            preferred_element_type=F32) + bias
                        m = jnp.max(s, axis=-1, keepdims=True)
                        p = jnp.exp2(s - m).astype(BF16)
                        return jnp.dot(p, v_h[keys], preferred_element_type=F32), m

                    ol_a, m_a = one_head(k_a, v_a)
                    ol_b, m_b = one_head(k_b, v_b)
                    rows = rows_of(q_start + u * stride * blk, blk, stride)
                    o_pat[pat, rows, :] = jnp.where(head_a, ol_a, ol_b)
                    m_pat[pat, rows, :] = jnp.where(head_a, m_a, m_b)
                    l_pat[pat, rows, :] = jnp.where(head_a, ol_b, ol_a)
            return carry

        lax.fori_loop(0, dil * n_blk // (n_run * n_res), body, 0)

    blocks_per_seg = len4 // blk

    def merge(i, carry):
        seg, c = i // blocks_per_seg, i % blocks_per_seg
        rows4 = pl.ds(pl.multiple_of(seg * len4 + c * blk, blk), blk)
        rows1 = pl.ds(seg + dint * blk * c, blk, stride=dint)
        rows = [rows1 if dil == 1 else rows4 for _, dil in DILATED_PATTERNS]
        ms = [m_pat[p, rw, :] for p, rw in enumerate(rows)]
        m_top = functools.reduce(jnp.maximum, ms)
        ws = [jnp.exp2(m - m_top) for m in ms]
        num = sum(w * o_pat[p, rw, :] for p, (w, rw) in enumerate(zip(ws, rows)))
        den = sum(w * pltpu.roll(l_pat[p, rw, :], ATTN_HEAD_DIM, 1)
                  for p, (w, rw) in enumerate(zip(ws, rows)))
        ssq = jnp.dot((num * num).astype(BF16), seg_ref[...], preferred_element_type=F32)
        out = num * lax.rsqrt(ssq * (1.0 / ATTN_HEAD_DIM) + NORM_EPS * (den * den))
        qf[rows1, :] = out
        return carry

    assert DILATED_PATTERNS[0][1] == 1
    lax.fori_loop(0, seq // blk, merge, 0, unroll=16)
    o_ref[...] = ((qf[...] * nw_ref[...]) * gate_ref[...].astype(F32)).astype(BF16)


def _band_bias():
    qi = np.arange(ATTN_BLOCK)[:, None]
    kj = np.arange(2 * ATTN_BLOCK)[None, :]
    dist = ATTN_BLOCK + qi - kj
    band = (dist >= 0) & (dist <= ATTN_BLOCK)
    first = band & (kj >= ATTN_BLOCK)
    to_bias = lambda mk: np.where(mk, 0.0, -np.inf).astype(np.float32)
    return to_bias(band), to_bias(first)


def _attention(q, k, v, gate, norm_w):
    b, seq, width = q.shape
    bias, bias0 = _band_bias()
    n_pat = len(DILATED_PATTERNS)
    pad4 = ATTN_BLOCK * max(d for _, d in DILATED_PATTERNS) // ATTN_DEINTERLEAVE
    rows4 = seq + ATTN_DEINTERLEAVE * pad4
    head_of_lane = np.arange(LANES) // ATTN_HEAD_DIM
    seg = jnp.asarray((head_of_lane[:, None] == head_of_lane[None, :]).astype(np.float32), BF16)
    col_spec = pl.BlockSpec((None, seq, LANES), lambda i, j: (i, 0, j))
    bias_spec = pl.BlockSpec(bias.shape, lambda i, j: (0, 0))
    seg_spec = pl.BlockSpec(seg.shape, lambda i, j: (0, 0))
    nw_spec = pl.BlockSpec((1, LANES), lambda i, j: (0, j))
    return pl.pallas_call(
        functools.partial(_attn_kernel, seq=seq),
        out_shape=jax.ShapeDtypeStruct((b, seq, width), BF16),
        grid=(b, width // LANES),
        in_specs=[col_spec, col_spec, col_spec, col_spec, bias_spec, bias_spec, seg_spec, nw_spec],
        out_specs=col_spec,
        scratch_shapes=[pltpu.VMEM((seq, LANES), F32),
                        pltpu.VMEM((ATTN_BLOCK + seq, LANES), F32),
                        pltpu.VMEM((ATTN_BLOCK + seq, LANES), F32),
                        pltpu.VMEM((seq, LANES), F32),
                        pltpu.VMEM((rows4, LANES), F32),
                        pltpu.VMEM((rows4, LANES), F32),
                        pltpu.VMEM((n_pat, seq, LANES), F32),
                        pltpu.VMEM((n_pat, seq, LANES), F32),
                        pltpu.VMEM((n_pat, seq, LANES), F32)],
        compiler_params=pltpu.CompilerParams(
            dimension_semantics=("parallel", "parallel"), vmem_limit_bytes=VMEM_LIMIT),
        name="dilated_attn",
    )(q, k, v, gate, bias, bias0, seg, norm_w)


def _out_proj_kernel(x_ref, ya_ref, yh_ref, fnw_ref, w_ref, out_ref, *, attn_width):
    mixed = jnp.dot(ya_ref[...], w_ref[0:attn_width, :].astype(BF16),
                    preferred_element_type=F32)
    mixed = mixed + jnp.dot(yh_ref[...], w_ref[attn_width:, :].astype(BF16),
                            preferred_element_type=F32)
    x = x_ref[...] + mixed
    ms = jnp.mean(x * x, axis=-1, keepdims=True)
    out_ref[...] = (x * lax.rsqrt(ms + NORM_EPS)) * fnw_ref[...]


def _out_proj(x2, ya, yh, fnw, w):
    n, d_model = x2.shape
    attn_width = ya.shape[-1]
    rows = OUT_ROWS
    row_spec = lambda c: pl.BlockSpec((rows, c), lambda i: (i, 0))
    full_spec = lambda a: pl.BlockSpec(a.shape, lambda i: (0,) * a.ndim)
    return pl.pallas_call(
        functools.partial(_out_proj_kernel, attn_width=attn_width),
        out_shape=jax.ShapeDtypeStruct((n, d_model), F32),
        grid=(n // rows,),
        in_specs=[row_spec(d_model), row_spec(attn_width), row_spec(yh.shape[-1]),
                  full_spec(fnw), full_spec(w)],
        out_specs=row_spec(d_model),
        compiler_params=pltpu.CompilerParams(
            dimension_semantics=("parallel",), vmem_limit_bytes=VMEM_LIMIT),
        name="out_proj",
    )(x2, ya, yh, fnw, w)


def kernel(x, positions, w_in, w_out, mix_norm_w, attn_out_norm_w, hgrn_out_norm_w,
           hgrn_lb_raw, final_norm_w):
    b, seq, d_model = x.shape
    depth = w_in.shape[0]
    attn_width = attn_out_norm_w.shape[-1]
    hgrn_width = hgrn_out_norm_w.shape[-1]
    assert depth == 1 and attn_width == hgrn_width and w_in.shape[-1] == 8 * attn_width
    assert seq % (max(d for _, d in DILATED_PATTERNS) * ATTN_BLOCK) == 0
    n = b * seq
    layer = 0
    x2 = x.reshape(n, d_model)
    q, k, v, ag, yh = _in_proj(
        x2, positions, mix_norm_w[layer][None], w_in[layer], hgrn_lb_raw,
        hgrn_out_norm_w[layer][None], attn_width, layer, seq)
    to3 = lambda t: t.reshape(b, seq, t.shape[-1])
    ya = _attention(to3(q), to3(k), to3(v), to3(ag), attn_out_norm_w[layer][None])
    out = _out_proj(x2, ya.reshape(n, attn_width), yh, final_norm_w[None],
                    w_out[layer])
    return out.reshape(b, seq, d_model)
```

```python
import functools

import numpy as np
import jax
import jax.numpy as jnp
from jax import lax
from jax.experimental import pallas as pl
from jax.experimental.pallas import tpu as pltpu

F32 = jnp.float32
BF16 = jnp.bfloat16

LANES = 128
SUBLANES = 8
ATTN_HEAD_DIM = 64
HGRN_HEAD_DIM = 128
DILATED_PATTERNS = ((128, 1), (512, 4), (2048, 16))
ATTN_BLOCK = 128
ATTN_DEINTERLEAVE = 4
ROPE_THETA = 500000.0
ROPE_DIMS = ATTN_HEAD_DIM // 4
HGRN_CHUNK = 64
NORM_EPS = 1e-6
LOG2_E = 1.4426950408889634
VMEM_LIMIT = 56 * 1024 * 1024

PROJ_ROWS = 1024
PROJ_TILES = 1
PROJ_STAGES = 6
PROJ_SKEW = 2
OUT_ROWS = 2048
ATTN_RUN = 32
ATTN_CHAINS = 32


def _silu(t):
    return t * jax.nn.sigmoid(t)


def _cumsum_rows(t):
    n_rows, width = t.shape
    row = lax.broadcasted_iota(jnp.int32, (n_rows, width), 0)
    shift = 1
    while shift < n_rows:
        if shift < SUBLANES:
            prev = jnp.where(row >= shift, pltpu.roll(t, shift, 0), 0.0)
        else:
            prev = jnp.concatenate([jnp.zeros((shift, width), t.dtype), t[:n_rows - shift]], axis=0)
        t = t + prev
        shift *= 2
    return t


def _hgrn_tile(hq, hf, hi, gate, lbraw_ref, norm_w_ref, state_ref, o_ref, layer, interleave):
    chunk = HGRN_CHUNK
    rows, width = hq.shape
    n_chunks = rows // chunk
    n_heads = width // HGRN_HEAD_DIM
    heads = [slice(h * HGRN_HEAD_DIM, (h + 1) * HGRN_HEAD_DIM) for h in range(n_heads)]
    chunks = [slice(c * chunk, (c + 1) * chunk) for c in range(n_chunks)]

    raw = lbraw_ref[...]
    e = jnp.exp(raw - jnp.max(raw, axis=0, keepdims=True))
    sm = e / jnp.sum(e, axis=0, keepdims=True)
    lb = jnp.sum(sm[0:layer + 1, :], axis=0, keepdims=True)

    ti = lax.broadcasted_iota(jnp.int32, (chunk, chunk), 0)
    si = lax.broadcasted_iota(jnp.int32, (chunk, chunk), 1)
    causal = ti >= si

    hkey, cum = [], []
    for rws in chunks:
        f = lb + (1.0 - lb) * jax.nn.sigmoid(hf[rws, :])
        hkey.append(1.0 - f)
        cum.append(_cumsum_rows(jnp.log(f)))
    for fill in interleave[0]:
        fill()
    yield

    q_dec, decay, att, upd = [], [], [], []
    for c, rws in enumerate(chunks):
        dec_c = jnp.exp(cum[c][chunk - 1:chunk, :])
        q_c = (hq[rws, :] * jnp.exp(cum[c])).astype(BF16)
        k_inv = hkey[c] * jnp.exp(-cum[c])
        k_end = (k_inv * dec_c).astype(BF16)
        k_inv = k_inv.astype(BF16)
        att.append([lax.dot_general(q_c[:, cs], k_inv[:, cs], (((1,), (1,)), ((), ())),
                                    preferred_element_type=F32) for cs in heads])
        upd.append([lax.dot_general(hi[rws, cs], k_end[:, cs], (((0,), (0,)), ((), ())),
                                    preferred_element_type=F32) for cs in heads])
        q_dec.append(q_c)
        decay.append(dec_c)
    for fill in interleave[1]:
        fill()
    yield

    o_intra = [[jnp.dot(jnp.where(causal, att[c][h], 0.0).astype(BF16), hi[rws, cs],
                        preferred_element_type=F32) for h, cs in enumerate(heads)]
               for c, rws in enumerate(chunks)]
    for fill in interleave[2]:
        fill()
    yield

    for h, cs in enumerate(heads):
        if h > 0:
            for fill in interleave[2 + h]:
                fill()
        if h == n_heads // 2:
            yield
        st = state_ref[h]
        for c, rws in enumerate(chunks):
            o = o_intra[c][h] + lax.dot_general(
                q_dec[c][:, cs], st.astype(BF16), (((1,), (1,)), ((), ())),
                preferred_element_type=F32)
            ms = jnp.mean(o * o, axis=-1, keepdims=True)
            y = (o * lax.rsqrt(ms + NORM_EPS)) * norm_w_ref[:, cs] * gate[rws, cs]
            o_ref[rws, cs] = y.astype(o_ref.dtype)
            st = decay[c][:, cs] * st + upd[c][h]
        state_ref[h] = st


def _in_proj_kernel(x_first_ref, x_next_ref, pos_ref, freq_ref, spread_ref, cos_base_ref,
                    sgn_up_ref, sgn_dn_ref, nw_ref, lbraw_ref, hnw_ref, w_ref,
                    q_ref, k_ref, v_ref, ag_ref, yh_ref, state_ref, hn_ref,
                    *, width, layer, steps_per_seq):
    def pre_norm(x_ref):
        x = x_ref[...]
        ms = jnp.mean(x * x, axis=-1, keepdims=True)
        return ((x * lax.rsqrt(ms + NORM_EPS)) * nw_ref[...]).astype(BF16)

    @pl.when(pl.program_id(0) == 0)
    def _():
        hn_ref[...] = pre_norm(x_first_ref)

    @pl.when(pl.program_id(0) % steps_per_seq == 0)
    def _():
        state_ref[...] = jnp.zeros(state_ref.shape, F32)

    tiles = []
    for t in range(PROJ_TILES):
        rows = pl.ds(t * PROJ_ROWS, PROJ_ROWS)
        tiles.append(_in_proj_tile(
            hn_ref[rows, :], pos_ref[:, t * PROJ_ROWS:(t + 1) * PROJ_ROWS],
            freq_ref, spread_ref, cos_base_ref, sgn_up_ref, sgn_dn_ref,
            nw_ref, lbraw_ref, hnw_ref, w_ref,
            q_ref.at[rows], k_ref.at[rows], v_ref.at[rows], ag_ref.at[rows],
            yh_ref.at[rows], state_ref, width, layer, last=t == PROJ_TILES - 1))
    for step in range(PROJ_STAGES + PROJ_SKEW * (PROJ_TILES - 1)):
        for t, tile in enumerate(tiles):
            if 0 <= step - PROJ_SKEW * t < PROJ_STAGES:
                next(tile, None)
    done = object()
    assert all(next(tile, done) is done for tile in tiles)

    hn_ref[...] = pre_norm(x_next_ref)


def _in_proj_tile(hn, pos, freq_ref, spread_ref, cos_base_ref, sgn_up_ref, sgn_dn_ref,
                  nw_ref, lbraw_ref, hnw_ref, w_ref,
                  q_ref, k_ref, v_ref, ag_ref, yh_ref, state_ref, width, layer, last):
    ang = freq_ref[...] * pos.astype(F32)
    parts = []
    for t in (jnp.cos(ang), jnp.sin(ang)):
        hi = t.astype(BF16).astype(F32)
        parts += [hi, t - hi]
    tabs = lax.dot_general(jnp.concatenate(parts, axis=0).astype(BF16), spread_ref[...],
                           (((0,), (0,)), ((), ())), preferred_element_type=F32)
    cos = tabs[:, :LANES] + cos_base_ref[...]
    sin = tabs[:, LANES:]
    sgn_up = sgn_up_ref[...]
    sgn_dn = sgn_dn_ref[...]
    half = ROPE_DIMS // 2

    def rope(t):
        cols = []
        for j in range(t.shape[1] // LANES):
            tj = t[:, j * LANES:(j + 1) * LANES]
            up = pltpu.roll(tj, LANES - half, 1)
            dn = pltpu.roll(tj, half, 1)
            cols.append(tj * cos + sin * (up * sgn_up + dn * sgn_dn))
        return jnp.concatenate(cols, axis=1)

    def proj(g, part=None):
        lo, size = g * width, width
        if part is not None:
            lo, size = lo + part * (width // 2), width // 2
        return jnp.dot(hn, w_ref[:, lo:lo + size], preferred_element_type=F32)

    scale = ATTN_HEAD_DIM ** -0.5 * LOG2_E

    def emit(ref, g, post):
        def half(part):
            def fill():
                cols = slice(part * (width // 2), (part + 1) * (width // 2))
                ref[:, cols] = post(proj(g, part)).astype(BF16)
            return fill
        return half(0), half(1)

    q_lo, q_hi = emit(q_ref, 0, lambda t: rope(t) * scale)
    k_lo, k_hi = emit(k_ref, 1, rope)
    v_lo, v_hi = emit(v_ref, 2, lambda t: t)
    g_lo, g_hi = emit(ag_ref, 3, _silu)
    if last:
        fills = [[q_lo, q_hi], [k_lo, k_hi], [v_lo], [v_hi], [g_lo], [g_hi]]
    else:
        fills = [[q_lo, q_hi], [k_lo, k_hi], [v_lo, v_hi], [], [g_lo, g_hi], []]

    hf = proj(5)
    hq = _silu(proj(4))
    hi = proj(6).astype(BF16)
    gate = _silu(proj(7))
    yield
    yield from _hgrn_tile(hq, hf, hi, gate, lbraw_ref, hnw_ref, state_ref, yh_ref, layer, fills)


def _rope_tables():
    half = ROPE_DIMS // 2
    inv_freq = ROPE_THETA ** (-np.arange(half, dtype=np.float32) * (2.0 / ROPE_DIMS))
    d = np.arange(LANES) % ATTN_HEAD_DIM
    rotary = d < ROPE_DIMS
    lane_uses = (np.arange(half)[:, None] == (d % half)[None, :]) & rotary[None, :]
    spread = np.zeros((4 * half, 2 * LANES), np.float32)
    for t in range(4):
        table = t // 2
        spread[t * half:(t + 1) * half, table * LANES:(table + 1) * LANES] = lane_uses
    cos_base = np.where(rotary, 0.0, 1.0).astype(np.float32)
    sgn_up = np.where(d < half, -1.0, 0.0).astype(np.float32)
    sgn_dn = np.where((d >= half) & rotary, 1.0, 0.0).astype(np.float32)
    return (inv_freq.astype(np.float32)[:, None], jnp.asarray(spread, BF16), cos_base[None],
            sgn_up[None], sgn_dn[None])


def _in_proj(x2, positions, norm_w, w, lb_raw, hgrn_norm_w, width, layer, seq):
    n, d_model = x2.shape
    rows = PROJ_ROWS * PROJ_TILES
    pos3 = positions.reshape(n // rows, 1, rows)
    consts = _rope_tables()
    n_steps = n // rows
    row_spec = lambda c: pl.BlockSpec((rows, c), lambda i: (i, 0))
    full_spec = lambda a: pl.BlockSpec(a.shape, lambda i: (0,) * a.ndim)
    pos_spec = pl.BlockSpec((None, 1, rows), lambda i: (i, 0, 0))
    x_first_spec = pl.BlockSpec((rows, d_model), lambda i: (0, 0), pipeline_mode=pl.Buffered(1))
    x_next_spec = pl.BlockSpec((rows, d_model), lambda i: (jnp.minimum(i + 1, n_steps - 1), 0))
    out_sds = jax.ShapeDtypeStruct((n, width), BF16)
    n_out = 5
    small = [*consts, norm_w, lb_raw, hgrn_norm_w]
    return pl.pallas_call(
        functools.partial(_in_proj_kernel, width=width, layer=layer, steps_per_seq=seq // rows),
        out_shape=[out_sds] * n_out,
        grid=(n_steps,),
        in_specs=[x_first_spec, x_next_spec, pos_spec] + [full_spec(a) for a in small] + [
            pl.BlockSpec(w.shape, lambda i: (0, 0), pipeline_mode=pl.Buffered(1))],
        out_specs=[row_spec(width)] * n_out,
        scratch_shapes=[pltpu.VMEM((width // HGRN_HEAD_DIM, HGRN_HEAD_DIM, HGRN_HEAD_DIM), F32),
                        pltpu.VMEM((rows, d_model), BF16)],
        compiler_params=pltpu.CompilerParams(
            dimension_semantics=("arbitrary",), vmem_limit_bytes=VMEM_LIMIT),
        name="in_proj",
    )(x2, x2, pos3, *small, w)


def _attn_kernel(q_ref, k_ref, v_ref, gate_ref, bias_ref, bias0_ref, seg_ref, nw_ref, o_ref,
                 qf, kf, vf, q4, k4, v4, o_pat, m_pat, l_pat, *, seq):
    blk = ATTN_BLOCK
    dint = ATTN_DEINTERLEAVE
    len4 = seq // dint
    pad4 = blk * max(d for _, d in DILATED_PATTERNS) // dint
    seg4 = pad4 + len4

    qf[...] = q_ref[...].astype(F32)
    for src, dst, dst4 in ((k_ref, kf, k4), (v_ref, vf, v4)):
        dst[0:blk, :] = jnp.zeros((blk, LANES), F32)
        dst[blk:blk + seq, :] = src[...].astype(F32)
        for r in range(dint):
            dst4[r * seg4:r * seg4 + pad4, :] = jnp.zeros((pad4, LANES), F32)
            dst4[r * seg4 + pad4:(r + 1) * seg4, :] = dst[pl.ds(blk + r, len4, stride=dint), :]
    for r in range(dint):
        q4[r * len4:(r + 1) * len4, :] = qf[pl.ds(r, len4, stride=dint), :]

    lane = lax.broadcasted_iota(jnp.int32, (1, LANES), 1)
    head_a = lane < ATTN_HEAD_DIM
    head_b = jnp.logical_not(head_a)

    def rows_of(start, size, stride):
        if stride == 1:
            return pl.ds(pl.multiple_of(start, blk), size)
        return pl.ds(start, size, stride=stride)

    for pat, (window, dil) in enumerate(DILATED_PATTERNS):
        assert window // dil == blk and (dil == 1 or dil % dint == 0)
        n_blk = seq // dil // blk
        n_run = min(ATTN_RUN, n_blk)
        n_res = min(ATTN_CHAINS // n_run, dil)
        runs_per_res = n_blk // n_run
        stride = 1 if dil == 1 else dil // dint
        q_src, k_src, v_src = (qf, kf, vf) if dil == 1 else (q4, k4, v4)

        def body(i, carry, pat=pat, dil=dil, n_run=n_run, n_res=n_res,
                 runs_per_res=runs_per_res, stride=stride, q_src=q_src, k_src=k_src, v_src=v_src):
            for j in range(n_res):
                r = (i // runs_per_res) * n_res + j
                n0 = (i % runs_per_res) * n_run
                if dil == 1:
                    q_start = blk * n0
                    k_start = blk + blk * (n0 - 1)
                else:
                    seg, off = r % dint, r // dint
                    q_start = seg * len4 + off + stride * blk * n0
                    k_start = seg * seg4 + pad4 + off + stride * blk * (n0 - 1)
                k_rows = rows_of(k_start, (n_run + 1) * blk, stride)
                kb = k_src[k_rows, :]
                k_a = jnp.where(head_a, kb, 0.0).astype(BF16)
                k_b = jnp.where(head_b, kb, 0.0).astype(BF16)
                vb = v_src[k_rows, :]
                v_a = jnp.where(head_a, vb, 1.0).astype(BF16)
                v_b = jnp.where(head_b, vb, 1.0).astype(BF16)
                qb = q_src[rows_of(q_start, n_run * blk, stride), :].astype(BF16)
                for u in range(n_run):
                    bias = bias_ref[...]
                    if u == 0:
                        bias = jnp.where(n0 == 0, bias0_ref[...], bias)
                    q_u = qb[u * blk:(u + 1) * blk]
                    keys = slice(u * blk, (u + 2) * blk)

                    def one_head(k_h, v_h):
                        s = lax.dot_general(q_u, k_h[keys], (((1,), (1,)), ((), ())),
                                            preferred_element_type=F32) + bias
                        m = jnp.max(s, axis=-1, keepdims=True)
                        p = jnp.exp2(s - m).astype(BF16)
                        return jnp.dot(p, v_h[keys], preferred_element_type=F32), m

                    ol_a, m_a = one_head(k_a, v_a)
                    ol_b, m_b = one_head(k_b, v_b)
                    rows = rows_of(q_start + u * stride * blk, blk, stride)
                    o_pat[pat, rows, :] = jnp.where(head_a, ol_a, ol_b)
                    m_pat[pat, rows, :] = jnp.where(head_a, m_a, m_b)
                    l_pat[pat, rows, :] = jnp.where(head_a, ol_b, ol_a)
            return carry

        lax.fori_loop(0, dil * n_blk // (n_run * n_res), body, 0)

    blocks_per_seg = len4 // blk

    def merge(i, carry):
        seg, c = i // blocks_per_seg, i % blocks_per_seg
        rows4 = pl.ds(pl.multiple_of(seg * len4 + c * blk, blk), blk)
        rows1 = pl.ds(seg + dint * blk * c, blk, stride=dint)
        rows = [rows1 if dil == 1 else rows4 for _, dil in DILATED_PATTERNS]
        ms = [m_pat[p, rw, :] for p, rw in enumerate(rows)]
        m_top = functools.reduce(jnp.maximum, ms)
        ws = [jnp.exp2(m - m_top) for m in ms]
        num = sum(w * o_pat[p, rw, :] for p, (w, rw) in enumerate(zip(ws, rows)))
        den = sum(w * pltpu.roll(l_pat[p, rw, :], ATTN_HEAD_DIM, 1)
                  for p, (w, rw) in enumerate(zip(ws, rows)))
        ssq = jnp.dot((num * num).astype(BF16), seg_ref[...], preferred_element_type=F32)
        out = num * lax.rsqrt(ssq * (1.0 / ATTN_HEAD_DIM) + NORM_EPS * (den * den))
        qf[rows1, :] = out
        return carry

    assert DILATED_PATTERNS[0][1] == 1
    lax.fori_loop(0, seq // blk, merge, 0, unroll=16)
    o_ref[...] = ((qf[...] * nw_ref[...]) * gate_ref[...].astype(F32)).astype(BF16)


def _band_bias():
    qi = np.arange(ATTN_BLOCK)[:, None]
    kj = np.arange(2 * ATTN_BLOCK)[None, :]
    dist = ATTN_BLOCK + qi - kj
    band = (dist >= 0) & (dist <= ATTN_BLOCK)
    first = band & (kj >= ATTN_BLOCK)
    to_bias = lambda mk: np.where(mk, 0.0, -np.inf).astype(np.float32)
    return to_bias(band), to_bias(first)


def _attention(q, k, v, gate, norm_w):
    b, seq, width = q.shape
    bias, bias0 = _band_bias()
    n_pat = len(DILATED_PATTERNS)
    pad4 = ATTN_BLOCK * max(d for _, d in DILATED_PATTERNS) // ATTN_DEINTERLEAVE
    rows4 = seq + ATTN_DEINTERLEAVE * pad4
    head_of_lane = np.arange(LANES) // ATTN_HEAD_DIM
    seg = jnp.asarray((head_of_lane[:, None] == head_of_lane[None, :]).astype(np.float32), BF16)
    col_spec = pl.BlockSpec((None, seq, LANES), lambda i, j: (i, 0, j))
    bias_spec = pl.BlockSpec(bias.shape, lambda i, j: (0, 0))
    seg_spec = pl.BlockSpec(seg.shape, lambda i, j: (0, 0))
    nw_spec = pl.BlockSpec((1, LANES), lambda i, j: (0, j))
    return pl.pallas_call(
        functools.partial(_attn_kernel, seq=seq),
        out_shape=jax.ShapeDtypeStruct((b, seq, width), BF16),
        grid=(b, width // LANES),
        in_specs=[col_spec, col_spec, col_spec, col_spec, bias_spec, bias_spec, seg_spec, nw_spec],
        out_specs=col_spec,
        scratch_shapes=[pltpu.VMEM((seq, LANES), F32),
                        pltpu.VMEM((ATTN_BLOCK + seq, LANES), F32),
                        pltpu.VMEM((ATTN_BLOCK + seq, LANES), F32),
                        pltpu.VMEM((seq, LANES), F32),
                        pltpu.VMEM((rows4, LANES), F32),
                        pltpu.VMEM((rows4, LANES), F32),
                        pltpu.VMEM((n_pat, seq, LANES), F32),
                        pltpu.VMEM((n_pat, seq, LANES), F32),
                        pltpu.VMEM((n_pat, seq, LANES), F32)],
        compiler_params=pltpu.CompilerParams(
            dimension_semantics=("parallel", "parallel"), vmem_limit_bytes=VMEM_LIMIT),
        name="dilated_attn",
    )(q, k, v, gate, bias, bias0, seg, norm_w)


def _out_proj_kernel(x_ref, ya_ref, yh_ref, fnw_ref, w_ref, out_ref, *, attn_width):
    mixed = jnp.dot(ya_ref[...], w_ref[0:attn_width, :].astype(BF16),
                    preferred_element_type=F32)
    mixed = mixed + jnp.dot(yh_ref[...], w_ref[attn_width:, :].astype(BF16),
                            preferred_element_type=F32)
    x = x_ref[...] + mixed
    ms = jnp.mean(x * x, axis=-1, keepdims=True)
    out_ref[...] = (x * lax.rsqrt(ms + NORM_EPS)) * fnw_ref[...]


def _out_proj(x2, ya, yh, fnw, w):
    n, d_model = x2.shape
    attn_width = ya.shape[-1]
    rows = OUT_ROWS
    row_spec = lambda c: pl.BlockSpec((rows, c), lambda i: (i, 0))
    full_spec = lambda a: pl.BlockSpec(a.shape, lambda i: (0,) * a.ndim)
    return pl.pallas_call(
        functools.partial(_out_proj_kernel, attn_width=attn_width),
        out_shape=jax.ShapeDtypeStruct((n, d_model), F32),
        grid=(n // rows,),
        in_specs=[row_spec(d_model), row_spec(attn_width), row_spec(yh.shape[-1]),
                  full_spec(fnw), full_spec(w)],
        out_specs=row_spec(d_model),
        compiler_params=pltpu.CompilerParams(
            dimension_semantics=("parallel",), vmem_limit_bytes=VMEM_LIMIT),
        name="out_proj",
    )(x2, ya, yh, fnw, w)


def kernel(x, positions, w_in, w_out, mix_norm_w, attn_out_norm_w, hgrn_out_norm_w,
           hgrn_lb_raw, final_norm_w):
    b, seq, d_model = x.shape
    depth = w_in.shape[0]
    attn_width = attn_out_norm_w.shape[-1]
    hgrn_width = hgrn_out_norm_w.shape[-1]
    assert depth == 1 and attn_width == hgrn_width and w_in.shape[-1] == 8 * attn_width
    assert seq % (max(d for _, d in DILATED_PATTERNS) * ATTN_BLOCK) == 0
    n = b * seq
    layer = 0
    x2 = x.reshape(n, d_model)
    q, k, v, ag, yh = _in_proj(
        x2, positions, mix_norm_w[layer][None], w_in[layer].astype(BF16), hgrn_lb_raw,
        hgrn_out_norm_w[layer][None], attn_width, layer, seq)
    to3 = lambda t: t.reshape(b, seq, t.shape[-1])
    ya = _attention(to3(q), to3(k), to3(v), to3(ag), attn_out_norm_w[layer][None])
    out = _out_proj(x2, ya.reshape(n, attn_width), yh, final_norm_w[None],
                    w_out[layer])
    return out.reshape(b, seq, d_model)
```

```python
import functools

import numpy as np
import jax
import jax.numpy as jnp
from jax import lax
from jax.experimental import pallas as pl
from jax.experimental.pallas import tpu as pltpu

F32 = jnp.float32
BF16 = jnp.bfloat16

LANES = 128
SUBLANES = 8
ATTN_HEAD_DIM = 64
HGRN_HEAD_DIM = 128
DILATED_PATTERNS = ((128, 1), (512, 4), (2048, 16))
ATTN_BLOCK = 128
ATTN_DEINTERLEAVE = 4
ROPE_THETA = 500000.0
ROPE_DIMS = ATTN_HEAD_DIM // 4
HGRN_CHUNK = 64
NORM_EPS = 1e-6
LOG2_E = 1.4426950408889634
VMEM_LIMIT = 56 * 1024 * 1024

PROJ_ROWS = 1024
OUT_ROWS = 2048
ATTN_RUN = 32
ATTN_CHAINS = 32


def _silu(t):
    return t * jax.nn.sigmoid(t)


def _cumsum_rows(t):
    n_rows, width = t.shape
    row = lax.broadcasted_iota(jnp.int32, (n_rows, width), 0)
    shift = 1
    while shift < n_rows:
        if shift < SUBLANES:
            prev = jnp.where(row >= shift, pltpu.roll(t, shift, 0), 0.0)
        else:
            prev = jnp.concatenate([jnp.zeros((shift, width), t.dtype), t[:n_rows - shift]], axis=0)
        t = t + prev
        shift *= 2
    return t


def _hgrn_tile(hq, hf, hi, gate, lbraw_ref, norm_w_ref, state_ref, o_ref, layer, interleave):
    chunk = HGRN_CHUNK
    rows, width = hq.shape
    n_chunks = rows // chunk
    n_heads = width // HGRN_HEAD_DIM
    heads = [slice(h * HGRN_HEAD_DIM, (h + 1) * HGRN_HEAD_DIM) for h in range(n_heads)]
    chunks = [slice(c * chunk, (c + 1) * chunk) for c in range(n_chunks)]

    raw = lbraw_ref[...]
    e = jnp.exp(raw - jnp.max(raw, axis=0, keepdims=True))
    sm = e / jnp.sum(e, axis=0, keepdims=True)
    lb = jnp.sum(sm[0:layer + 1, :], axis=0, keepdims=True)

    ti = lax.broadcasted_iota(jnp.int32, (chunk, chunk), 0)
    si = lax.broadcasted_iota(jnp.int32, (chunk, chunk), 1)
    causal = ti >= si

    hkey, cum = [], []
    for rws in chunks:
        f = lb + (1.0 - lb) * jax.nn.sigmoid(hf[rws, :])
        hkey.append(1.0 - f)
        cum.append(_cumsum_rows(jnp.log(f)))
    for fill in interleave[0]:
        fill()

    q_dec, decay, att, upd = [], [], [], []
    for c, rws in enumerate(chunks):
        dec_c = jnp.exp(cum[c][chunk - 1:chunk, :])
        q_c = (hq[rws, :] * jnp.exp(cum[c])).astype(BF16)
        k_inv = hkey[c] * jnp.exp(-cum[c])
        k_end = (k_inv * dec_c).astype(BF16)
        k_inv = k_inv.astype(BF16)
        att.append([lax.dot_general(q_c[:, cs], k_inv[:, cs], (((1,), (1,)), ((), ())),
                                    preferred_element_type=F32) for cs in heads])
        upd.append([lax.dot_general(hi[rws, cs], k_end[:, cs], (((0,), (0,)), ((), ())),
                                    preferred_element_type=F32) for cs in heads])
        q_dec.append(q_c)
        decay.append(dec_c)
    for fill in interleave[1]:
        fill()

    o_intra = [[jnp.dot(jnp.where(causal, att[c][h], 0.0).astype(BF16), hi[rws, cs],
                        preferred_element_type=F32) for h, cs in enumerate(heads)]
               for c, rws in enumerate(chunks)]
    for fill in interleave[2]:
        fill()

    for h, cs in enumerate(heads):
        if h > 0:
            for fill in interleave[2 + h]:
                fill()
        st = state_ref[h]
        for c, rws in enumerate(chunks):
            o = o_intra[c][h] + lax.dot_general(
                q_dec[c][:, cs], st.astype(BF16), (((1,), (1,)), ((), ())),
                preferred_element_type=F32)
            ms = jnp.mean(o * o, axis=-1, keepdims=True)
            y = (o * lax.rsqrt(ms + NORM_EPS)) * norm_w_ref[:, cs] * gate[rws, cs]
            o_ref[rws, cs] = y.astype(o_ref.dtype)
            st = decay[c][:, cs] * st + upd[c][h]
        state_ref[h] = st


def _in_proj_kernel(x_ref, pos_ref, freq_ref, spread_ref, cos_base_ref, sgn_up_ref, sgn_dn_ref,
                    nw_ref, lbraw_ref, hnw_ref, w_ref,
                    q_ref, k_ref, v_ref, ag_ref, yh_ref, state_ref, wb_ref,
                    *, width, layer, steps_per_seq):
    @pl.when(pl.program_id(0) == 0)
    def _():
        wb_ref[...] = w_ref[...].astype(BF16)

    @pl.when(pl.program_id(0) % steps_per_seq == 0)
    def _():
        state_ref[...] = jnp.zeros(state_ref.shape, F32)

    _in_proj_tile(x_ref, pos_ref[...], freq_ref, spread_ref, cos_base_ref, sgn_up_ref, sgn_dn_ref,
                  nw_ref, lbraw_ref, hnw_ref, wb_ref,
                  q_ref, k_ref, v_ref, ag_ref, yh_ref, state_ref, width, layer)


def _in_proj_tile(x_ref, pos, freq_ref, spread_ref, cos_base_ref, sgn_up_ref, sgn_dn_ref,
                  nw_ref, lbraw_ref, hnw_ref, w_ref,
                  q_ref, k_ref, v_ref, ag_ref, yh_ref, state_ref, width, layer):
    x = x_ref[...]
    ms = jnp.mean(x * x, axis=-1, keepdims=True)
    hn = ((x * lax.rsqrt(ms + NORM_EPS)) * nw_ref[...]).astype(BF16)

    ang = freq_ref[...] * pos.astype(F32)
    parts = []
    for t in (jnp.cos(ang), jnp.sin(ang)):
        hi = t.astype(BF16).astype(F32)
        parts += [hi, t - hi]
    tabs = lax.dot_general(jnp.concatenate(parts, axis=0).astype(BF16), spread_ref[...],
                           (((0,), (0,)), ((), ())), preferred_element_type=F32)
    cos = tabs[:, :LANES] + cos_base_ref[...]
    sin = tabs[:, LANES:]
    sgn_up = sgn_up_ref[...]
    sgn_dn = sgn_dn_ref[...]
    half = ROPE_DIMS // 2

    def rope(t):
        cols = []
        for j in range(t.shape[1] // LANES):
            tj = t[:, j * LANES:(j + 1) * LANES]
            up = pltpu.roll(tj, LANES - half, 1)
            dn = pltpu.roll(tj, half, 1)
            cols.append(tj * cos + sin * (up * sgn_up + dn * sgn_dn))
        return jnp.concatenate(cols, axis=1)

    def proj(g, part=None):
        lo, size = g * width, width
        if part is not None:
            lo, size = lo + part * (width // 2), width // 2
        return jnp.dot(hn, w_ref[:, lo:lo + size], preferred_element_type=F32)

    scale = ATTN_HEAD_DIM ** -0.5 * LOG2_E

    def emit(ref, g, post):
        def half(part):
            def fill():
                cols = slice(part * (width // 2), (part + 1) * (width // 2))
                ref[:, cols] = post(proj(g, part)).astype(BF16)
            return fill
        return half(0), half(1)

    q_lo, q_hi = emit(q_ref, 0, lambda t: rope(t) * scale)
    k_lo, k_hi = emit(k_ref, 1, rope)
    v_lo, v_hi = emit(v_ref, 2, lambda t: t)
    g_lo, g_hi = emit(ag_ref, 3, _silu)
    fills = [[q_lo, q_hi], [k_lo, k_hi], [v_lo], [v_hi], [g_lo], [g_hi]]

    hf = proj(5)
    hq = _silu(proj(4))
    hi = proj(6).astype(BF16)
    gate = _silu(proj(7))
    _hgrn_tile(hq, hf, hi, gate, lbraw_ref, hnw_ref, state_ref, yh_ref, layer, fills)


def _rope_tables():
    half = ROPE_DIMS // 2
    inv_freq = ROPE_THETA ** (-np.arange(half, dtype=np.float32) * (2.0 / ROPE_DIMS))
    d = np.arange(LANES) % ATTN_HEAD_DIM
    rotary = d < ROPE_DIMS
    lane_uses = (np.arange(half)[:, None] == (d % half)[None, :]) & rotary[None, :]
    spread = np.zeros((4 * half, 2 * LANES), np.float32)
    for t in range(4):
        table = t // 2
        spread[t * half:(t + 1) * half, table * LANES:(table + 1) * LANES] = lane_uses
    cos_base = np.where(rotary, 0.0, 1.0).astype(np.float32)
    sgn_up = np.where(d < half, -1.0, 0.0).astype(np.float32)
    sgn_dn = np.where((d >= half) & rotary, 1.0, 0.0).astype(np.float32)
    return (inv_freq.astype(np.float32)[:, None], jnp.asarray(spread, BF16), cos_base[None],
            sgn_up[None], sgn_dn[None])


def _in_proj(x2, positions, norm_w, w, lb_raw, hgrn_norm_w, width, layer, seq):
    n, d_model = x2.shape
    rows = PROJ_ROWS
    pos3 = positions.reshape(n // rows, 1, rows)
    consts = _rope_tables()
    row_spec = lambda c: pl.BlockSpec((rows, c), lambda i: (i, 0))
    full_spec = lambda a: pl.BlockSpec(a.shape, lambda i: (0,) * a.ndim)
    pos_spec = pl.BlockSpec((None, 1, rows), lambda i: (i, 0, 0))
    out_sds = jax.ShapeDtypeStruct((n, width), BF16)
    n_out = 5
    small = [*consts, norm_w, lb_raw, hgrn_norm_w]
    return pl.pallas_call(
        functools.partial(_in_proj_kernel, width=width, layer=layer, steps_per_seq=seq // rows),
        out_shape=[out_sds] * n_out,
        grid=(n // rows,),
        in_specs=[row_spec(d_model), pos_spec] + [full_spec(a) for a in small] + [
            pl.BlockSpec(w.shape, lambda i: (0, 0), pipeline_mode=pl.Buffered(1))],
        out_specs=[row_spec(width)] * n_out,
        scratch_shapes=[pltpu.VMEM((width // HGRN_HEAD_DIM, HGRN_HEAD_DIM, HGRN_HEAD_DIM), F32),
                        pltpu.VMEM(w.shape, BF16)],
        compiler_params=pltpu.CompilerParams(
            dimension_semantics=("arbitrary",), vmem_limit_bytes=VMEM_LIMIT),
        name="in_proj",
    )(x2, pos3, *small, w)


def _attn_kernel(q_ref, k_ref, v_ref, gate_ref, bias_ref, bias0_ref, seg_ref, nw_ref, o_ref,
                 qf, kf, vf, q4, k4, v4, o_pat, m_pat, l_pat, *, seq):
    blk = ATTN_BLOCK
    dint = ATTN_DEINTERLEAVE
    len4 = seq // dint
    pad4 = blk * max(d for _, d in DILATED_PATTERNS) // dint
    seg4 = pad4 + len4

    qf[...] = q_ref[...].astype(F32)
    for src, dst, dst4 in ((k_ref, kf, k4), (v_ref, vf, v4)):
        dst[0:blk, :] = jnp.zeros((blk, LANES), F32)
        dst[blk:blk + seq, :] = src[...].astype(F32)
        for r in range(dint):
            dst4[r * seg4:r * seg4 + pad4, :] = jnp.zeros((pad4, LANES), F32)
            dst4[r * seg4 + pad4:(r + 1) * seg4, :] = dst[pl.ds(blk + r, len4, stride=dint), :]
    for r in range(dint):
        q4[r * len4:(r + 1) * len4, :] = qf[pl.ds(r, len4, stride=dint), :]

    lane = lax.broadcasted_iota(jnp.int32, (1, LANES), 1)
    head_a = lane < ATTN_HEAD_DIM
    head_b = jnp.logical_not(head_a)

    def rows_of(start, size, stride):
        if stride == 1:
            return pl.ds(pl.multiple_of(start, blk), size)
        return pl.ds(start, size, stride=stride)

    for pat, (window, dil) in enumerate(DILATED_PATTERNS):
        assert window // dil == blk and (dil == 1 or dil % dint == 0)
        n_blk = seq // dil // blk
        n_run = min(ATTN_RUN, n_blk)
        n_res = min(ATTN_CHAINS // n_run, dil)
        runs_per_res = n_blk // n_run
        stride = 1 if dil == 1 else dil // dint
        q_src, k_src, v_src = (qf, kf, vf) if dil == 1 else (q4, k4, v4)

        def body(i, carry, pat=pat, dil=dil, n_run=n_run, n_res=n_res,
                 runs_per_res=runs_per_res, stride=stride, q_src=q_src, k_src=k_src, v_src=v_src):
            for j in range(n_res):
                r = (i // runs_per_res) * n_res + j
                n0 = (i % runs_per_res) * n_run
                if dil == 1:
                    q_start = blk * n0
                    k_start = blk + blk * (n0 - 1)
                else:
                    seg, off = r % dint, r // dint
                    q_start = seg * len4 + off + stride * blk * n0
                    k_start = seg * seg4 + pad4 + off + stride * blk * (n0 - 1)
                k_rows = rows_of(k_start, (n_run + 1) * blk, stride)
                kb = k_src[k_rows, :]
                k_a = jnp.where(head_a, kb, 0.0).astype(BF16)
                k_b = jnp.where(head_b, kb, 0.0).astype(BF16)
                vb = v_src[k_rows, :]
                v_a = jnp.where(head_a, vb, 1.0).astype(BF16)
                v_b = jnp.where(head_b, vb, 1.0).astype(BF16)
                qb = q_src[rows_of(q_start, n_run * blk, stride), :].astype(BF16)
                for u in range(n_run):
                    bias = bias_ref[...]
                    if u == 0:
                        bias = jnp.where(n0 == 0, bias0_ref[...], bias)
                    q_u = qb[u * blk:(u + 1) * blk]
                    keys = slice(u * blk, (u + 2) * blk)

                    def one_head(k_h, v_h):
                        s = lax.dot_general(q_u, k_h[keys], (((1,), (1,)), ((), ())),
                                            preferred_element_type=F32) + bias
                        m = jnp.max(s, axis=-1, keepdims=True)
                        p = jnp.exp2(s - m).astype(BF16)
                        return jnp.dot(p, v_h[keys], preferred_element_type=F32), m

                    ol_a, m_a = one_head(k_a, v_a)
                    ol_b, m_b = one_head(k_b, v_b)
                    rows = rows_of(q_start + u * stride * blk, blk, stride)
                    o_pat[pat, rows, :] = jnp.where(head_a, ol_a, ol_b)
                    m_pat[pat, rows, :] = jnp.where(head_a, m_a, m_b)
                    l_pat[pat, rows, :] = jnp.where(head_a, ol_b, ol_a)
            return carry

        lax.fori_loop(0, dil * n_blk // (n_run * n_res), body, 0)

    blocks_per_seg = len4 // blk

    def merge(i, carry):
        seg, c = i // blocks_per_seg, i % blocks_per_seg
        rows4 = pl.ds(pl.multiple_of(seg * len4 + c * blk, blk), blk)
        rows1 = pl.ds(seg + dint * blk * c, blk, stride=dint)
        rows = [rows1 if dil == 1 else rows4 for _, dil in DILATED_PATTERNS]
        ms = [m_pat[p, rw, :] for p, rw in enumerate(rows)]
        m_top = functools.reduce(jnp.maximum, ms)
        ws = [jnp.exp2(m - m_top) for m in ms]
        num = sum(w * o_pat[p, rw, :] for p, (w, rw) in enumerate(zip(ws, rows)))
        den = sum(w * pltpu.roll(l_pat[p, rw, :], ATTN_HEAD_DIM, 1)
                  for p, (w, rw) in enumerate(zip(ws, rows)))
        ssq = jnp.dot((num * num).astype(BF16), seg_ref[...], preferred_element_type=F32)
        out = num * lax.rsqrt(ssq * (1.0 / ATTN_HEAD_DIM) + NORM_EPS * (den * den))
        qf[rows1, :] = out
        return carry

    assert DILATED_PATTERNS[0][1] == 1
    lax.fori_loop(0, seq // blk, merge, 0, unroll=16)
    o_ref[...] = ((qf[...] * nw_ref[...]) * gate_ref[...].astype(F32)).astype(BF16)


def _band_bias():
    qi = np.arange(ATTN_BLOCK)[:, None]
    kj = np.arange(2 * ATTN_BLOCK)[None, :]
    dist = ATTN_BLOCK + qi - kj
    band = (dist >= 0) & (dist <= ATTN_BLOCK)
    first = band & (kj >= ATTN_BLOCK)
    to_bias = lambda mk: np.where(mk, 0.0, -np.inf).astype(np.float32)
    return to_bias(band), to_bias(first)


def _attention(q, k, v, gate, norm_w):
    b, seq, width = q.shape
    bias, bias0 = _band_bias()
    n_pat = len(DILATED_PATTERNS)
    pad4 = ATTN_BLOCK * max(d for _, d in DILATED_PATTERNS) // ATTN_DEINTERLEAVE
    rows4 = seq + ATTN_DEINTERLEAVE * pad4
    head_of_lane = np.arange(LANES) // ATTN_HEAD_DIM
    seg = jnp.asarray((head_of_lane[:, None] == head_of_lane[None, :]).astype(np.float32), BF16)
    col_spec = pl.BlockSpec((None, seq, LANES), lambda i, j: (i, 0, j))
    bias_spec = pl.BlockSpec(bias.shape, lambda i, j: (0, 0))
    seg_spec = pl.BlockSpec(seg.shape, lambda i, j: (0, 0))
    nw_spec = pl.BlockSpec((1, LANES), lambda i, j: (0, j))
    return pl.pallas_call(
        functools.partial(_attn_kernel, seq=seq),
        out_shape=jax.ShapeDtypeStruct((b, seq, width), BF16),
        grid=(b, width // LANES),
        in_specs=[col_spec, col_spec, col_spec, col_spec, bias_spec, bias_spec, seg_spec, nw_spec],
        out_specs=col_spec,
        scratch_shapes=[pltpu.VMEM((seq, LANES), F32),
                        pltpu.VMEM((ATTN_BLOCK + seq, LANES), F32),
                        pltpu.VMEM((ATTN_BLOCK + seq, LANES), F32),
                        pltpu.VMEM((seq, LANES), F32),
                        pltpu.VMEM((rows4, LANES), F32),
                        pltpu.VMEM((rows4, LANES), F32),
                        pltpu.VMEM((n_pat, seq, LANES), F32),
                        pltpu.VMEM((n_pat, seq, LANES), F32),
                        pltpu.VMEM((n_pat, seq, LANES), F32)],
        compiler_params=pltpu.CompilerParams(
            dimension_semantics=("parallel", "parallel"), vmem_limit_bytes=VMEM_LIMIT),
        name="dilated_attn",
    )(q, k, v, gate, bias, bias0, seg, norm_w)


def _out_proj_kernel(x_ref, ya_ref, yh_ref, fnw_ref, w_ref, out_ref, *, attn_width):
    mixed = jnp.dot(ya_ref[...], w_ref[0:attn_width, :].astype(BF16),
                    preferred_element_type=F32)
    mixed = mixed + jnp.dot(yh_ref[...], w_ref[attn_width:, :].astype(BF16),
                            preferred_element_type=F32)
    x = x_ref[...] + mixed
    ms = jnp.mean(x * x, axis=-1, keepdims=True)
    out_ref[...] = (x * lax.rsqrt(ms + NORM_EPS)) * fnw_ref[...]


def _out_proj(x2, ya, yh, fnw, w):
    n, d_model = x2.shape
    attn_width = ya.shape[-1]
    rows = OUT_ROWS
    row_spec = lambda c: pl.BlockSpec((rows, c), lambda i: (i, 0))
    full_spec = lambda a: pl.BlockSpec(a.shape, lambda i: (0,) * a.ndim)
    return pl.pallas_call(
        functools.partial(_out_proj_kernel, attn_width=attn_width),
        out_shape=jax.ShapeDtypeStruct((n, d_model), F32),
        grid=(n // rows,),
        in_specs=[row_spec(d_model), row_spec(attn_width), row_spec(yh.shape[-1]),
                  full_spec(fnw), full_spec(w)],
        out_specs=row_spec(d_model),
        compiler_params=pltpu.CompilerParams(
            dimension_semantics=("parallel",), vmem_limit_bytes=VMEM_LIMIT),
        name="out_proj",
    )(x2, ya, yh, fnw, w)


def kernel(x, positions, w_in, w_out, mix_norm_w, attn_out_norm_w, hgrn_out_norm_w,
           hgrn_lb_raw, final_norm_w):
    b, seq, d_model = x.shape
    depth = w_in.shape[0]
    attn_width = attn_out_norm_w.shape[-1]
    hgrn_width = hgrn_out_norm_w.shape[-1]
    assert depth == 1 and attn_width == hgrn_width and w_in.shape[-1] == 8 * attn_width
    assert seq % (max(d for _, d in DILATED_PATTERNS) * ATTN_BLOCK) == 0
    n = b * seq
    layer = 0
    x2 = x.reshape(n, d_model)
    q, k, v, ag, yh = _in_proj(
        x2, positions, mix_norm_w[layer][None], w_in[layer], hgrn_lb_raw,
        hgrn_out_norm_w[layer][None], attn_width, layer, seq)
    to3 = lambda t: t.reshape(b, seq, t.shape[-1])
    ya = _attention(to3(q), to3(k), to3(v), to3(ag), attn_out_norm_w[layer][None])
    out = _out_proj(x2, ya.reshape(n, attn_width), yh, final_norm_w[None],
                    w_out[layer])
    return out.reshape(b, seq, d_model)
```

```python
import functools

import numpy as np
import jax
import jax.numpy as jnp
from jax import lax
from jax.experimental import pallas as pl
from jax.experimental.pallas import tpu as pltpu

F32 = jnp.float32
BF16 = jnp.bfloat16

LANES = 128
SUBLANES = 8
ATTN_HEAD_DIM = 64
HGRN_HEAD_DIM = 128
DILATED_PATTERNS = ((128, 1), (512, 4), (2048, 16))
ATTN_BLOCK = 128
ATTN_DEINTERLEAVE = 4
ROPE_THETA = 500000.0
ROPE_DIMS = ATTN_HEAD_DIM // 4
HGRN_CHUNK = 64
NORM_EPS = 1e-6
LOG2_E = 1.4426950408889634
VMEM_LIMIT = 56 * 1024 * 1024

PROJ_ROWS = 1024
OUT_ROWS = 2048
ATTN_RUN = 32
ATTN_CHAINS = 32


def _silu(t):
    return t * jax.nn.sigmoid(t)


def _cumsum_rows(t):
    n_rows, width = t.shape
    row = lax.broadcasted_iota(jnp.int32, (n_rows, width), 0)
    shift = 1
    while shift < n_rows:
        if shift < SUBLANES:
            prev = jnp.where(row >= shift, pltpu.roll(t, shift, 0), 0.0)
        else:
            prev = jnp.concatenate([jnp.zeros((shift, width), t.dtype), t[:n_rows - shift]], axis=0)
        t = t + prev
        shift *= 2
    return t


def _hgrn_tile(hq, hf, hi, gate, lbraw_ref, norm_w_ref, state_ref, o_ref, layer, interleave):
    chunk = HGRN_CHUNK
    rows, width = hq.shape
    n_chunks = rows // chunk
    n_heads = width // HGRN_HEAD_DIM
    heads = [slice(h * HGRN_HEAD_DIM, (h + 1) * HGRN_HEAD_DIM) for h in range(n_heads)]
    chunks = [slice(c * chunk, (c + 1) * chunk) for c in range(n_chunks)]

    raw = lbraw_ref[...]
    e = jnp.exp(raw - jnp.max(raw, axis=0, keepdims=True))
    sm = e / jnp.sum(e, axis=0, keepdims=True)
    lb = jnp.sum(sm[0:layer + 1, :], axis=0, keepdims=True)

    ti = lax.broadcasted_iota(jnp.int32, (chunk, chunk), 0)
    si = lax.broadcasted_iota(jnp.int32, (chunk, chunk), 1)
    causal = ti >= si

    hkey, cum = [], []
    for rws in chunks:
        f = lb + (1.0 - lb) * jax.nn.sigmoid(hf[rws, :])
        hkey.append(1.0 - f)
        cum.append(_cumsum_rows(jnp.log(f)))
    for fill in interleave[0]:
        fill()

    q_dec, decay, att, upd = [], [], [], []
    for c, rws in enumerate(chunks):
        dec_c = jnp.exp(cum[c][chunk - 1:chunk, :])
        q_c = (hq[rws, :] * jnp.exp(cum[c])).astype(BF16)
        k_inv = hkey[c] * jnp.exp(-cum[c])
        k_end = (k_inv * dec_c).astype(BF16)
        k_inv = k_inv.astype(BF16)
        att.append([lax.dot_general(q_c[:, cs], k_inv[:, cs], (((1,), (1,)), ((), ())),
                                    preferred_element_type=F32) for cs in heads])
        upd.append([lax.dot_general(hi[rws, cs], k_end[:, cs], (((0,), (0,)), ((), ())),
                                    preferred_element_type=F32) for cs in heads])
        q_dec.append(q_c)
        decay.append(dec_c)
    for fill in interleave[1]:
        fill()

    o_intra = [[jnp.dot(jnp.where(causal, att[c][h], 0.0).astype(BF16), hi[rws, cs],
                        preferred_element_type=F32) for h, cs in enumerate(heads)]
               for c, rws in enumerate(chunks)]
    for fill in interleave[2]:
        fill()

    for h, cs in enumerate(heads):
        if h > 0:
            for fill in interleave[2 + h]:
                fill()
        st = state_ref[h]
        for c, rws in enumerate(chunks):
            o = o_intra[c][h] + lax.dot_general(
                q_dec[c][:, cs], st.astype(BF16), (((1,), (1,)), ((), ())),
                preferred_element_type=F32)
            ms = jnp.mean(o * o, axis=-1, keepdims=True)
            y = (o * lax.rsqrt(ms + NORM_EPS)) * norm_w_ref[:, cs] * gate[rws, cs]
            o_ref[rws, cs] = y.astype(o_ref.dtype)
            st = decay[c][:, cs] * st + upd[c][h]
        state_ref[h] = st


def _in_proj_kernel(x_ref, pos_ref, freq_ref, spread_ref, cos_base_ref, sgn_up_ref, sgn_dn_ref,
                    nw_ref, lbraw_ref, hnw_ref, w_ref,
                    q_ref, k_ref, v_ref, ag_ref, yh_ref, state_ref, wb_ref,
                    *, width, layer, steps_per_seq):
    @pl.when(pl.program_id(0) == 0)
    def _():
        wb_ref[...] = w_ref[...].astype(BF16)

    @pl.when(pl.program_id(0) % steps_per_seq == 0)
    def _():
        state_ref[...] = jnp.zeros(state_ref.shape, F32)

    pos = pos_ref[pl.ds(pl.program_id(0) // steps_per_seq, 1), :]
    _in_proj_tile(x_ref, pos, freq_ref, spread_ref, cos_base_ref, sgn_up_ref, sgn_dn_ref,
                  nw_ref, lbraw_ref, hnw_ref, wb_ref,
                  q_ref, k_ref, v_ref, ag_ref, yh_ref, state_ref, width, layer)


def _in_proj_tile(x_ref, pos, freq_ref, spread_ref, cos_base_ref, sgn_up_ref, sgn_dn_ref,
                  nw_ref, lbraw_ref, hnw_ref, w_ref,
                  q_ref, k_ref, v_ref, ag_ref, yh_ref, state_ref, width, layer):
    x = x_ref[...]
    ms = jnp.mean(x * x, axis=-1, keepdims=True)
    hn = ((x * lax.rsqrt(ms + NORM_EPS)) * nw_ref[...]).astype(BF16)

    ang = freq_ref[...] * pos.astype(F32)
    parts = []
    for t in (jnp.cos(ang), jnp.sin(ang)):
        hi = t.astype(BF16).astype(F32)
        parts += [hi, t - hi]
    tabs = lax.dot_general(jnp.concatenate(parts, axis=0).astype(BF16), spread_ref[...],
                           (((0,), (0,)), ((), ())), preferred_element_type=F32)
    cos = tabs[:, :LANES] + cos_base_ref[...]
    sin = tabs[:, LANES:]
    sgn_up = sgn_up_ref[...]
    sgn_dn = sgn_dn_ref[...]
    half = ROPE_DIMS // 2

    def rope(t):
        cols = []
        for j in range(t.shape[1] // LANES):
            tj = t[:, j * LANES:(j + 1) * LANES]
            up = pltpu.roll(tj, LANES - half, 1)
            dn = pltpu.roll(tj, half, 1)
            cols.append(tj * cos + sin * (up * sgn_up + dn * sgn_dn))
        return jnp.concatenate(cols, axis=1)

    def proj(g, part=None):
        lo, size = g * width, width
        if part is not None:
            lo, size = lo + part * (width // 2), width // 2
        return jnp.dot(hn, w_ref[:, lo:lo + size], preferred_element_type=F32)

    scale = ATTN_HEAD_DIM ** -0.5 * LOG2_E

    def emit(ref, g, post):
        def half(part):
            def fill():
                cols = slice(part * (width // 2), (part + 1) * (width // 2))
                ref[:, cols] = post(proj(g, part)).astype(BF16)
            return fill
        return half(0), half(1)

    q_lo, q_hi = emit(q_ref, 0, lambda t: rope(t) * scale)
    k_lo, k_hi = emit(k_ref, 1, rope)
    v_lo, v_hi = emit(v_ref, 2, lambda t: t)
    g_lo, g_hi = emit(ag_ref, 3, _silu)
    fills = [[q_lo, q_hi], [k_lo, k_hi], [v_lo], [v_hi], [g_lo], [g_hi]]

    hf = proj(5)
    hq = _silu(proj(4))
    hi = proj(6).astype(BF16)
    gate = _silu(proj(7))
    _hgrn_tile(hq, hf, hi, gate, lbraw_ref, hnw_ref, state_ref, yh_ref, layer, fills)


def _rope_tables():
    half = ROPE_DIMS // 2
    inv_freq = ROPE_THETA ** (-np.arange(half, dtype=np.float32) * (2.0 / ROPE_DIMS))
    d = np.arange(LANES) % ATTN_HEAD_DIM
    rotary = d < ROPE_DIMS
    lane_uses = (np.arange(half)[:, None] == (d % half)[None, :]) & rotary[None, :]
    spread = np.zeros((4 * half, 2 * LANES), np.float32)
    for t in range(4):
        table = t // 2
        spread[t * half:(t + 1) * half, table * LANES:(table + 1) * LANES] = lane_uses
    cos_base = np.where(rotary, 0.0, 1.0).astype(np.float32)
    sgn_up = np.where(d < half, -1.0, 0.0).astype(np.float32)
    sgn_dn = np.where((d >= half) & rotary, 1.0, 0.0).astype(np.float32)
    return (inv_freq.astype(np.float32)[:, None], jnp.asarray(spread, BF16), cos_base[None],
            sgn_up[None], sgn_dn[None])


def _in_proj(x2, positions, norm_w, w, lb_raw, hgrn_norm_w, width, layer, seq):
    n, d_model = x2.shape
    rows = PROJ_ROWS
    consts = _rope_tables()
    row_spec = lambda c: pl.BlockSpec((rows, c), lambda i: (i, 0))
    full_spec = lambda a: pl.BlockSpec(a.shape, lambda i: (0,) * a.ndim)
    steps_per_seq = seq // rows
    pos_spec = pl.BlockSpec((positions.shape[0], rows), lambda i: (0, i % steps_per_seq))
    out_sds = jax.ShapeDtypeStruct((n, width), BF16)
    n_out = 5
    small = [*consts, norm_w, lb_raw, hgrn_norm_w]
    return pl.pallas_call(
        functools.partial(_in_proj_kernel, width=width, layer=layer, steps_per_seq=steps_per_seq),
        out_shape=[out_sds] * n_out,
        grid=(n // rows,),
        in_specs=[row_spec(d_model), pos_spec] + [full_spec(a) for a in small] + [
            pl.BlockSpec(w.shape, lambda i: (0, 0), pipeline_mode=pl.Buffered(1))],
        out_specs=[row_spec(width)] * n_out,
        scratch_shapes=[pltpu.VMEM((width // HGRN_HEAD_DIM, HGRN_HEAD_DIM, HGRN_HEAD_DIM), F32),
                        pltpu.VMEM(w.shape, BF16)],
        compiler_params=pltpu.CompilerParams(
            dimension_semantics=("arbitrary",), vmem_limit_bytes=VMEM_LIMIT),
        name="in_proj",
    )(x2, positions, *small, w)


def _attn_kernel(q_ref, k_ref, v_ref, gate_ref, bias_ref, bias0_ref, seg_ref, nw_ref, o_ref,
                 qf, kf, vf, q4, k4, v4, o_pat, m_pat, l_pat, *, seq):
    blk = ATTN_BLOCK
    dint = ATTN_DEINTERLEAVE
    len4 = seq // dint
    pad4 = blk * max(d for _, d in DILATED_PATTERNS) // dint
    seg4 = pad4 + len4

    qf[...] = q_ref[...].astype(F32)
    for src, dst, dst4 in ((k_ref, kf, k4), (v_ref, vf, v4)):
        dst[0:blk, :] = jnp.zeros((blk, LANES), F32)
        dst[blk:blk + seq, :] = src[...].astype(F32)
        for r in range(dint):
            dst4[r * seg4:r * seg4 + pad4, :] = jnp.zeros((pad4, LANES), F32)
            dst4[r * seg4 + pad4:(r + 1) * seg4, :] = dst[pl.ds(blk + r, len4, stride=dint), :]
    for r in range(dint):
        q4[r * len4:(r + 1) * len4, :] = qf[pl.ds(r, len4, stride=dint), :]

    lane = lax.broadcasted_iota(jnp.int32, (1, LANES), 1)
    head_a = lane < ATTN_HEAD_DIM
    head_b = jnp.logical_not(head_a)

    def rows_of(start, size, stride):
        if stride == 1:
            return pl.ds(pl.multiple_of(start, blk), size)
        return pl.ds(start, size, stride=stride)

    for pat, (window, dil) in enumerate(DILATED_PATTERNS):
        assert window // dil == blk and (dil == 1 or dil % dint == 0)
        n_blk = seq // dil // blk
        n_run = min(ATTN_RUN, n_blk)
        n_res = min(ATTN_CHAINS // n_run, dil)
        runs_per_res = n_blk // n_run
        stride = 1 if dil == 1 else dil // dint
        q_src, k_src, v_src = (qf, kf, vf) if dil == 1 else (q4, k4, v4)

        def body(i, carry, pat=pat, dil=dil, n_run=n_run, n_res=n_res,
                 runs_per_res=runs_per_res, stride=stride, q_src=q_src, k_src=k_src, v_src=v_src):
            for j in range(n_res):
                r = (i // runs_per_res) * n_res + j
                n0 = (i % runs_per_res) * n_run
                if dil == 1:
                    q_start = blk * n0
                    k_start = blk + blk * (n0 - 1)
                else:
                    seg, off = r % dint, r // dint
                    q_start = seg * len4 + off + stride * blk * n0
                    k_start = seg * seg4 + pad4 + off + stride * blk * (n0 - 1)
                k_rows = rows_of(k_start, (n_run + 1) * blk, stride)
                kb = k_src[k_rows, :]
                k_a = jnp.where(head_a, kb, 0.0).astype(BF16)
                k_b = jnp.where(head_b, kb, 0.0).astype(BF16)
                vb = v_src[k_rows, :]
                v_a = jnp.where(head_a, vb, 1.0).astype(BF16)
                v_b = jnp.where(head_b, vb, 1.0).astype(BF16)
                qb = q_src[rows_of(q_start, n_run * blk, stride), :].astype(BF16)
                for u in range(n_run):
                    bias = bias_ref[...]
                    if u == 0:
                        bias = jnp.where(n0 == 0, bias0_ref[...], bias)
                    q_u = qb[u * blk:(u + 1) * blk]
                    keys = slice(u * blk, (u + 2) * blk)

                    def one_head(k_h, v_h):
                        s = lax.dot_general(q_u, k_h[keys], (((1,), (1,)), ((), ())),
                                            preferred_element_type=F32) + bias
                        m = jnp.max(s, axis=-1, keepdims=True)
                        p = jnp.exp2(s - m).astype(BF16)
                        return jnp.dot(p, v_h[keys], preferred_element_type=F32), m

                    ol_a, m_a = one_head(k_a, v_a)
                    ol_b, m_b = one_head(k_b, v_b)
                    rows = rows_of(q_start + u * stride * blk, blk, stride)
                    o_pat[pat, rows, :] = jnp.where(head_a, ol_a, ol_b)
                    m_pat[pat, rows, :] = jnp.where(head_a, m_a, m_b)
                    l_pat[pat, rows, :] = jnp.where(head_a, ol_b, ol_a)
            return carry

        lax.fori_loop(0, dil * n_blk // (n_run * n_res), body, 0)

    blocks_per_seg = len4 // blk

    def merge(i, carry):
        seg, c = i // blocks_per_seg, i % blocks_per_seg
        rows4 = pl.ds(pl.multiple_of(seg * len4 + c * blk, blk), blk)
        rows1 = pl.ds(seg + dint * blk * c, blk, stride=dint)
        rows = [rows1 if dil == 1 else rows4 for _, dil in DILATED_PATTERNS]
        ms = [m_pat[p, rw, :] for p, rw in enumerate(rows)]
        m_top = functools.reduce(jnp.maximum, ms)
        ws = [jnp.exp2(m - m_top) for m in ms]
        num = sum(w * o_pat[p, rw, :] for p, (w, rw) in enumerate(zip(ws, rows)))
        den = sum(w * pltpu.roll(l_pat[p, rw, :], ATTN_HEAD_DIM, 1)
                  for p, (w, rw) in enumerate(zip(ws, rows)))
        ssq = jnp.dot((num * num).astype(BF16), seg_ref[...], preferred_element_type=F32)
        out = num * lax.rsqrt(ssq * (1.0 / ATTN_HEAD_DIM) + NORM_EPS * (den * den))
        qf[rows1, :] = out
        return carry

    assert DILATED_PATTERNS[0][1] == 1
    lax.fori_loop(0, seq // blk, merge, 0, unroll=16)
    o_ref[...] = ((qf[...] * nw_ref[...]) * gate_ref[...].astype(F32)).astype(BF16)


def _band_bias():
    qi = np.arange(ATTN_BLOCK)[:, None]
    kj = np.arange(2 * ATTN_BLOCK)[None, :]
    dist = ATTN_BLOCK + qi - kj
    band = (dist >= 0) & (dist <= ATTN_BLOCK)
    first = band & (kj >= ATTN_BLOCK)
    to_bias = lambda mk: np.where(mk, 0.0, -np.inf).astype(np.float32)
    return to_bias(band), to_bias(first)


def _attention(q, k, v, gate, norm_w):
    b, seq, width = q.shape
    bias, bias0 = _band_bias()
    n_pat = len(DILATED_PATTERNS)
    pad4 = ATTN_BLOCK * max(d for _, d in DILATED_PATTERNS) // ATTN_DEINTERLEAVE
    rows4 = seq + ATTN_DEINTERLEAVE * pad4
    head_of_lane = np.arange(LANES) // ATTN_HEAD_DIM
    seg = jnp.asarray((head_of_lane[:, None] == head_of_lane[None, :]).astype(np.float32), BF16)
    col_spec = pl.BlockSpec((None, seq, LANES), lambda i, j: (i, 0, j))
    bias_spec = pl.BlockSpec(bias.shape, lambda i, j: (0, 0))
    seg_spec = pl.BlockSpec(seg.shape, lambda i, j: (0, 0))
    nw_spec = pl.BlockSpec((1, LANES), lambda i, j: (0, j))
    return pl.pallas_call(
        functools.partial(_attn_kernel, seq=seq),
        out_shape=jax.ShapeDtypeStruct((b, seq, width), BF16),
        grid=(b, width // LANES),
        in_specs=[col_spec, col_spec, col_spec, col_spec, bias_spec, bias_spec, seg_spec, nw_spec],
        out_specs=col_spec,
        scratch_shapes=[pltpu.VMEM((seq, LANES), F32),
                        pltpu.VMEM((ATTN_BLOCK + seq, LANES), F32),
                        pltpu.VMEM((ATTN_BLOCK + seq, LANES), F32),
                        pltpu.VMEM((seq, LANES), F32),
                        pltpu.VMEM((rows4, LANES), F32),
                        pltpu.VMEM((rows4, LANES), F32),
                        pltpu.VMEM((n_pat, seq, LANES), F32),
                        pltpu.VMEM((n_pat, seq, LANES), F32),
                        pltpu.VMEM((n_pat, seq, LANES), F32)],
        compiler_params=pltpu.CompilerParams(
            dimension_semantics=("parallel", "parallel"), vmem_limit_bytes=VMEM_LIMIT),
        name="dilated_attn",
    )(q, k, v, gate, bias, bias0, seg, norm_w)


def _out_proj_kernel(x_ref, ya_ref, yh_ref, fnw_ref, w_ref, out_ref, *, attn_width):
    mixed = jnp.dot(ya_ref[...], w_ref[0:attn_width, :].astype(BF16),
                    preferred_element_type=F32)
    mixed = mixed + jnp.dot(yh_ref[...], w_ref[attn_width:, :].astype(BF16),
                            preferred_element_type=F32)
    x = x_ref[...] + mixed
    ms = jnp.mean(x * x, axis=-1, keepdims=True)
    out_ref[...] = (x * lax.rsqrt(ms + NORM_EPS)) * fnw_ref[...]


def _out_proj(x2, ya, yh, fnw, w):
    n, d_model = x2.shape
    attn_width = ya.shape[-1]
    rows = OUT_ROWS
    row_spec = lambda c: pl.BlockSpec((rows, c), lambda i: (i, 0))
    full_spec = lambda a: pl.BlockSpec(a.shape, lambda i: (0,) * a.ndim)
    return pl.pallas_call(
        functools.partial(_out_proj_kernel, attn_width=attn_width),
        out_shape=jax.ShapeDtypeStruct((n, d_model), F32),
        grid=(n // rows,),
        in_specs=[row_spec(d_model), row_spec(attn_width), row_spec(yh.shape[-1]),
                  full_spec(fnw), full_spec(w)],
        out_specs=row_spec(d_model),
        compiler_params=pltpu.CompilerParams(
            dimension_semantics=("parallel",), vmem_limit_bytes=VMEM_LIMIT),
        name="out_proj",
    )(x2, ya, yh, fnw, w)


def kernel(x, positions, w_in, w_out, mix_norm_w, attn_out_norm_w, hgrn_out_norm_w,
           hgrn_lb_raw, final_norm_w):
    b, seq, d_model = x.shape
    depth = w_in.shape[0]
    attn_width = attn_out_norm_w.shape[-1]
    hgrn_width = hgrn_out_norm_w.shape[-1]
    assert depth == 1 and attn_width == hgrn_width and w_in.shape[-1] == 8 * attn_width
    assert seq % (max(d for _, d in DILATED_PATTERNS) * ATTN_BLOCK) == 0
    n = b * seq
    layer = 0
    x2 = x.reshape(n, d_model)
    q, k, v, ag, yh = _in_proj(
        x2, positions, mix_norm_w[layer][None], w_in[layer], hgrn_lb_raw,
        hgrn_out_norm_w[layer][None], attn_width, layer, seq)
    to3 = lambda t: t.reshape(b, seq, t.shape[-1])
    ya = _attention(to3(q), to3(k), to3(v), to3(ag), attn_out_norm_w[layer][None])
    out = _out_proj(x2, ya.reshape(n, attn_width), yh, final_norm_w[None],
                    w_out[layer])
    return out.reshape(b, seq, d_model)
```
